```python
import jax, jax.numpy as jnp
from jax import lax
import numpy as np

D_MODEL = 1024
BATCH = 16
SEQ = 2048
DEPTH = 1
DEC_BATCH = 2
DEC_SEQ = 16384
PAST_LEN = 128

GRID_W = 64
WIN_ROWS = 8
WIN_COLS = 16
NA_HEADS = 8
NA_HEAD_DIM = D_MODEL // 16
NA_WIDTH = NA_HEADS * NA_HEAD_DIM
F_GROUPS = 8
F_GROUP_DIM = D_MODEL // 16
F_WIDTH = F_GROUPS * F_GROUP_DIM
MIX_WIDTH = NA_WIDTH + F_WIDTH
IN_WIDTH = 4 * NA_WIDTH + 2 * F_WIDTH
Q_BLOCK_COLS = 16
N_COL_BLOCKS = GRID_W // Q_BLOCK_COLS
K_SLAB_COLS = 2 * Q_BLOCK_COLS
RMS_EPS = 1e-6
NEG_INF = -1e30

kernel_name = 'hybrid_na_fourier_encoder'


def rms_norm(x, g):
    xf = x.astype(jnp.float32)
    inv = lax.rsqrt(jnp.mean(xf * xf, axis=-1, keepdims=True) + RMS_EPS)
    return (xf * inv * g.astype(jnp.float32)).astype(x.dtype)


def _column_geometry():
    q_cols = np.arange(GRID_W).reshape(N_COL_BLOCKS, Q_BLOCK_COLS)
    win_start = np.clip(q_cols - WIN_COLS // 2, 0, GRID_W - WIN_COLS)
    slab_start = np.clip(np.arange(N_COL_BLOCKS) * Q_BLOCK_COLS - WIN_COLS // 2,
                         0, GRID_W - K_SLAB_COLS)
    slab_cols = slab_start[:, None] + np.arange(K_SLAB_COLS)[None, :]
    kc = slab_cols[:, None, :]
    valid = (kc >= win_start[:, :, None]) & (kc < win_start[:, :, None] + WIN_COLS)
    col_off = np.clip(kc - q_cols[:, :, None] + WIN_COLS - 1, 0, 2 * WIN_COLS - 2)
    return slab_cols, valid, col_off


def neighbourhood_attention(q, k, v, rpb):
    b, s, _ = q.shape
    rows = s // GRID_W
    kr = min(WIN_ROWS, rows)
    slab_cols, valid, col_off = _column_geometry()
    to_grid = lambda t: t.reshape(b, rows, GRID_W, NA_HEADS, NA_HEAD_DIM)
    qg, kg, vg = to_grid(q), to_grid(k), to_grid(v)
    bias_table = rpb.astype(jnp.float32)
    scale = NA_HEAD_DIM ** -0.5

    def one_row(r):
        rs = jnp.clip(r - kr // 2, 0, rows - kr)
        q_row = lax.dynamic_index_in_dim(qg, r, axis=1, keepdims=False)
        q_blk = q_row.reshape(b, N_COL_BLOCKS, Q_BLOCK_COLS, NA_HEADS, NA_HEAD_DIM)
        k_slab = lax.dynamic_slice_in_dim(kg, rs, kr, axis=1)[:, :, slab_cols]
        v_slab = lax.dynamic_slice_in_dim(vg, rs, kr, axis=1)[:, :, slab_cols]
        logits = jnp.einsum('bjqhd,bijkhd->bhjqik', q_blk, k_slab).astype(jnp.float32) * scale
        row_off = rs + jnp.arange(kr) - r + WIN_ROWS - 1
        bias = bias_table[:, row_off[None, None, :, None], col_off[:, :, None, :]]
        logits = jnp.where(valid[:, :, None, :], logits + bias, NEG_INF)
        shp = logits.shape
        p = jax.nn.softmax(logits.reshape(shp[:4] + (kr * K_SLAB_COLS,)), axis=-1).reshape(shp)
        out = jnp.einsum('bhjqik,bijkhd->bjqhd', p.astype(v.dtype), v_slab)
        return out.reshape(b, GRID_W, NA_WIDTH)

    out = lax.map(one_row, jnp.arange(rows))
    return jnp.swapaxes(out, 0, 1).reshape(b, s, NA_WIDTH)


def fourier_mix(u, w_f, b_f):
    b, s, _ = u.shape
    ug = u.astype(jnp.float32).reshape(b, s, F_GROUPS, F_GROUP_DIM)
    spec = jnp.fft.fft2(ug, axes=(1, 3), norm='ortho')
    mixed = spec.real.reshape(b, s, F_WIDTH).astype(u.dtype)
    return mixed @ w_f + b_f


def encoder_layer(x, w_in, rpb, w_fourier, b_fourier, g_pre, g_na, g_f, w_out, g_post):
    h = rms_norm(x, g_pre)
    proj = h @ w_in
    q, k, v, z_a, u_f, z_f = jnp.split(
        proj, [NA_WIDTH, 2 * NA_WIDTH, 3 * NA_WIDTH, 4 * NA_WIDTH, 4 * NA_WIDTH + F_WIDTH], axis=-1)
    y_a = neighbourhood_attention(q, k, v, rpb)
    y_f = fourier_mix(u_f, w_fourier, b_fourier)
    mixed = jnp.concatenate([rms_norm(y_a, g_na) * jax.nn.silu(z_a),
                             rms_norm(y_f, g_f) * jax.nn.silu(z_f)], axis=-1)
    out = mixed @ w_out
    return x + rms_norm(out, g_post)


def setup_inputs(seed: int = 0) -> dict:
    key = jax.random.key(seed)
    ks = jax.random.split(key, 12)
    f32 = jnp.float32
    nrm = lambda k, shp: jax.random.normal(k, shp, f32)
    return {
        'x_prompt': nrm(ks[0], (BATCH, SEQ, D_MODEL)),
        'x_sample': nrm(ks[1], (DEC_BATCH, DEC_SEQ, D_MODEL)),
        'w_in': nrm(ks[2], (DEPTH, D_MODEL, IN_WIDTH)) * D_MODEL ** -0.5,
        'rpb': nrm(ks[3], (DEPTH, NA_HEADS, 2 * WIN_ROWS - 1, 2 * WIN_COLS - 1)) * 0.5,
        'w_fourier': nrm(ks[4], (DEPTH, F_WIDTH, F_WIDTH)) * F_WIDTH ** -0.5,
        'b_fourier': nrm(ks[5], (DEPTH, F_WIDTH)) * 0.01,
        'g_pre': 1.0 + 0.1 * nrm(ks[6], (DEPTH, D_MODEL)),
        'g_na': 1.0 + 0.1 * nrm(ks[7], (DEPTH, NA_WIDTH)),
        'g_f': 1.0 + 0.1 * nrm(ks[8], (DEPTH, F_WIDTH)),
        'w_out': nrm(ks[9], (DEPTH, MIX_WIDTH, D_MODEL)) * MIX_WIDTH ** -0.5,
        'g_post': 1.0 + 0.1 * nrm(ks[10], (DEPTH, D_MODEL)),
    }


def reference(x_prompt, x_sample, w_in, rpb, w_fourier, b_fourier, g_pre, g_na, g_f, w_out, g_post):
    y_prompt = x_prompt
    y_sample = x_sample
    for l in range(DEPTH):
        y_prompt = encoder_layer(y_prompt, w_in[l], rpb[l], w_fourier[l], b_fourier[l],
                                 g_pre[l], g_na[l], g_f[l], w_out[l], g_post[l])
        y_sample = encoder_layer(y_sample, w_in[l], rpb[l], w_fourier[l], b_fourier[l],
                                 g_pre[l], g_na[l], g_f[l], w_out[l], g_post[l])
    return (y_prompt, y_sample)
```

```python
import functools

import numpy as np
import jax
import jax.numpy as jnp
from jax import lax
from jax.experimental import pallas as pl
from jax.experimental.pallas import tpu as pltpu

D_MODEL = 1024
GRID_W = 64
WIN_ROWS = 8
WIN_COLS = 16
NA_HEADS = 8
NA_HEAD_DIM = 64
NA_WIDTH = NA_HEADS * NA_HEAD_DIM
F_GROUPS = 8
F_GROUP_DIM = 64
F_WIDTH = F_GROUPS * F_GROUP_DIM
RMS_EPS = 1e-6
NEG_INF = -1e30

Q_ROWS = 4
Q_TOK = Q_ROWS * GRID_W
KV_GROUPS = 3
KV_TOK = KV_GROUPS * Q_TOK
DFT_MINOR = 128
VMEM_LIMIT = 56 * 1024 * 1024

BF16 = jnp.bfloat16
F32 = jnp.float32


def _rms(x, g):
    inv = lax.rsqrt(jnp.mean(x * x, axis=-1, keepdims=True) + RMS_EPS)
    return x * inv * g


def _silu(z):
    return z * (1.0 / (1.0 + jnp.exp(-z)))


def _in_proj_kernel(x_ref, g_ref, wn_ref, wt_ref,
                    k_ref, za_ref, uf_ref, zf_ref, qt_ref, vt_ref):
    h = _rms(x_ref[...], g_ref[...]).astype(BF16)
    nat = jnp.dot(h, wn_ref[...], preferred_element_type=F32)
    k_ref[...] = nat[:, 0 * NA_WIDTH:1 * NA_WIDTH].astype(BF16)
    za_ref[...] = nat[:, 1 * NA_WIDTH:2 * NA_WIDTH].astype(BF16)
    uf_ref[...] = nat[:, 2 * NA_WIDTH:3 * NA_WIDTH].astype(BF16)
    zf_ref[...] = nat[:, 3 * NA_WIDTH:4 * NA_WIDTH].astype(BF16)
    tr = lax.dot_general(wt_ref[...], h, (((1,), (1,)), ((), ())),
                         preferred_element_type=F32)
    qt_ref[...] = tr[:NA_WIDTH].astype(BF16)
    vt_ref[...] = tr[NA_WIDTH:].astype(BF16)


def _in_proj(x2d, g_pre, w_nat, w_tr, tm=512):
    t = x2d.shape[0]
    tok = lambda i: (i, 0)
    tok_t = lambda i: (0, i)
    const = lambda i: (0, 0)
    nat_shape = jax.ShapeDtypeStruct((t, NA_WIDTH), BF16)
    tr_shape = jax.ShapeDtypeStruct((NA_WIDTH, t), BF16)
    return pl.pallas_call(
        _in_proj_kernel,
        grid=(t // tm,),
        in_specs=[pl.BlockSpec((tm, D_MODEL), tok),
                  pl.BlockSpec((1, D_MODEL), const),
                  pl.BlockSpec(w_nat.shape, const),
                  pl.BlockSpec(w_tr.shape, const)],
        out_specs=[pl.BlockSpec((tm, NA_WIDTH), tok)] * 4
                  + [pl.BlockSpec((NA_WIDTH, tm), tok_t)] * 2,
        out_shape=[nat_shape] * 4 + [tr_shape] * 2,
        compiler_params=pltpu.CompilerParams(
            dimension_semantics=("arbitrary",), vmem_limit_bytes=VMEM_LIMIT),
        name="in_proj",
    )(x2d, g_pre, w_nat, w_tr)


def _attention_kernel(qt_ref, k0_ref, k1_ref, k2_ref, v0_ref, v1_ref, v2_ref,
                      za_ref, bias_ref, g_ref, out_ref, ot_ref):
    k_refs = (k0_ref, k1_ref, k2_ref)
    v_refs = (v0_ref, v1_ref, v2_ref)
    lanes = 2 * NA_HEAD_DIM
    row_head = lax.broadcasted_iota(jnp.int32, (lanes, Q_TOK), 0) // NA_HEAD_DIM
    for pair in range(NA_HEADS // 2):
        cols = slice(pair * lanes, (pair + 1) * lanes)
        k_pair = jnp.concatenate([r[:, cols] for r in k_refs], axis=0)
        qt_pair = qt_ref[cols, :]
        for sub in range(2):
            head = 2 * pair + sub
            qt_head = jnp.where(row_head == sub, qt_pair, jnp.zeros_like(qt_pair))
            s = jnp.dot(k_pair, qt_head, preferred_element_type=F32)
            s = s + bias_ref[0, head]
            m = jnp.max(s, axis=0, keepdims=True)
            p = jnp.exp(s - m)
            denom = jnp.sum(p, axis=0, keepdims=True)
            rows = slice(head * NA_HEAD_DIM, (head + 1) * NA_HEAD_DIM)
            vt_head = jnp.concatenate([r[rows, :] for r in v_refs], axis=1)
            o_t = jnp.dot(vt_head, p.astype(BF16), preferred_element_type=F32)
            ot_ref[rows, :] = o_t / denom
    y = ot_ref[...].T
    gated = _rms(y, g_ref[...]) * _silu(za_ref[...].astype(F32))
    out_ref[...] = gated.astype(BF16)


def _attention(qt, k, vt, za, bias, g_na, batch, seq):
    t = batch * seq
    groups = seq // Q_TOK
    assert groups >= KV_GROUPS

    def first_kv(u):
        return jnp.clip(u - 1, 0, groups - KV_GROUPS)

    def tok(u, b):
        return (b * groups + u, 0)

    def tok_t(u, b):
        return (0, b * groups + u)

    def kv(i):
        return lambda u, b: (b * groups + first_kv(u) + i, 0)

    def kv_t(i):
        return lambda u, b: (0, b * groups + first_kv(u) + i)

    def variant(u, b):
        return (jnp.where(u == 0, 0, jnp.where(u == groups - 1, 2, 1)), 0, 0, 0)

    const = lambda u, b: (0, 0)
    return pl.pallas_call(
        _attention_kernel,
        grid=(groups, batch),
        in_specs=[pl.BlockSpec((NA_WIDTH, Q_TOK), tok_t)]
                 + [pl.BlockSpec((Q_TOK, NA_WIDTH), kv(i)) for i in range(KV_GROUPS)]
                 + [pl.BlockSpec((NA_WIDTH, Q_TOK), kv_t(i)) for i in range(KV_GROUPS)]
                 + [pl.BlockSpec((Q_TOK, NA_WIDTH), tok),
                    pl.BlockSpec((1, NA_HEADS, KV_TOK, Q_TOK), variant),
                    pl.BlockSpec((1, NA_WIDTH), const)],
        out_specs=pl.BlockSpec((Q_TOK, NA_WIDTH), tok),
        out_shape=jax.ShapeDtypeStruct((t, NA_WIDTH), BF16),
        scratch_shapes=[pltpu.VMEM((NA_WIDTH, Q_TOK), F32)],
        compiler_params=pltpu.CompilerParams(
            dimension_semantics=("arbitrary", "arbitrary"),
            vmem_limit_bytes=VMEM_LIMIT),
        name="attention",
    )(qt, k, k, k, vt, vt, vt, za, bias, g_na)


def _attention_bias(rpb):
    kc = np.arange(GRID_W)[:, None]
    qc = np.arange(GRID_W)[None, :]
    win_start = np.clip(qc - WIN_COLS // 2, 0, GRID_W - WIN_COLS)
    col_valid = (kc >= win_start) & (kc < win_start + WIN_COLS)
    col_off = np.clip(kc - qc + WIN_COLS - 1, 0, 2 * WIN_COLS - 2)
    toeplitz = jnp.where(col_valid[None, None], rpb.astype(F32)[:, :, col_off], NEG_INF)

    key_rows = KV_GROUPS * Q_ROWS
    row_off = np.zeros((3, key_rows, Q_ROWS), np.int32)
    row_valid = np.zeros((3, key_rows, Q_ROWS), bool)
    for var in range(3):
        for j in range(Q_ROWS):
            start = {0: 0, 1: j, 2: key_rows - WIN_ROWS}[var]
            q_row = {0: j, 1: Q_ROWS + j, 2: key_rows - Q_ROWS + j}[var]
            for i in range(key_rows):
                row_valid[var, i, j] = start <= i < start + WIN_ROWS
                row_off[var, i, j] = np.clip(i - q_row + WIN_ROWS - 1, 0, 2 * WIN_ROWS - 2)
    full = toeplitz[:, row_off]
    full = jnp.where(row_valid[None, :, :, :, None, None], full, NEG_INF)
    full = jnp.transpose(full, (1, 0, 2, 4, 3, 5))
    return full.reshape(3, NA_HEADS, KV_TOK, Q_TOK)


def _fourier_1_kernel(w_ref, u_ref, a_ref):
    a_ref[0] = jnp.dot(w_ref[...], u_ref[0], preferred_element_type=F32).astype(BF16)


def _fourier_1(u3, w1, tn=8192):
    batch, major, width = u3.shape
    return pl.pallas_call(
        _fourier_1_kernel,
        grid=(batch, width // tn),
        in_specs=[pl.BlockSpec(w1.shape, lambda b, j: (0, 0)),
                  pl.BlockSpec((1, major, tn), lambda b, j: (b, 0, j))],
        out_specs=pl.BlockSpec((1, 2 * major, tn), lambda b, j: (b, 0, j)),
        out_shape=jax.ShapeDtypeStruct((batch, 2 * major, width), BF16),
        compiler_params=pltpu.CompilerParams(
            dimension_semantics=("arbitrary", "arbitrary"),
            vmem_limit_bytes=VMEM_LIMIT),
        name="fourier_1",
    )(w1, u3)


def _fourier_2_kernel(a_ref, m_ref, cs_ref, wf_ref, bf_ref, y_ref, *, tr):
    for j in range(tr):
        rhs = jnp.concatenate([a_ref[0, 0, j], a_ref[0, 1, j]], axis=0)
        x = jnp.dot(m_ref[j], rhs, preferred_element_type=F32)
        xc = jnp.concatenate([x[:DFT_MINOR], x[DFT_MINOR:]], axis=1).astype(BF16)
        mixed = jnp.dot(xc, cs_ref[...], preferred_element_type=F32)
        y = jnp.dot(mixed.astype(BF16), wf_ref[...], preferred_element_type=F32) + bf_ref[...]
        y_ref[0, j] = y.astype(BF16)


def _fourier_2(a5, m_tab, cs, w_f, b_f, tr=8):
    batch, _, major, _, _ = a5.shape
    tr = min(tr, major)
    return pl.pallas_call(
        functools.partial(_fourier_2_kernel, tr=tr),
        grid=(batch, major // tr),
        in_specs=[pl.BlockSpec((1, 2, tr, DFT_MINOR, F_WIDTH), lambda b, r: (b, 0, r, 0, 0)),
                  pl.BlockSpec((tr, 2 * DFT_MINOR, 2 * DFT_MINOR), lambda b, r: (r, 0, 0)),
                  pl.BlockSpec(cs.shape, lambda b, r: (0, 0)),
                  pl.BlockSpec(w_f.shape, lambda b, r: (0, 0)),
                  pl.BlockSpec((1, F_WIDTH), lambda b, r: (0, 0))],
        out_specs=pl.BlockSpec((1, tr, DFT_MINOR, F_WIDTH), lambda b, r: (b, r, 0, 0)),
        out_shape=jax.ShapeDtypeStruct((batch, major, DFT_MINOR, F_WIDTH), BF16),
        compiler_params=pltpu.CompilerParams(
            dimension_semantics=("arbitrary", "arbitrary"),
            vmem_limit_bytes=VMEM_LIMIT),
        name="fourier_2",
    )(a5, m_tab, cs, w_f, b_f)


@functools.lru_cache(maxsize=None)
def _dft_constants(seq):
    major = seq // DFT_MINOR
    a = np.arange(major)
    ang1 = 2.0 * np.pi * np.outer(a, a) / major
    w1 = np.concatenate([np.cos(ang1), -np.sin(ang1)], axis=0)
    r = np.arange(major)[:, None, None]
    p = np.arange(DFT_MINOR)[None, :, None]
    b = np.arange(DFT_MINOR)[None, None, :]
    ang2 = 2.0 * np.pi * ((b * (r + major * p)) % seq) / seq
    scale = 1.0 / np.sqrt(seq * F_GROUP_DIM)
    e_re, e_im = np.cos(ang2) * scale, -np.sin(ang2) * scale
    m_tab = np.concatenate([np.concatenate([e_re, -e_im], axis=2),
                            np.concatenate([e_im, e_re], axis=2)], axis=1)
    c = np.arange(F_GROUP_DIM)
    ang3 = 2.0 * np.pi * np.outer(c, c) / F_GROUP_DIM
    eye = np.eye(F_GROUPS)
    cs = np.concatenate([np.kron(eye, np.cos(ang3)), np.kron(eye, np.sin(ang3))], axis=0)
    return (np.asarray(w1, np.float32), np.asarray(m_tab, np.float32), np.asarray(cs, np.float32))


def _fourier(u, w_f, b_f, batch, seq):
    major = seq // DFT_MINOR
    w1, m_tab, cs = _dft_constants(seq)
    a = _fourier_1(u.reshape(batch, major, DFT_MINOR * F_WIDTH), jnp.asarray(w1).astype(BF16))
    y_perm = _fourier_2(a.reshape(batch, 2, major, DFT_MINOR, F_WIDTH),
                        jnp.asarray(m_tab).astype(BF16), jnp.asarray(cs).astype(BF16), w_f, b_f)
    return jnp.swapaxes(y_perm, 1, 2).reshape(batch * seq, F_WIDTH)


def _out_proj_kernel(ma_ref, yf_ref, zf_ref, x_ref, wa_ref, wf_ref, gf_ref, gp_ref, o_ref):
    mixed_f = _rms(yf_ref[...].astype(F32), gf_ref[...]) * _silu(zf_ref[...].astype(F32))
    out = jnp.dot(ma_ref[...], wa_ref[...], preferred_element_type=F32)
    out = out + jnp.dot(mixed_f.astype(BF16), wf_ref[...], preferred_element_type=F32)
    o_ref[...] = x_ref[...] + _rms(out, gp_ref[...])


def _out_proj(mixed_a, y_f, z_f, x2d, w_a, w_fo, g_f, g_post, tm=512):
    t = x2d.shape[0]
    tok = lambda i: (i, 0)
    const = lambda i: (0, 0)
    return pl.pallas_call(
        _out_proj_kernel,
        grid=(t // tm,),
        in_specs=[pl.BlockSpec((tm, NA_WIDTH), tok),
                  pl.BlockSpec((tm, F_WIDTH), tok),
                  pl.BlockSpec((tm, F_WIDTH), tok),
                  pl.BlockSpec((tm, D_MODEL), tok),
                  pl.BlockSpec(w_a.shape, const),
                  pl.BlockSpec(w_fo.shape, const),
                  pl.BlockSpec((1, F_WIDTH), const),
                  pl.BlockSpec((1, D_MODEL), const)],
        out_specs=pl.BlockSpec((tm, D_MODEL), tok),
        out_shape=jax.ShapeDtypeStruct((t, D_MODEL), F32),
        compiler_params=pltpu.CompilerParams(
            dimension_semantics=("arbitrary",), vmem_limit_bytes=VMEM_LIMIT),
        name="out_proj",
    )(mixed_a, y_f, z_f, x2d, w_a, w_fo, g_f, g_post)


def _encoder_layer(x, w_nat, w_tr, bias, w_f, b_f, g_pre, g_na, g_f, w_oa, w_of, g_post):
    batch, seq, _ = x.shape
    x2d = x.reshape(batch * seq, D_MODEL)
    k, z_a, u_f, z_f, q_t, v_t = _in_proj(x2d, g_pre, w_nat, w_tr)
    mixed_a = _attention(q_t, k, v_t, z_a, bias, g_na, batch, seq)
    y_f = _fourier(u_f, w_f, b_f, batch, seq)
    out = _out_proj(mixed_a, y_f, z_f, x2d, w_oa, w_of, g_f, g_post)
    return out.reshape(batch, seq, D_MODEL)


def kernel(x_prompt, x_sample, w_in, rpb, w_fourier, b_fourier, g_pre, g_na, g_f, w_out, g_post):
    depth = w_in.shape[0]
    y_prompt, y_sample = x_prompt, x_sample
    scale = NA_HEAD_DIM ** -0.5
    for l in range(depth):
        w = w_in[l]
        w_q, w_k, w_v, w_za, w_uf, w_zf = (w[:, i * NA_WIDTH:(i + 1) * NA_WIDTH] for i in range(6))
        w_nat = jnp.concatenate([w_k, w_za, w_uf, w_zf], axis=1).astype(BF16)
        w_tr = jnp.concatenate([w_q * scale, w_v], axis=1).T.astype(BF16)
        bias = _attention_bias(rpb[l])
        row = lambda v: v.reshape(1, -1).astype(F32)
        args = (w_nat, w_tr, bias, w_fourier[l].astype(BF16), row(b_fourier[l]),
                row(g_pre[l]), row(g_na[l]), row(g_f[l]),
                w_out[l][:NA_WIDTH].astype(BF16), w_out[l][NA_WIDTH:].astype(BF16), row(g_post[l]))
        y_prompt = _encoder_layer(y_prompt, *args)
        y_sample = _encoder_layer(y_sample, *args)
    return (y_prompt, y_sample)
```

```python
import functools

import numpy as np
import jax
import jax.numpy as jnp
from jax import lax
from jax.experimental import pallas as pl
from jax.experimental.pallas import tpu as pltpu

D_MODEL = 1024
GRID_W = 64
WIN_ROWS = 8
WIN_COLS = 16
NA_HEADS = 8
NA_HEAD_DIM = 64
NA_WIDTH = NA_HEADS * NA_HEAD_DIM
F_GROUPS = 8
F_GROUP_DIM = 64
F_WIDTH = F_GROUPS * F_GROUP_DIM
RMS_EPS = 1e-6
NEG_INF = -1e30

Q_ROWS = 4
Q_TOK = Q_ROWS * GRID_W
KV_GROUPS = 3
KV_TOK = KV_GROUPS * Q_TOK
KEY_ROWS = KV_GROUPS * Q_ROWS
ROW_SLOTS = 2 * WIN_ROWS
DFT_MINOR = 128
VMEM_LIMIT = 56 * 1024 * 1024

BF16 = jnp.bfloat16
F32 = jnp.float32


def _rms(x, g):
    inv = lax.rsqrt(jnp.mean(x * x, axis=-1, keepdims=True) + RMS_EPS)
    return x * inv * g


def _silu(z):
    return z * (1.0 / (1.0 + jnp.exp(-z)))


def _in_proj_kernel(x_ref, g_ref, wn_ref, wt_ref,
                    k_ref, za_ref, uf_ref, zf_ref, qt_ref, vt_ref):
    h = _rms(x_ref[...], g_ref[...]).astype(BF16)
    nat = jnp.dot(h, wn_ref[...], preferred_element_type=F32)
    k_ref[...] = nat[:, 0 * NA_WIDTH:1 * NA_WIDTH].astype(BF16)
    za_ref[...] = nat[:, 1 * NA_WIDTH:2 * NA_WIDTH].astype(BF16)
    uf_ref[...] = nat[:, 2 * NA_WIDTH:3 * NA_WIDTH].astype(BF16)
    zf_ref[...] = nat[:, 3 * NA_WIDTH:4 * NA_WIDTH].astype(BF16)
    tr = lax.dot_general(wt_ref[...], h, (((1,), (1,)), ((), ())),
                         preferred_element_type=F32)
    qt_ref[...] = tr[:NA_WIDTH].astype(BF16)
    vt_ref[...] = tr[NA_WIDTH:].astype(BF16)


def _in_proj(x2d, g_pre, w_nat, w_tr, tm=512):
    t = x2d.shape[0]
    tok = lambda i: (i, 0)
    tok_t = lambda i: (0, i)
    const = lambda i: (0, 0)
    nat_shape = jax.ShapeDtypeStruct((t, NA_WIDTH), BF16)
    tr_shape = jax.ShapeDtypeStruct((NA_WIDTH, t), BF16)
    return pl.pallas_call(
        _in_proj_kernel,
        grid=(t // tm,),
        in_specs=[pl.BlockSpec((tm, D_MODEL), tok),
                  pl.BlockSpec((1, D_MODEL), const),
                  pl.BlockSpec(w_nat.shape, const),
                  pl.BlockSpec(w_tr.shape, const)],
        out_specs=[pl.BlockSpec((tm, NA_WIDTH), tok)] * 4
                  + [pl.BlockSpec((NA_WIDTH, tm), tok_t)] * 2,
        out_shape=[nat_shape] * 4 + [tr_shape] * 2,
        compiler_params=pltpu.CompilerParams(
            dimension_semantics=("arbitrary",), vmem_limit_bytes=VMEM_LIMIT),
        name="in_proj",
    )(x2d, g_pre, w_nat, w_tr)


def _bias_table_kernel(rpb_ref, t2_ref):
    head = pl.program_id(0)
    shape = (GRID_W, 2 * GRID_W)
    kc = lax.broadcasted_iota(jnp.int32, shape, 0)
    lane = lax.broadcasted_iota(jnp.int32, shape, 1)
    second = lane >= GRID_W
    qc = jnp.where(second, lane - GRID_W, lane)
    col_off = kc - qc + (WIN_COLS - 1)
    win_start = jnp.clip(qc - WIN_COLS // 2, 0, GRID_W - WIN_COLS)
    col_valid = (kc >= win_start) & (kc < win_start + WIN_COLS)
    n_off = 2 * WIN_COLS - 1
    n_row = 2 * WIN_ROWS - 1

    def entry(d, o):
        if 0 <= d < n_row:
            return rpb_ref[(head * n_row + d) * n_off + o]
        return jnp.float32(NEG_INF)

    for d in range(ROW_SLOTS):
        tile = jnp.full(shape, NEG_INF, F32)
        for o in range(n_off):
            val = jnp.where(second, entry(d - 1, o), entry(d, o))
            tile = jnp.where(col_off == o, val, tile)
        t2_ref[0, d] = jnp.where(col_valid, tile, NEG_INF)


def _bias_table(rpb):
    return pl.pallas_call(
        _bias_table_kernel,
        grid=(NA_HEADS,),
        in_specs=[pl.BlockSpec(memory_space=pltpu.SMEM)],
        out_specs=pl.BlockSpec((1, ROW_SLOTS, GRID_W, 2 * GRID_W), lambda h: (h, 0, 0, 0)),
        out_shape=jax.ShapeDtypeStruct((NA_HEADS, ROW_SLOTS, GRID_W, 2 * GRID_W), F32),
        compiler_params=pltpu.CompilerParams(dimension_semantics=("arbitrary",)),
        name="bias_table",
    )(rpb.astype(F32).reshape(-1))


def _token_range(refs, lo, hi, other, axis):
    pieces = []
    for idx, ref in enumerate(refs):
        a, b = max(lo, idx * Q_TOK), min(hi, (idx + 1) * Q_TOK)
        if a < b:
            tok = slice(a - idx * Q_TOK, b - idx * Q_TOK)
            pieces.append(ref[tok, other] if axis == 0 else ref[other, tok])
    return pieces[0] if len(pieces) == 1 else jnp.concatenate(pieces, axis=axis)


def _window_plan(variant, jp):
    if variant == 0:
        return [(i, i - 2 * jp + WIN_ROWS - 1, None) for i in range(WIN_ROWS)]
    if variant == 2:
        q_row = KEY_ROWS - Q_ROWS + 2 * jp
        return [(i, i - q_row + WIN_ROWS - 1, None) for i in range(KEY_ROWS - WIN_ROWS, KEY_ROWS)]
    q_row = Q_ROWS + 2 * jp
    plan = []
    for i in range(q_row - WIN_ROWS // 2, q_row + WIN_ROWS // 2 + 1):
        half = "low" if i == q_row - WIN_ROWS // 2 else "high" if i == q_row + WIN_ROWS // 2 else None
        plan.append((i, i - q_row + WIN_ROWS - 1, half))
    return plan


def _attention_kernel(qt_ref, k0_ref, k1_ref, k2_ref, v0_ref, v1_ref, v2_ref,
                      za_ref, t2_ref, g_ref, out_ref, sb_ref, p_ref, ot_ref, *, groups):
    k_refs = (k0_ref, k1_ref, k2_ref)
    v_refs = (v0_ref, v1_ref, v2_ref)
    group = pl.program_id(0)
    pair_lanes = 2 * NA_HEAD_DIM
    q_lanes = 2 * GRID_W
    n_pairs = NA_HEADS // 2

    def body(variant):
        plans = [_window_plan(variant, jp) for jp in range(Q_ROWS // 2)]
        key_lo = min(p[0][0] for p in plans)
        key_hi = max(p[-1][0] for p in plans) + 1
        if (key_hi - key_lo) % 2:
            key_hi += 1
        lane = lax.broadcasted_iota(jnp.int32, (1, q_lanes), 1)
        half_mask = {"low": jnp.where(lane < GRID_W, 0.0, NEG_INF).astype(F32),
                     "high": jnp.where(lane >= GRID_W, 0.0, NEG_INF).astype(F32)}
        ones_rows = jnp.ones((16, (key_hi - key_lo) * GRID_W), BF16)
        zero_blk = jnp.zeros((GRID_W, q_lanes), BF16)

        def scores(pair, slot):
            cols = slice(pair * pair_lanes, (pair + 1) * pair_lanes)
            k_pair = _token_range(k_refs, key_lo * GRID_W, key_hi * GRID_W, cols, 0)
            qt_pair = qt_ref[cols, :]
            zero = jnp.zeros((NA_HEAD_DIM, Q_TOK), BF16)
            qt_both = jnp.concatenate(
                [jnp.concatenate([qt_pair[:NA_HEAD_DIM], zero], axis=0),
                 jnp.concatenate([zero, qt_pair[NA_HEAD_DIM:]], axis=0)], axis=1)
            s = jnp.dot(k_pair, qt_both, preferred_element_type=F32)
            maxima = []
            for sub in range(2):
                for jp, plan in enumerate(plans):
                    lanes = slice(sub * Q_TOK + jp * q_lanes, sub * Q_TOK + (jp + 1) * q_lanes)
                    m_acc = None
                    for i, slot_d, half in plan:
                        rows = slice((i - key_lo) * GRID_W, (i - key_lo + 1) * GRID_W)
                        blk = s[rows, lanes] + t2_ref[2 * pair + sub, slot_d]
                        if half is not None:
                            blk = blk + half_mask[half]
                        sb_ref[slot, rows, lanes] = blk
                        m_acc = blk if m_acc is None else jnp.maximum(m_acc, blk)
                    maxima.append(jnp.max(m_acc, axis=0, keepdims=True))
            return maxima

        def softmax_pv(pair, slot, maxima):
            n_keys = (key_hi - key_lo) * GRID_W
            for sub in range(2):
                head = 2 * pair + sub
                for jp, plan in enumerate(plans):
                    lanes = slice(sub * Q_TOK + jp * q_lanes, sub * Q_TOK + (jp + 1) * q_lanes)
                    m = maxima[2 * sub + jp]
                    live = {i for i, _, _ in plan}
                    for i in range(key_lo, key_hi):
                        rows = slice((i - key_lo) * GRID_W, (i - key_lo + 1) * GRID_W)
                        if i in live:
                            p_ref[slot, rows, lanes] = jnp.exp((sb_ref[slot, rows, lanes] - m).astype(BF16))
                        else:
                            p_ref[slot, rows, lanes] = zero_blk
                rows = slice(head * NA_HEAD_DIM, (head + 1) * NA_HEAD_DIM)
                vt_head = _token_range(v_refs, key_lo * GRID_W, key_hi * GRID_W, rows, 1)
                vt_ext = jnp.concatenate([vt_head, ones_rows], axis=0)
                o = jnp.dot(vt_ext, p_ref[slot, :n_keys, sub * Q_TOK:(sub + 1) * Q_TOK],
                            preferred_element_type=F32)
                ot_ref[rows, :] = o[:NA_HEAD_DIM] * (1.0 / o[NA_HEAD_DIM:NA_HEAD_DIM + 1])

        maxima = scores(0, 0)
        for pair in range(n_pairs):
            nxt = scores(pair + 1, (pair + 1) % 2) if pair + 1 < n_pairs else None
            softmax_pv(pair, pair % 2, maxima)
            maxima = nxt

    is_top = group == 0
    is_bot = group == groups - 1
    pl.when(is_top)(lambda: body(0))
    pl.when(jnp.logical_not(is_top | is_bot))(lambda: body(1))
    pl.when(is_bot)(lambda: body(2))

    y = ot_ref[...].T
    gated = _rms(y, g_ref[...]) * _silu(za_ref[...].astype(F32))
    out_ref[...] = gated.astype(BF16)


def _attention(qt, k, vt, za, t2, g_na, batch, seq):
    t = batch * seq
    groups = seq // Q_TOK
    assert groups >= KV_GROUPS

    def first_kv(u):
        return jnp.clip(u - 1, 0, groups - KV_GROUPS)

    def tok(u, b):
        return (b * groups + u, 0)

    def tok_t(u, b):
        return (0, b * groups + u)

    def kv(i):
        return lambda u, b: (b * groups + first_kv(u) + i, 0)

    def kv_t(i):
        return lambda u, b: (0, b * groups + first_kv(u) + i)

    const = lambda u, b: (0, 0)
    return pl.pallas_call(
        functools.partial(_attention_kernel, groups=groups),
        grid=(groups, batch),
        in_specs=[pl.BlockSpec((NA_WIDTH, Q_TOK), tok_t)]
                 + [pl.BlockSpec((Q_TOK, NA_WIDTH), kv(i)) for i in range(KV_GROUPS)]
                 + [pl.BlockSpec((NA_WIDTH, Q_TOK), kv_t(i)) for i in range(KV_GROUPS)]
                 + [pl.BlockSpec((Q_TOK, NA_WIDTH), tok),
                    pl.BlockSpec(t2.shape, lambda u, b: (0, 0, 0, 0)),
                    pl.BlockSpec((1, NA_WIDTH), const)],
        out_specs=pl.BlockSpec((Q_TOK, NA_WIDTH), tok),
        out_shape=jax.ShapeDtypeStruct((t, NA_WIDTH), BF16),
        scratch_shapes=[pltpu.VMEM((2, KV_TOK, 2 * Q_TOK), F32),
                        pltpu.VMEM((2, KV_TOK, 2 * Q_TOK), BF16),
                        pltpu.VMEM((NA_WIDTH, Q_TOK), F32)],
        compiler_params=pltpu.CompilerParams(
            dimension_semantics=("arbitrary", "arbitrary"),
            vmem_limit_bytes=VMEM_LIMIT),
        name="attention",
    )(qt, k, k, k, vt, vt, vt, za, t2, g_na)


DFT_CHUNK = 16
LANES = 128


def _to_lane_tiles(scr, x):
    for c in range(scr.shape[0]):
        scr[c] = x[:, c * LANES:(c + 1) * LANES]


def _from_lane_tiles(scr):
    return jnp.concatenate([scr[c] for c in range(scr.shape[0])], axis=1)


def _strided_rows(scr, start, size, stride):
    return jnp.concatenate([scr[c, pl.ds(start, size, stride=stride), :] for c in range(scr.shape[0])], axis=1)


def _store_strided_rows(scr, start, stride, x):
    for c in range(scr.shape[0]):
        scr[c, pl.ds(start, x.shape[0], stride=stride), :] = x[:, c * LANES:(c + 1) * LANES]


def _fourier_1_kernel(w_ref, u_ref, a_ref, *scratch, major, chunks):
    if not scratch:
        for c in range(chunks):
            rows = slice(c * DFT_CHUNK, (c + 1) * DFT_CHUNK)
            u2d = u_ref[0, :, rows, :].reshape(major * DFT_CHUNK, F_WIDTH)
            res = jnp.dot(w_ref[...], u2d, preferred_element_type=F32)
            a_ref[0, :, :, rows, :] = res.astype(BF16).reshape(major, 2, DFT_CHUNK, F_WIDTH)
        return
    in_f32, out_f32 = scratch
    _to_lane_tiles(in_f32, u_ref[0].reshape(major * DFT_CHUNK, F_WIDTH).astype(F32))
    for j in range(DFT_CHUNK):
        rhs = _strided_rows(in_f32, j, major, DFT_CHUNK).astype(BF16)
        res = jnp.dot(w_ref[...], rhs, preferred_element_type=F32)
        _store_strided_rows(out_f32, j, DFT_CHUNK, res)
    a_ref[0] = _from_lane_tiles(out_f32).astype(BF16).reshape(major, 2, DFT_CHUNK, F_WIDTH)


def _fourier_1(u4, w1, kron):
    batch, major, minor, _ = u4.shape
    chunks = minor // DFT_CHUNK if kron else 1
    tb = chunks * DFT_CHUNK
    scratch = [] if kron else [pltpu.VMEM((F_WIDTH // LANES, major * DFT_CHUNK, LANES), F32),
                               pltpu.VMEM((F_WIDTH // LANES, 2 * major * DFT_CHUNK, LANES), F32)]
    return pl.pallas_call(
        functools.partial(_fourier_1_kernel, major=major, chunks=chunks),
        grid=(batch, minor // tb),
        in_specs=[pl.BlockSpec(w1.shape, lambda b, j: (0, 0)),
                  pl.BlockSpec((1, major, tb, F_WIDTH), lambda b, j: (b, 0, j, 0))],
        out_specs=pl.BlockSpec((1, major, 2, tb, F_WIDTH), lambda b, j: (b, 0, 0, j, 0)),
        out_shape=jax.ShapeDtypeStruct((batch, major, 2, minor, F_WIDTH), BF16),
        scratch_shapes=scratch,
        compiler_params=pltpu.CompilerParams(
            dimension_semantics=("arbitrary", "arbitrary"),
            vmem_limit_bytes=VMEM_LIMIT),
        name="fourier_1",
    )(w1, u4)


def _fourier_2_kernel(a_ref, m_ref, cs_ref, wf_ref, bf_ref, y_ref, x_scr, y_scr):
    for j in range(DFT_CHUNK):
        rhs = a_ref[0, j].reshape(2 * DFT_MINOR, F_WIDTH)
        x = jnp.dot(m_ref[j], rhs, preferred_element_type=F32)
        rows = slice(j * DFT_MINOR, (j + 1) * DFT_MINOR)
        x_scr[rows, :F_WIDTH] = x[:DFT_MINOR].astype(BF16)
        x_scr[rows, F_WIDTH:] = x[DFT_MINOR:].astype(BF16)
    mixed = jnp.dot(x_scr[...], cs_ref[...], preferred_element_type=F32)
    y = jnp.dot(mixed.astype(BF16), wf_ref[...], preferred_element_type=F32) + bf_ref[...]
    for j in range(DFT_CHUNK):
        _store_strided_rows(y_scr, j, DFT_CHUNK, y[j * DFT_MINOR:(j + 1) * DFT_MINOR])
    y_ref[0] = _from_lane_tiles(y_scr).astype(BF16).reshape(DFT_MINOR, DFT_CHUNK, F_WIDTH)


def _fourier_2(a5, m_tab, cs, w_f, b_f):
    batch, major, _, minor, _ = a5.shape
    tr = DFT_CHUNK
    return pl.pallas_call(
        _fourier_2_kernel,
        grid=(batch, major // tr),
        in_specs=[pl.BlockSpec((1, tr, 2, minor, F_WIDTH), lambda b, r: (b, r, 0, 0, 0)),
                  pl.BlockSpec((tr, 2 * minor, 2 * minor), lambda b, r: (r, 0, 0)),
                  pl.BlockSpec(cs.shape, lambda b, r: (0, 0)),
                  pl.BlockSpec(w_f.shape, lambda b, r: (0, 0)),
                  pl.BlockSpec((1, F_WIDTH), lambda b, r: (0, 0))],
        out_specs=pl.BlockSpec((1, minor, tr, F_WIDTH), lambda b, r: (b, 0, r, 0)),
        out_shape=jax.ShapeDtypeStruct((batch, minor, major, F_WIDTH), BF16),
        scratch_shapes=[pltpu.VMEM((tr * minor, 2 * F_WIDTH), BF16),
                        pltpu.VMEM((F_WIDTH // LANES, minor * tr, LANES), F32)],
        compiler_params=pltpu.CompilerParams(
            dimension_semantics=("arbitrary", "arbitrary"),
            vmem_limit_bytes=VMEM_LIMIT),
        name="fourier_2",
    )(a5, m_tab, cs, w_f, b_f)


@functools.lru_cache(maxsize=None)
def _dft_constants(seq):
    major = seq // DFT_MINOR
    kron = major * DFT_CHUNK <= 256
    a = np.arange(major)
    ang1 = 2.0 * np.pi * np.outer(a, a) / major
    w1 = np.stack([np.cos(ang1), -np.sin(ang1)], axis=1).reshape(2 * major, major)
    if kron:
        w1 = np.kron(w1, np.eye(DFT_CHUNK))
    r = np.arange(major)[:, None, None]
    p = np.arange(DFT_MINOR)[None, :, None]
    b = np.arange(DFT_MINOR)[None, None, :]
    ang2 = 2.0 * np.pi * ((b * (r + major * p)) % seq) / seq
    scale = 1.0 / np.sqrt(seq * F_GROUP_DIM)
    e_re, e_im = np.cos(ang2) * scale, -np.sin(ang2) * scale
    m_tab = np.concatenate([np.concatenate([e_re, -e_im], axis=2),
                            np.concatenate([e_im, e_re], axis=2)], axis=1)
    c = np.arange(F_GROUP_DIM)
    ang3 = 2.0 * np.pi * np.outer(c, c) / F_GROUP_DIM
    eye = np.eye(F_GROUPS)
    cs = np.concatenate([np.kron(eye, np.cos(ang3)), np.kron(eye, np.sin(ang3))], axis=0)
    return (kron, np.asarray(w1, np.float32), np.asarray(m_tab, np.float32), np.asarray(cs, np.float32))


def _fourier(u, w_f, b_f, batch, seq):
    major = seq // DFT_MINOR
    kron, w1, m_tab, cs = _dft_constants(seq)
    a = _fourier_1(u.reshape(batch, major, DFT_MINOR, F_WIDTH), jnp.asarray(w1).astype(BF16), kron)
    y = _fourier_2(a, jnp.asarray(m_tab).astype(BF16), jnp.asarray(cs).astype(BF16), w_f, b_f)
    return y.reshape(batch * seq, F_WIDTH)


def _out_proj_kernel(ma_ref, yf_ref, zf_ref, x_ref, wa_ref, wf_ref, gf_ref, gp_ref, o_ref):
    mixed_f = _rms(yf_ref[...].astype(F32), gf_ref[...]) * _silu(zf_ref[...].astype(F32))
    out = jnp.dot(ma_ref[...], wa_ref[...], preferred_element_type=F32)
    out = out + jnp.dot(mixed_f.astype(BF16), wf_ref[...], preferred_element_type=F32)
    o_ref[...] = x_ref[...] + _rms(out, gp_ref[...])


def _out_proj(mixed_a, y_f, z_f, x2d, w_a, w_fo, g_f, g_post, tm=512):
    t = x2d.shape[0]
    tok = lambda i: (i, 0)
    const = lambda i: (0, 0)
    return pl.pallas_call(
        _out_proj_kernel,
        grid=(t // tm,),
        in_specs=[pl.BlockSpec((tm, NA_WIDTH), tok),
                  pl.BlockSpec((tm, F_WIDTH), tok),
                  pl.BlockSpec((tm, F_WIDTH), tok),
                  pl.BlockSpec((tm, D_MODEL), tok),
                  pl.BlockSpec(w_a.shape, const),
                  pl.BlockSpec(w_fo.shape, const),
                  pl.BlockSpec((1, F_WIDTH), const),
                  pl.BlockSpec((1, D_MODEL), const)],
        out_specs=pl.BlockSpec((tm, D_MODEL), tok),
        out_shape=jax.ShapeDtypeStruct((t, D_MODEL), F32),
        compiler_params=pltpu.CompilerParams(
            dimension_semantics=("arbitrary",), vmem_limit_bytes=VMEM_LIMIT),
        name="out_proj",
    )(mixed_a, y_f, z_f, x2d, w_a, w_fo, g_f, g_post)


def _encoder_layer(x, w_nat, w_tr, bias, w_f, b_f, g_pre, g_na, g_f, w_oa, w_of, g_post):
    batch, seq, _ = x.shape
    x2d = x.reshape(batch * seq, D_MODEL)
    k, z_a, u_f, z_f, q_t, v_t = _in_proj(x2d, g_pre, w_nat, w_tr)
    mixed_a = _attention(q_t, k, v_t, z_a, bias, g_na, batch, seq)
    y_f = _fourier(u_f, w_f, b_f, batch, seq)
    out = _out_proj(mixed_a, y_f, z_f, x2d, w_oa, w_of, g_f, g_post)
    return out.reshape(batch, seq, D_MODEL)


def kernel(x_prompt, x_sample, w_in, rpb, w_fourier, b_fourier, g_pre, g_na, g_f, w_out, g_post):
    depth = w_in.shape[0]
    y_prompt, y_sample = x_prompt, x_sample
    scale = NA_HEAD_DIM ** -0.5
    for l in range(depth):
        w = w_in[l]
        w_q, w_k, w_v, w_za, w_uf, w_zf = (w[:, i * NA_WIDTH:(i + 1) * NA_WIDTH] for i in range(6))
        w_nat = jnp.concatenate([w_k, w_za, w_uf, w_zf], axis=1).astype(BF16)
        w_tr = jnp.concatenate([w_q * scale, w_v], axis=1).T.astype(BF16)
        bias = _bias_table(rpb[l])
        row = lambda v: v.reshape(1, -1).astype(F32)
        args = (w_nat, w_tr, bias, w_fourier[l].astype(BF16), row(b_fourier[l]),
                row(g_pre[l]), row(g_na[l]), row(g_f[l]),
                w_out[l][:NA_WIDTH].astype(BF16), w_out[l][NA_WIDTH:].astype(BF16), row(g_post[l]))
        y_prompt = _encoder_layer(y_prompt, *args)
        y_sample = _encoder_layer(y_sample, *args)
    return (y_prompt, y_sample)
```

```python
import functools

import numpy as np
import jax
import jax.numpy as jnp
from jax import lax
from jax.experimental import pallas as pl
from jax.experimental.pallas import tpu as pltpu

D_MODEL = 1024
GRID_W = 64
WIN_ROWS = 8
WIN_COLS = 16
NA_HEADS = 8
NA_HEAD_DIM = 64
NA_WIDTH = NA_HEADS * NA_HEAD_DIM
F_GROUPS = 8
F_GROUP_DIM = 64
F_WIDTH = F_GROUPS * F_GROUP_DIM
RMS_EPS = 1e-6
NEG_INF = -1e30

Q_ROWS = 4
Q_TOK = Q_ROWS * GRID_W
KV_GROUPS = 3
KV_TOK = KV_GROUPS * Q_TOK
KEY_ROWS = KV_GROUPS * Q_ROWS
ROW_SLOTS = 2 * WIN_ROWS
DFT_MINOR = 128
TOKEN_BLOCK = 512
VMEM_LIMIT = 56 * 1024 * 1024

BF16 = jnp.bfloat16
F32 = jnp.float32


def _rms(x, g):
    inv = lax.rsqrt(jnp.mean(x * x, axis=-1, keepdims=True) + RMS_EPS)
    return x * inv * g


def _silu(z):
    return z * (1.0 / (1.0 + jnp.exp(-z)))


def _in_proj_kernel(x_ref, g_ref, wn_ref, wt_ref,
                    k_ref, uf_ref, zf_ref, qt_ref, vt_ref, zat_ref):
    h = _rms(x_ref[...], g_ref[...]).astype(BF16)
    nat = jnp.dot(h, wn_ref[...], preferred_element_type=F32)
    k_ref[...] = nat[:, 0 * NA_WIDTH:1 * NA_WIDTH].astype(BF16)
    uf_ref[...] = nat[:, 1 * NA_WIDTH:2 * NA_WIDTH].astype(BF16)
    zf_ref[...] = nat[:, 2 * NA_WIDTH:3 * NA_WIDTH].astype(BF16)
    tr = lax.dot_general(wt_ref[...], h, (((1,), (1,)), ((), ())),
                         preferred_element_type=F32)
    for i in range(qt_ref.shape[0]):
        tok = slice(i * Q_TOK, (i + 1) * Q_TOK)
        qt_ref[i] = tr[0 * NA_WIDTH:1 * NA_WIDTH, tok].astype(BF16)
        vt_ref[i] = tr[1 * NA_WIDTH:2 * NA_WIDTH, tok].astype(BF16)
    zat_ref[0] = tr[2 * NA_WIDTH:3 * NA_WIDTH].astype(BF16)


def _in_proj(x2d, g_pre, w_nat, w_tr, tm=TOKEN_BLOCK):
    t = x2d.shape[0]
    tok = lambda i: (i, 0)
    blk = lambda i: (i, 0, 0)
    const = lambda i: (0, 0)
    nat_shape = jax.ShapeDtypeStruct((t, NA_WIDTH), BF16)
    qv_shape = jax.ShapeDtypeStruct((t // Q_TOK, NA_WIDTH, Q_TOK), BF16)
    za_shape = jax.ShapeDtypeStruct((t // tm, NA_WIDTH, tm), BF16)
    return pl.pallas_call(
        _in_proj_kernel,
        grid=(t // tm,),
        in_specs=[pl.BlockSpec((tm, D_MODEL), tok),
                  pl.BlockSpec((1, D_MODEL), const),
                  pl.BlockSpec(w_nat.shape, const),
                  pl.BlockSpec(w_tr.shape, const)],
        out_specs=[pl.BlockSpec((tm, NA_WIDTH), tok)] * 3
                  + [pl.BlockSpec((tm // Q_TOK, NA_WIDTH, Q_TOK), blk)] * 2
                  + [pl.BlockSpec((1, NA_WIDTH, tm), blk)],
        out_shape=[nat_shape] * 3 + [qv_shape] * 2 + [za_shape],
        compiler_params=pltpu.CompilerParams(
            dimension_semantics=("arbitrary",), vmem_limit_bytes=VMEM_LIMIT),
        name="in_proj",
    )(x2d, g_pre, w_nat, w_tr)


def _bias_table_kernel(rpb_ref, t2_ref):
    head = pl.program_id(0)
    shape = (GRID_W, 2 * GRID_W)
    kc = lax.broadcasted_iota(jnp.int32, shape, 0)
    lane = lax.broadcasted_iota(jnp.int32, shape, 1)
    second = lane >= GRID_W
    qc = jnp.where(second, lane - GRID_W, lane)
    col_off = kc - qc + (WIN_COLS - 1)
    win_start = jnp.clip(qc - WIN_COLS // 2, 0, GRID_W - WIN_COLS)
    col_valid = (kc >= win_start) & (kc < win_start + WIN_COLS)
    n_off = 2 * WIN_COLS - 1
    n_row = 2 * WIN_ROWS - 1

    def entry(d, o):
        if 0 <= d < n_row:
            return rpb_ref[(head * n_row + d) * n_off + o]
        return jnp.float32(NEG_INF)

    for d in range(ROW_SLOTS):
        tile = jnp.full(shape, NEG_INF, F32)
        for o in range(n_off):
            val = jnp.where(second, entry(d - 1, o), entry(d, o))
            tile = jnp.where(col_off == o, val, tile)
        t2_ref[0, d] = jnp.where(col_valid, tile, NEG_INF)


def _bias_table(rpb):
    return pl.pallas_call(
        _bias_table_kernel,
        grid=(NA_HEADS,),
        in_specs=[pl.BlockSpec(memory_space=pltpu.SMEM)],
        out_specs=pl.BlockSpec((1, ROW_SLOTS, GRID_W, 2 * GRID_W), lambda h: (h, 0, 0, 0)),
        out_shape=jax.ShapeDtypeStruct((NA_HEADS, ROW_SLOTS, GRID_W, 2 * GRID_W), F32),
        compiler_params=pltpu.CompilerParams(dimension_semantics=("arbitrary",)),
        name="bias_table",
    )(rpb.astype(F32).reshape(-1))


def _token_range(refs, lo, hi, other, axis):
    pieces = []
    for idx, ref in enumerate(refs):
        a, b = max(lo, idx * Q_TOK), min(hi, (idx + 1) * Q_TOK)
        if a < b:
            tok = slice(a - idx * Q_TOK, b - idx * Q_TOK)
            pieces.append(ref[tok, other] if axis == 0 else ref[0, other, tok])
    return pieces[0] if len(pieces) == 1 else jnp.concatenate(pieces, axis=axis)


def _window_plan(variant, jp):
    if variant == 0:
        return [(i, i - 2 * jp + WIN_ROWS - 1, None) for i in range(WIN_ROWS)]
    if variant == 2:
        q_row = KEY_ROWS - Q_ROWS + 2 * jp
        return [(i, i - q_row + WIN_ROWS - 1, None) for i in range(KEY_ROWS - WIN_ROWS, KEY_ROWS)]
    q_row = Q_ROWS + 2 * jp
    plan = []
    for i in range(q_row - WIN_ROWS // 2, q_row + WIN_ROWS // 2 + 1):
        half = "low" if i == q_row - WIN_ROWS // 2 else "high" if i == q_row + WIN_ROWS // 2 else None
        plan.append((i, i - q_row + WIN_ROWS - 1, half))
    return plan


def _attention_kernel(qt_ref, k0_ref, k1_ref, k2_ref, v0_ref, v1_ref, v2_ref,
                      t2_ref, out_ref, sb_ref, p_ref, *, groups):
    k_refs = (k0_ref, k1_ref, k2_ref)
    v_refs = (v0_ref, v1_ref, v2_ref)
    group = pl.program_id(0)
    pair_lanes = 2 * NA_HEAD_DIM
    q_lanes = 2 * GRID_W
    n_pairs = NA_HEADS // 2

    def body(variant):
        plans = [_window_plan(variant, jp) for jp in range(Q_ROWS // 2)]
        key_lo = min(p[0][0] for p in plans)
        key_hi = max(p[-1][0] for p in plans) + 1
        if (key_hi - key_lo) % 2:
            key_hi += 1
        lane = lax.broadcasted_iota(jnp.int32, (1, q_lanes), 1)
        half_mask = {"low": jnp.where(lane < GRID_W, 0.0, NEG_INF).astype(F32),
                     "high": jnp.where(lane >= GRID_W, 0.0, NEG_INF).astype(F32)}
        ones_rows = jnp.ones((16, (key_hi - key_lo) * GRID_W), BF16)
        zero_blk = jnp.zeros((GRID_W, q_lanes), BF16)

        def scores(pair, slot):
            cols = slice(pair * pair_lanes, (pair + 1) * pair_lanes)
            k_pair = _token_range(k_refs, key_lo * GRID_W, key_hi * GRID_W, cols, 0)
            qt_pair = qt_ref[0, cols, :]
            zero = jnp.zeros((NA_HEAD_DIM, Q_TOK), BF16)
            qt_both = jnp.concatenate(
                [jnp.concatenate([qt_pair[:NA_HEAD_DIM], zero], axis=0),
                 jnp.concatenate([zero, qt_pair[NA_HEAD_DIM:]], axis=0)], axis=1)
            s = jnp.dot(k_pair, qt_both, preferred_element_type=F32)
            maxima = []
            for sub in range(2):
                for jp, plan in enumerate(plans):
                    lanes = slice(sub * Q_TOK + jp * q_lanes, sub * Q_TOK + (jp + 1) * q_lanes)
                    m_acc = None
                    for i, slot_d, half in plan:
                        rows = slice((i - key_lo) * GRID_W, (i - key_lo + 1) * GRID_W)
                        blk = s[rows, lanes] + t2_ref[2 * pair + sub, slot_d]
                        if half is not None:
                            blk = blk + half_mask[half]
                        sb_ref[slot, rows, lanes] = blk
                        m_acc = blk if m_acc is None else jnp.maximum(m_acc, blk)
                    maxima.append(jnp.max(m_acc, axis=0, keepdims=True))
            return maxima

        def softmax_pv(pair, slot, maxima):
            n_keys = (key_hi - key_lo) * GRID_W
            for sub in range(2):
                head = 2 * pair + sub
                for jp, plan in enumerate(plans):
                    lanes = slice(sub * Q_TOK + jp * q_lanes, sub * Q_TOK + (jp + 1) * q_lanes)
                    m = maxima[2 * sub + jp]
                    live = {i for i, _, _ in plan}
                    for i in range(key_lo, key_hi):
                        rows = slice((i - key_lo) * GRID_W, (i - key_lo + 1) * GRID_W)
                        if i in live:
                            p_ref[slot, rows, lanes] = jnp.exp((sb_ref[slot, rows, lanes] - m).astype(BF16))
                        else:
                            p_ref[slot, rows, lanes] = zero_blk
                rows = slice(head * NA_HEAD_DIM, (head + 1) * NA_HEAD_DIM)
                vt_head = _token_range(v_refs, key_lo * GRID_W, key_hi * GRID_W, rows, 1)
                vt_ext = jnp.concatenate([vt_head, ones_rows], axis=0)
                o = jnp.dot(vt_ext, p_ref[slot, :n_keys, sub * Q_TOK:(sub + 1) * Q_TOK],
                            preferred_element_type=F32)
                out_ref[0, rows, :] = (o[:NA_HEAD_DIM] * (1.0 / o[NA_HEAD_DIM:NA_HEAD_DIM + 1])).astype(BF16)

        maxima = scores(0, 0)
        for pair in range(n_pairs):
            nxt = scores(pair + 1, (pair + 1) % 2) if pair + 1 < n_pairs else None
            softmax_pv(pair, pair % 2, maxima)
            maxima = nxt

    is_top = group == 0
    is_bot = group == groups - 1
    pl.when(is_top)(lambda: body(0))
    pl.when(jnp.logical_not(is_top | is_bot))(lambda: body(1))
    pl.when(is_bot)(lambda: body(2))


def _attention(qt, k, vt, t2, batch, seq):
    t = batch * seq
    groups = seq // Q_TOK
    assert groups >= KV_GROUPS

    def first_kv(u):
        return jnp.clip(u - 1, 0, groups - KV_GROUPS)

    def blk(u, b):
        return (b * groups + u, 0, 0)

    def kv(i):
        return lambda u, b: (b * groups + first_kv(u) + i, 0)

    def kv_blk(i):
        return lambda u, b: (b * groups + first_kv(u) + i, 0, 0)

    return pl.pallas_call(
        functools.partial(_attention_kernel, groups=groups),
        grid=(groups, batch),
        in_specs=[pl.BlockSpec((1, NA_WIDTH, Q_TOK), blk)]
                 + [pl.BlockSpec((Q_TOK, NA_WIDTH), kv(i)) for i in range(KV_GROUPS)]
                 + [pl.BlockSpec((1, NA_WIDTH, Q_TOK), kv_blk(i)) for i in range(KV_GROUPS)]
                 + [pl.BlockSpec(t2.shape, lambda u, b: (0, 0, 0, 0))],
        out_specs=pl.BlockSpec((1, NA_WIDTH, Q_TOK), blk),
        out_shape=jax.ShapeDtypeStruct((t // Q_TOK, NA_WIDTH, Q_TOK), BF16),
        scratch_shapes=[pltpu.VMEM((2, KV_TOK, 2 * Q_TOK), F32),
                        pltpu.VMEM((2, KV_TOK, 2 * Q_TOK), BF16)],
        compiler_params=pltpu.CompilerParams(
            dimension_semantics=("arbitrary", "arbitrary"),
            vmem_limit_bytes=VMEM_LIMIT),
        name="attention",
    )(qt, k, k, k, vt, vt, vt, t2)


DFT_CHUNK = 16
LANES = 128


def _to_lane_tiles(scr, x):
    for c in range(scr.shape[0]):
        scr[c] = x[:, c * LANES:(c + 1) * LANES]


def _from_lane_tiles(scr):
    return jnp.concatenate([scr[c] for c in range(scr.shape[0])], axis=1)


def _strided_rows(scr, start, size, stride):
    return jnp.concatenate([scr[c, pl.ds(start, size, stride=stride), :] for c in range(scr.shape[0])], axis=1)


def _store_strided_rows(scr, start, stride, x):
    for c in range(scr.shape[0]):
        scr[c, pl.ds(start, x.shape[0], stride=stride), :] = x[:, c * LANES:(c + 1) * LANES]


def _fourier_1_kernel(w_ref, u_ref, a_ref, *scratch, major, chunks):
    if not scratch:
        for c in range(chunks):
            rows = slice(c * DFT_CHUNK, (c + 1) * DFT_CHUNK)
            u2d = u_ref[0, :, rows, :].reshape(major * DFT_CHUNK, F_WIDTH)
            res = jnp.dot(w_ref[...], u2d, preferred_element_type=F32)
            a_ref[0, :, :, rows, :] = res.astype(BF16).reshape(major, 2, DFT_CHUNK, F_WIDTH)
        return
    in_f32, out_f32 = scratch
    _to_lane_tiles(in_f32, u_ref[0].reshape(major * DFT_CHUNK, F_WIDTH).astype(F32))
    for j in range(DFT_CHUNK):
        rhs = _strided_rows(in_f32, j, major, DFT_CHUNK).astype(BF16)
        res = jnp.dot(w_ref[...], rhs, preferred_element_type=F32)
        _store_strided_rows(out_f32, j, DFT_CHUNK, res)
    a_ref[0] = _from_lane_tiles(out_f32).astype(BF16).reshape(major, 2, DFT_CHUNK, F_WIDTH)


def _fourier_1(u4, w1, kron):
    batch, major, minor, _ = u4.shape
    chunks = minor // DFT_CHUNK if kron else 1
    tb = chunks * DFT_CHUNK
    scratch = [] if kron else [pltpu.VMEM((F_WIDTH // LANES, major * DFT_CHUNK, LANES), F32),
                               pltpu.VMEM((F_WIDTH // LANES, 2 * major * DFT_CHUNK, LANES), F32)]
    return pl.pallas_call(
        functools.partial(_fourier_1_kernel, major=major, chunks=chunks),
        grid=(batch, minor // tb),
        in_specs=[pl.BlockSpec(w1.shape, lambda b, j: (0, 0)),
                  pl.BlockSpec((1, major, tb, F_WIDTH), lambda b, j: (b, 0, j, 0))],
        out_specs=pl.BlockSpec((1, major, 2, tb, F_WIDTH), lambda b, j: (b, 0, 0, j, 0)),
        out_shape=jax.ShapeDtypeStruct((batch, major, 2, minor, F_WIDTH), BF16),
        scratch_shapes=scratch,
        compiler_params=pltpu.CompilerParams(
            dimension_semantics=("arbitrary", "arbitrary"),
            vmem_limit_bytes=VMEM_LIMIT),
        name="fourier_1",
    )(w1, u4)


def _fourier_2_kernel(a_ref, m_ref, cs_ref, wf_ref, bf_ref, y_ref, x_scr, y_scr):
    for j in range(DFT_CHUNK):
        rhs = a_ref[0, j].reshape(2 * DFT_MINOR, F_WIDTH)
        x = jnp.dot(m_ref[j], rhs, preferred_element_type=F32)
        rows = slice(j * DFT_MINOR, (j + 1) * DFT_MINOR)
        x_scr[rows, :F_WIDTH] = x[:DFT_MINOR].astype(BF16)
        x_scr[rows, F_WIDTH:] = x[DFT_MINOR:].astype(BF16)
    mixed = jnp.dot(x_scr[...], cs_ref[...], preferred_element_type=F32)
    y = jnp.dot(mixed.astype(BF16), wf_ref[...], preferred_element_type=F32) + bf_ref[...]
    for j in range(DFT_CHUNK):
        _store_strided_rows(y_scr, j, DFT_CHUNK, y[j * DFT_MINOR:(j + 1) * DFT_MINOR])
    y_ref[0] = _from_lane_tiles(y_scr).astype(BF16).reshape(DFT_MINOR, DFT_CHUNK, F_WIDTH)


def _fourier_2(a5, m_tab, cs, w_f, b_f):
    batch, major, _, minor, _ = a5.shape
    tr = DFT_CHUNK
    return pl.pallas_call(
        _fourier_2_kernel,
        grid=(batch, major // tr),
        in_specs=[pl.BlockSpec((1, tr, 2, minor, F_WIDTH), lambda b, r: (b, r, 0, 0, 0)),
                  pl.BlockSpec((tr, 2 * minor, 2 * minor), lambda b, r: (r, 0, 0)),
                  pl.BlockSpec(cs.shape, lambda b, r: (0, 0)),
                  pl.BlockSpec(w_f.shape, lambda b, r: (0, 0)),
                  pl.BlockSpec((1, F_WIDTH), lambda b, r: (0, 0))],
        out_specs=pl.BlockSpec((1, minor, tr, F_WIDTH), lambda b, r: (b, 0, r, 0)),
        out_shape=jax.ShapeDtypeStruct((batch, minor, major, F_WIDTH), BF16),
        scratch_shapes=[pltpu.VMEM((tr * minor, 2 * F_WIDTH), BF16),
                        pltpu.VMEM((F_WIDTH // LANES, minor * tr, LANES), F32)],
        compiler_params=pltpu.CompilerParams(
            dimension_semantics=("arbitrary", "arbitrary"),
            vmem_limit_bytes=VMEM_LIMIT),
        name="fourier_2",
    )(a5, m_tab, cs, w_f, b_f)


@functools.lru_cache(maxsize=None)
def _dft_constants(seq):
    major = seq // DFT_MINOR
    kron = major * DFT_CHUNK <= 256
    a = np.arange(major)
    ang1 = 2.0 * np.pi * np.outer(a, a) / major
    w1 = np.stack([np.cos(ang1), -np.sin(ang1)], axis=1).reshape(2 * major, major)
    if kron:
        w1 = np.kron(w1, np.eye(DFT_CHUNK))
    r = np.arange(major)[:, None, None]
    p = np.arange(DFT_MINOR)[None, :, None]
    b = np.arange(DFT_MINOR)[None, None, :]
    ang2 = 2.0 * np.pi * ((b * (r + major * p)) % seq) / seq
    scale = 1.0 / np.sqrt(seq * F_GROUP_DIM)
    e_re, e_im = np.cos(ang2) * scale, -np.sin(ang2) * scale
    m_tab = np.concatenate([np.concatenate([e_re, -e_im], axis=2),
                            np.concatenate([e_im, e_re], axis=2)], axis=1)
    c = np.arange(F_GROUP_DIM)
    ang3 = 2.0 * np.pi * np.outer(c, c) / F_GROUP_DIM
    eye = np.eye(F_GROUPS)
    cs = np.concatenate([np.kron(eye, np.cos(ang3)), np.kron(eye, np.sin(ang3))], axis=0)
    return (kron, np.asarray(w1, np.float32), np.asarray(m_tab, np.float32), np.asarray(cs, np.float32))


def _fourier(u, w_f, b_f, batch, seq):
    major = seq // DFT_MINOR
    kron, w1, m_tab, cs = _dft_constants(seq)
    a = _fourier_1(u.reshape(batch, major, DFT_MINOR, F_WIDTH), jnp.asarray(w1).astype(BF16), kron)
    y = _fourier_2(a, jnp.asarray(m_tab).astype(BF16), jnp.asarray(cs).astype(BF16), w_f, b_f)
    return y.reshape(batch * seq, F_WIDTH)


def _out_proj_kernel(ot_ref, zat_ref, yf_ref, zf_ref, x_ref, wa_ref, wf_ref,
                     gna_ref, gf_ref, gp_ref, o_ref):
    y_t = jnp.concatenate([ot_ref[i] for i in range(ot_ref.shape[0])], axis=1).astype(F32)
    inv = lax.rsqrt(jnp.mean(y_t * y_t, axis=0, keepdims=True) + RMS_EPS)
    mixed_at = (y_t * inv * gna_ref[...]) * _silu(zat_ref[0].astype(F32))
    mixed_f = _rms(yf_ref[...].astype(F32), gf_ref[...]) * _silu(zf_ref[...].astype(F32))
    out = lax.dot_general(mixed_at.astype(BF16), wa_ref[...], (((0,), (0,)), ((), ())),
                          preferred_element_type=F32)
    out = out + jnp.dot(mixed_f.astype(BF16), wf_ref[...], preferred_element_type=F32)
    o_ref[...] = x_ref[...] + _rms(out, gp_ref[...])


def _out_proj(o_t, za_t, y_f, z_f, x2d, w_a, w_fo, g_na_col, g_f, g_post, tm=TOKEN_BLOCK):
    t = x2d.shape[0]
    tok = lambda i: (i, 0)
    blk = lambda i: (i, 0, 0)
    const = lambda i: (0, 0)
    return pl.pallas_call(
        _out_proj_kernel,
        grid=(t // tm,),
        in_specs=[pl.BlockSpec((tm // Q_TOK, NA_WIDTH, Q_TOK), blk),
                  pl.BlockSpec((1, NA_WIDTH, tm), blk),
                  pl.BlockSpec((tm, F_WIDTH), tok),
                  pl.BlockSpec((tm, F_WIDTH), tok),
                  pl.BlockSpec((tm, D_MODEL), tok),
                  pl.BlockSpec(w_a.shape, const),
                  pl.BlockSpec(w_fo.shape, const),
                  pl.BlockSpec((NA_WIDTH, 1), const),
                  pl.BlockSpec((1, F_WIDTH), const),
                  pl.BlockSpec((1, D_MODEL), const)],
        out_specs=pl.BlockSpec((tm, D_MODEL), tok),
        out_shape=jax.ShapeDtypeStruct((t, D_MODEL), F32),
        compiler_params=pltpu.CompilerParams(
            dimension_semantics=("arbitrary",), vmem_limit_bytes=VMEM_LIMIT),
        name="out_proj",
    )(o_t, za_t, y_f, z_f, x2d, w_a, w_fo, g_na_col, g_f, g_post)


def _encoder_layer(x, w_nat, w_tr, bias, w_f, b_f, g_pre, g_na_col, g_f, w_oa, w_of, g_post):
    batch, seq, _ = x.shape
    x2d = x.reshape(batch * seq, D_MODEL)
    k, u_f, z_f, q_t, v_t, za_t = _in_proj(x2d, g_pre, w_nat, w_tr)
    o_t = _attention(q_t, k, v_t, bias, batch, seq)
    y_f = _fourier(u_f, w_f, b_f, batch, seq)
    out = _out_proj(o_t, za_t, y_f, z_f, x2d, w_oa, w_of, g_na_col, g_f, g_post)
    return out.reshape(batch, seq, D_MODEL)


def kernel(x_prompt, x_sample, w_in, rpb, w_fourier, b_fourier, g_pre, g_na, g_f, w_out, g_post):
    depth = w_in.shape[0]
    y_prompt, y_sample = x_prompt, x_sample
    scale = NA_HEAD_DIM ** -0.5
    for l in range(depth):
        w = w_in[l]
        w_q, w_k, w_v, w_za, w_uf, w_zf = (w[:, i * NA_WIDTH:(i + 1) * NA_WIDTH] for i in range(6))
        w_nat = jnp.concatenate([w_k, w_uf, w_zf], axis=1).astype(BF16)
        w_tr = jnp.concatenate([w_q * scale, w_v, w_za], axis=1).T.astype(BF16)
        bias = _bias_table(rpb[l])
        row = lambda v: v.reshape(1, -1).astype(F32)
        args = (w_nat, w_tr, bias, w_fourier[l].astype(BF16), row(b_fourier[l]),
                row(g_pre[l]), g_na[l].reshape(-1, 1).astype(F32), row(g_f[l]),
                w_out[l][:NA_WIDTH].astype(BF16), w_out[l][NA_WIDTH:].astype(BF16), row(g_post[l]))
        y_prompt = _encoder_layer(y_prompt, *args)
        y_sample = _encoder_layer(y_sample, *args)
    return (y_prompt, y_sample)
```

```python
import functools

import numpy as np
import jax
import jax.numpy as jnp
from jax import lax
from jax.experimental import pallas as pl
from jax.experimental.pallas import tpu as pltpu

D_MODEL = 1024
GRID_W = 64
WIN_ROWS = 8
WIN_COLS = 16
NA_HEADS = 8
NA_HEAD_DIM = 64
NA_WIDTH = NA_HEADS * NA_HEAD_DIM
F_GROUPS = 8
F_GROUP_DIM = 64
F_WIDTH = F_GROUPS * F_GROUP_DIM
RMS_EPS = 1e-6
NEG_INF = -1e30

Q_ROWS = 4
Q_TOK = Q_ROWS * GRID_W
KV_GROUPS = 3
KV_TOK = KV_GROUPS * Q_TOK
KEY_ROWS = KV_GROUPS * Q_ROWS
ROW_SLOTS = 2 * WIN_ROWS
DFT_MINOR = 128
TOKEN_BLOCK = 512
VMEM_LIMIT = 56 * 1024 * 1024

BF16 = jnp.bfloat16
F32 = jnp.float32


def _rms(x, g):
    inv = lax.rsqrt(jnp.mean(x * x, axis=-1, keepdims=True) + RMS_EPS)
    return x * inv * g


def _silu(z):
    return z * (1.0 / (1.0 + jnp.exp(-z)))


def _gate(y_normed, z):
    return y_normed.astype(BF16) * _silu(z)


def _in_proj_kernel(x_ref, g_ref, wn_ref, wt_ref,
                    k_ref, uf_ref, zf_ref, qt_ref, vt_ref, zat_ref):
    h = _rms(x_ref[...], g_ref[...]).astype(BF16)
    nat = jnp.dot(h, wn_ref[...], preferred_element_type=F32)
    k_ref[...] = nat[:, 0 * NA_WIDTH:1 * NA_WIDTH].astype(BF16)
    uf_ref[...] = nat[:, 1 * NA_WIDTH:2 * NA_WIDTH].astype(BF16)
    zf_ref[...] = nat[:, 2 * NA_WIDTH:3 * NA_WIDTH].astype(BF16)
    tr = lax.dot_general(wt_ref[...], h, (((1,), (1,)), ((), ())),
                         preferred_element_type=F32)
    for i in range(qt_ref.shape[0]):
        tok = slice(i * Q_TOK, (i + 1) * Q_TOK)
        qt_ref[i] = tr[0 * NA_WIDTH:1 * NA_WIDTH, tok].astype(BF16)
        vt_ref[i] = tr[1 * NA_WIDTH:2 * NA_WIDTH, tok].astype(BF16)
    zat_ref[0] = tr[2 * NA_WIDTH:3 * NA_WIDTH].astype(BF16)


def _in_proj(x2d, g_pre, w_nat, w_tr, tm=TOKEN_BLOCK):
    t = x2d.shape[0]
    tok = lambda i: (i, 0)
    blk = lambda i: (i, 0, 0)
    const = lambda i: (0, 0)
    nat_shape = jax.ShapeDtypeStruct((t, NA_WIDTH), BF16)
    qv_shape = jax.ShapeDtypeStruct((t // Q_TOK, NA_WIDTH, Q_TOK), BF16)
    za_shape = jax.ShapeDtypeStruct((t // tm, NA_WIDTH, tm), BF16)
    return pl.pallas_call(
        _in_proj_kernel,
        grid=(t // tm,),
        in_specs=[pl.BlockSpec((tm, D_MODEL), tok),
                  pl.BlockSpec((1, D_MODEL), const),
                  pl.BlockSpec(w_nat.shape, const),
                  pl.BlockSpec(w_tr.shape, const)],
        out_specs=[pl.BlockSpec((tm, NA_WIDTH), tok)] * 3
                  + [pl.BlockSpec((tm // Q_TOK, NA_WIDTH, Q_TOK), blk)] * 2
                  + [pl.BlockSpec((1, NA_WIDTH, tm), blk)],
        out_shape=[nat_shape] * 3 + [qv_shape] * 2 + [za_shape],
        compiler_params=pltpu.CompilerParams(
            dimension_semantics=("arbitrary",), vmem_limit_bytes=VMEM_LIMIT),
        name="in_proj",
    )(x2d, g_pre, w_nat, w_tr)


def _bias_table_kernel(rpb_ref, t2_ref):
    head = pl.program_id(0)
    shape = (GRID_W, 2 * GRID_W)
    kc = lax.broadcasted_iota(jnp.int32, shape, 0)
    lane = lax.broadcasted_iota(jnp.int32, shape, 1)
    second = lane >= GRID_W
    qc = jnp.where(second, lane - GRID_W, lane)
    col_off = kc - qc + (WIN_COLS - 1)
    win_start = jnp.clip(qc - WIN_COLS // 2, 0, GRID_W - WIN_COLS)
    col_valid = (kc >= win_start) & (kc < win_start + WIN_COLS)
    n_off = 2 * WIN_COLS - 1
    n_row = 2 * WIN_ROWS - 1

    def entry(d, o):
        if 0 <= d < n_row:
            return rpb_ref[(head * n_row + d) * n_off + o]
        return jnp.float32(NEG_INF)

    for d in range(ROW_SLOTS):
        tile = jnp.full(shape, NEG_INF, F32)
        for o in range(n_off):
            val = jnp.where(second, entry(d - 1, o), entry(d, o))
            tile = jnp.where(col_off == o, val, tile)
        t2_ref[0, d] = jnp.where(col_valid, tile, NEG_INF)


def _bias_table(rpb):
    return pl.pallas_call(
        _bias_table_kernel,
        grid=(NA_HEADS,),
        in_specs=[pl.BlockSpec(memory_space=pltpu.SMEM)],
        out_specs=pl.BlockSpec((1, ROW_SLOTS, GRID_W, 2 * GRID_W), lambda h: (h, 0, 0, 0)),
        out_shape=jax.ShapeDtypeStruct((NA_HEADS, ROW_SLOTS, GRID_W, 2 * GRID_W), F32),
        compiler_params=pltpu.CompilerParams(dimension_semantics=("arbitrary",)),
        name="bias_table",
    )(rpb.astype(F32).reshape(-1))


def _token_range(refs, lo, hi, other, axis):
    pieces = []
    for idx, ref in enumerate(refs):
        a, b = max(lo, idx * Q_TOK), min(hi, (idx + 1) * Q_TOK)
        if a < b:
            tok = slice(a - idx * Q_TOK, b - idx * Q_TOK)
            pieces.append(ref[tok, other] if axis == 0 else ref[0, other, tok])
    return pieces[0] if len(pieces) == 1 else jnp.concatenate(pieces, axis=axis)


def _window_plan(variant, jp):
    if variant == 0:
        return [(i, i - 2 * jp + WIN_ROWS - 1, None) for i in range(WIN_ROWS)]
    if variant == 2:
        q_row = KEY_ROWS - Q_ROWS + 2 * jp
        return [(i, i - q_row + WIN_ROWS - 1, None) for i in range(KEY_ROWS - WIN_ROWS, KEY_ROWS)]
    q_row = Q_ROWS + 2 * jp
    plan = []
    for i in range(q_row - WIN_ROWS // 2, q_row + WIN_ROWS // 2 + 1):
        half = "low" if i == q_row - WIN_ROWS // 2 else "high" if i == q_row + WIN_ROWS // 2 else None
        plan.append((i, i - q_row + WIN_ROWS - 1, half))
    return plan


def _attention_kernel(qt_ref, k0_ref, k1_ref, k2_ref, v0_ref, v1_ref, v2_ref,
                      t2_ref, out_ref, sb_ref, p_ref, *, groups):
    k_refs = (k0_ref, k1_ref, k2_ref)
    v_refs = (v0_ref, v1_ref, v2_ref)
    group = pl.program_id(0)
    pair_lanes = 2 * NA_HEAD_DIM
    q_lanes = 2 * GRID_W
    n_pairs = NA_HEADS // 2

    def body(variant):
        plans = [_window_plan(variant, jp) for jp in range(Q_ROWS // 2)]
        key_lo = min(p[0][0] for p in plans)
        key_top = max(p[-1][0] for p in plans) + 1
        key_hi = key_top + (key_top - key_lo) % 2
        lane = lax.broadcasted_iota(jnp.int32, (1, q_lanes), 1)
        half_mask = {"low": jnp.where(lane < GRID_W, 0.0, NEG_INF).astype(F32),
                     "high": jnp.where(lane >= GRID_W, 0.0, NEG_INF).astype(F32)}
        ones_rows = jnp.ones((16, (key_hi - key_lo) * GRID_W), BF16)
        zero_blk = jnp.zeros((GRID_W, q_lanes), BF16)

        def scores(pair, slot):
            cols = slice(pair * pair_lanes, (pair + 1) * pair_lanes)
            k_pair = _token_range(k_refs, key_lo * GRID_W, key_top * GRID_W, cols, 0)
            qt_pair = qt_ref[0, cols, :]
            zero = jnp.zeros((NA_HEAD_DIM, Q_TOK), BF16)
            qt_both = jnp.concatenate(
                [jnp.concatenate([qt_pair[:NA_HEAD_DIM], zero], axis=0),
                 jnp.concatenate([zero, qt_pair[NA_HEAD_DIM:]], axis=0)], axis=1)
            s = jnp.dot(k_pair, qt_both, preferred_element_type=F32)
            maxima = []
            for sub in range(2):
                for jp, plan in enumerate(plans):
                    lanes = slice(sub * Q_TOK + jp * q_lanes, sub * Q_TOK + (jp + 1) * q_lanes)
                    m_acc = None
                    for i, slot_d, half in plan:
                        rows = slice((i - key_lo) * GRID_W, (i - key_lo + 1) * GRID_W)
                        blk = s[rows, lanes] + t2_ref[2 * pair + sub, slot_d]
                        if half is not None:
                            blk = blk + half_mask[half]
                        sb_ref[slot, rows, lanes] = blk
                        m_acc = blk if m_acc is None else jnp.maximum(m_acc, blk)
                    maxima.append(jnp.max(m_acc, axis=0, keepdims=True))
            return maxima

        def softmax_pv(pair, slot, maxima):
            n_keys = (key_hi - key_lo) * GRID_W
            for sub in range(2):
                head = 2 * pair + sub
                for jp, plan in enumerate(plans):
                    lanes = slice(sub * Q_TOK + jp * q_lanes, sub * Q_TOK + (jp + 1) * q_lanes)
                    m = maxima[2 * sub + jp]
                    live = {i for i, _, _ in plan}
                    for i in range(key_lo, key_hi):
                        rows = slice((i - key_lo) * GRID_W, (i - key_lo + 1) * GRID_W)
                        if i in live:
                            p_ref[slot, rows, lanes] = jnp.exp((sb_ref[slot, rows, lanes] - m).astype(BF16))
                        else:
                            p_ref[slot, rows, lanes] = zero_blk
                rows = slice(head * NA_HEAD_DIM, (head + 1) * NA_HEAD_DIM)
                vt_head = _token_range(v_refs, key_lo * GRID_W, key_hi * GRID_W, rows, 1)
                vt_ext = jnp.concatenate([vt_head, ones_rows], axis=0)
                o = jnp.dot(vt_ext, p_ref[slot, :n_keys, sub * Q_TOK:(sub + 1) * Q_TOK],
                            preferred_element_type=F32)
                out_ref[0, rows, :] = (o[:NA_HEAD_DIM] * (1.0 / o[NA_HEAD_DIM:NA_HEAD_DIM + 1])).astype(BF16)

        maxima = scores(0, 0)
        for pair in range(n_pairs):
            nxt = scores(pair + 1, (pair + 1) % 2) if pair + 1 < n_pairs else None
            softmax_pv(pair, pair % 2, maxima)
            maxima = nxt

    is_top = group == 0
    is_bot = group == groups - 1
    pl.when(is_top)(lambda: body(0))
    pl.when(jnp.logical_not(is_top | is_bot))(lambda: body(1))
    pl.when(is_bot)(lambda: body(2))


def _attention(qt, k, vt, t2, batch, seq):
    t = batch * seq
    groups = seq // Q_TOK
    assert groups >= KV_GROUPS

    def first_kv(u):
        return jnp.clip(u - 1, 0, groups - KV_GROUPS)

    def blk(u, b):
        return (b * groups + u, 0, 0)

    def kv(i):
        return lambda u, b: (b * groups + first_kv(u) + i, 0)

    def kv_blk(i):
        return lambda u, b: (b * groups + first_kv(u) + i, 0, 0)

    return pl.pallas_call(
        functools.partial(_attention_kernel, groups=groups),
        grid=(groups, batch),
        in_specs=[pl.BlockSpec((1, NA_WIDTH, Q_TOK), blk)]
                 + [pl.BlockSpec((Q_TOK, NA_WIDTH), kv(i)) for i in range(KV_GROUPS)]
                 + [pl.BlockSpec((1, NA_WIDTH, Q_TOK), kv_blk(i)) for i in range(KV_GROUPS)]
                 + [pl.BlockSpec(t2.shape, lambda u, b: (0, 0, 0, 0))],
        out_specs=pl.BlockSpec((1, NA_WIDTH, Q_TOK), blk),
        out_shape=jax.ShapeDtypeStruct((t // Q_TOK, NA_WIDTH, Q_TOK), BF16),
        scratch_shapes=[pltpu.VMEM((2, KV_TOK, 2 * Q_TOK), F32),
                        pltpu.VMEM((2, KV_TOK, 2 * Q_TOK), BF16)],
        compiler_params=pltpu.CompilerParams(
            dimension_semantics=("arbitrary", "arbitrary"),
            vmem_limit_bytes=VMEM_LIMIT),
        name="attention",
    )(qt, k, k, k, vt, vt, vt, t2)


DFT_CHUNK = 16
LANES = 128


def _to_lane_tiles(scr, x):
    for c in range(scr.shape[0]):
        scr[c] = x[:, c * LANES:(c + 1) * LANES]


def _from_lane_tiles(scr):
    return jnp.concatenate([scr[c] for c in range(scr.shape[0])], axis=1)


def _strided_rows(scr, start, size, stride):
    return jnp.concatenate([scr[c, pl.ds(start, size, stride=stride), :] for c in range(scr.shape[0])], axis=1)


def _store_strided_rows(scr, start, stride, x):
    for c in range(scr.shape[0]):
        scr[c, pl.ds(start, x.shape[0], stride=stride), :] = x[:, c * LANES:(c + 1) * LANES]


def _fourier_1_kernel(w_ref, u_ref, a_ref, *scratch, major, chunks):
    if not scratch:
        for c in range(chunks):
            rows = slice(c * DFT_CHUNK, (c + 1) * DFT_CHUNK)
            u2d = u_ref[0, :, rows, :].reshape(major * DFT_CHUNK, F_WIDTH)
            res = jnp.dot(w_ref[...], u2d, preferred_element_type=F32)
            a_ref[0, :, :, rows, :] = res.astype(BF16).reshape(major, 2, DFT_CHUNK, F_WIDTH)
        return
    in_f32, out_f32 = scratch
    _to_lane_tiles(in_f32, u_ref[0].reshape(major * DFT_CHUNK, F_WIDTH).astype(F32))
    for j in range(DFT_CHUNK):
        rhs = _strided_rows(in_f32, j, major, DFT_CHUNK).astype(BF16)
        res = jnp.dot(w_ref[...], rhs, preferred_element_type=F32)
        _store_strided_rows(out_f32, j, DFT_CHUNK, res)
    a_ref[0] = _from_lane_tiles(out_f32).astype(BF16).reshape(major, 2, DFT_CHUNK, F_WIDTH)


def _fourier_1(u4, w1, kron):
    batch, major, minor, _ = u4.shape
    chunks = minor // DFT_CHUNK if kron else 1
    tb = chunks * DFT_CHUNK
    scratch = [] if kron else [pltpu.VMEM((F_WIDTH // LANES, major * DFT_CHUNK, LANES), F32),
                               pltpu.VMEM((F_WIDTH // LANES, 2 * major * DFT_CHUNK, LANES), F32)]
    return pl.pallas_call(
        functools.partial(_fourier_1_kernel, major=major, chunks=chunks),
        grid=(batch, minor // tb),
        in_specs=[pl.BlockSpec(w1.shape, lambda b, j: (0, 0)),
                  pl.BlockSpec((1, major, tb, F_WIDTH), lambda b, j: (b, 0, j, 0))],
        out_specs=pl.BlockSpec((1, major, 2, tb, F_WIDTH), lambda b, j: (b, 0, 0, j, 0)),
        out_shape=jax.ShapeDtypeStruct((batch, major, 2, minor, F_WIDTH), BF16),
        scratch_shapes=scratch,
        compiler_params=pltpu.CompilerParams(
            dimension_semantics=("arbitrary", "arbitrary"),
            vmem_limit_bytes=VMEM_LIMIT),
        name="fourier_1",
    )(w1, u4)


def _fourier_2_kernel(a_ref, m_ref, wc_ref, bf_ref, y_ref, x_scr, y_scr):
    for j in range(DFT_CHUNK):
        rhs = a_ref[0, j].reshape(2 * DFT_MINOR, F_WIDTH)
        x = jnp.dot(m_ref[j], rhs, preferred_element_type=F32)
        rows = slice(j * DFT_MINOR, (j + 1) * DFT_MINOR)
        x_scr[rows, :F_WIDTH] = x[:DFT_MINOR].astype(BF16)
        x_scr[rows, F_WIDTH:] = x[DFT_MINOR:].astype(BF16)
    y = jnp.dot(x_scr[...], wc_ref[...], preferred_element_type=F32) + bf_ref[...]
    for j in range(DFT_CHUNK):
        _store_strided_rows(y_scr, j, DFT_CHUNK, y[j * DFT_MINOR:(j + 1) * DFT_MINOR])
    y_ref[0] = _from_lane_tiles(y_scr).astype(BF16).reshape(DFT_MINOR, DFT_CHUNK, F_WIDTH)


def _fold_channel_dft_kernel(cs_ref, wf_ref, wc_ref):
    wc_ref[...] = jnp.dot(cs_ref[...], wf_ref[...], preferred_element_type=F32,
                          precision=lax.Precision.HIGHEST).astype(BF16)


def _fold_channel_dft(w_f):
    c = np.arange(F_GROUP_DIM)
    ang = 2.0 * np.pi * np.outer(c, c) / F_GROUP_DIM
    eye = np.eye(F_GROUPS)
    cs = np.concatenate([np.kron(eye, np.cos(ang)), np.kron(eye, np.sin(ang))], axis=0)
    return pl.pallas_call(
        _fold_channel_dft_kernel,
        out_shape=jax.ShapeDtypeStruct((2 * F_WIDTH, F_WIDTH), BF16),
        name="fold_channel_dft",
    )(jnp.asarray(cs, F32), w_f.astype(F32))


def _fourier_2(a5, m_tab, w_c, b_f):
    batch, major, _, minor, _ = a5.shape
    tr = DFT_CHUNK
    return pl.pallas_call(
        _fourier_2_kernel,
        grid=(batch, major // tr),
        in_specs=[pl.BlockSpec((1, tr, 2, minor, F_WIDTH), lambda b, r: (b, r, 0, 0, 0)),
                  pl.BlockSpec((tr, 2 * minor, 2 * minor), lambda b, r: (r, 0, 0)),
                  pl.BlockSpec(w_c.shape, lambda b, r: (0, 0)),
                  pl.BlockSpec((1, F_WIDTH), lambda b, r: (0, 0))],
        out_specs=pl.BlockSpec((1, minor, tr, F_WIDTH), lambda b, r: (b, 0, r, 0)),
        out_shape=jax.ShapeDtypeStruct((batch, minor, major, F_WIDTH), BF16),
        scratch_shapes=[pltpu.VMEM((tr * minor, 2 * F_WIDTH), BF16),
                        pltpu.VMEM((F_WIDTH // LANES, minor * tr, LANES), F32)],
        compiler_params=pltpu.CompilerParams(
            dimension_semantics=("arbitrary", "arbitrary"),
            vmem_limit_bytes=VMEM_LIMIT),
        name="fourier_2",
    )(a5, m_tab, w_c, b_f)


@functools.lru_cache(maxsize=None)
def _dft_constants(seq):
    major = seq // DFT_MINOR
    kron = major * DFT_CHUNK <= 256
    a = np.arange(major)
    ang1 = 2.0 * np.pi * np.outer(a, a) / major
    w1 = np.stack([np.cos(ang1), -np.sin(ang1)], axis=1).reshape(2 * major, major)
    if kron:
        w1 = np.kron(w1, np.eye(DFT_CHUNK))
    r = np.arange(major)[:, None, None]
    p = np.arange(DFT_MINOR)[None, :, None]
    b = np.arange(DFT_MINOR)[None, None, :]
    ang2 = 2.0 * np.pi * ((b * (r + major * p)) % seq) / seq
    scale = 1.0 / np.sqrt(seq * F_GROUP_DIM)
    e_re, e_im = np.cos(ang2) * scale, -np.sin(ang2) * scale
    m_tab = np.concatenate([np.concatenate([e_re, -e_im], axis=2),
                            np.concatenate([e_im, e_re], axis=2)], axis=1)
    return kron, np.asarray(w1, np.float32), np.asarray(m_tab, np.float32)


def _fourier(u, w_c, b_f, batch, seq):
    major = seq // DFT_MINOR
    kron, w1, m_tab = _dft_constants(seq)
    a = _fourier_1(u.reshape(batch, major, DFT_MINOR, F_WIDTH), jnp.asarray(w1).astype(BF16), kron)
    y = _fourier_2(a, jnp.asarray(m_tab).astype(BF16), w_c, b_f)
    return y.reshape(batch * seq, F_WIDTH)


def _out_proj_kernel(ot_ref, zat_ref, yf_ref, zf_ref, x_ref, wa_ref, wf_ref,
                     gna_ref, gf_ref, gp_ref, o_ref):
    y_t = jnp.concatenate([ot_ref[i] for i in range(ot_ref.shape[0])], axis=1).astype(F32)
    inv = lax.rsqrt(jnp.mean(y_t * y_t, axis=0, keepdims=True) + RMS_EPS)
    mixed_at = _gate(y_t * inv * gna_ref[...], zat_ref[0])
    mixed_f = _gate(_rms(yf_ref[...].astype(F32), gf_ref[...]), zf_ref[...])
    out = lax.dot_general(mixed_at, wa_ref[...], (((0,), (0,)), ((), ())),
                          preferred_element_type=F32)
    out = out + jnp.dot(mixed_f, wf_ref[...], preferred_element_type=F32)
    o_ref[...] = x_ref[...] + _rms(out, gp_ref[...])


def _out_proj(o_t, za_t, y_f, z_f, x2d, w_a, w_fo, g_na_col, g_f, g_post, tm=TOKEN_BLOCK):
    t = x2d.shape[0]
    tok = lambda i: (i, 0)
    blk = lambda i: (i, 0, 0)
    const = lambda i: (0, 0)
    return pl.pallas_call(
        _out_proj_kernel,
        grid=(t // tm,),
        in_specs=[pl.BlockSpec((tm // Q_TOK, NA_WIDTH, Q_TOK), blk),
                  pl.BlockSpec((1, NA_WIDTH, tm), blk),
                  pl.BlockSpec((tm, F_WIDTH), tok),
                  pl.BlockSpec((tm, F_WIDTH), tok),
                  pl.BlockSpec((tm, D_MODEL), tok),
                  pl.BlockSpec(w_a.shape, const),
                  pl.BlockSpec(w_fo.shape, const),
                  pl.BlockSpec((NA_WIDTH, 1), const),
                  pl.BlockSpec((1, F_WIDTH), const),
                  pl.BlockSpec((1, D_MODEL), const)],
        out_specs=pl.BlockSpec((tm, D_MODEL), tok),
        out_shape=jax.ShapeDtypeStruct((t, D_MODEL), F32),
        compiler_params=pltpu.CompilerParams(
            dimension_semantics=("arbitrary",), vmem_limit_bytes=VMEM_LIMIT),
        name="out_proj",
    )(o_t, za_t, y_f, z_f, x2d, w_a, w_fo, g_na_col, g_f, g_post)


def _encoder_layer(x, w_nat, w_tr, bias, w_c, b_f, g_pre, g_na_col, g_f, w_oa, w_of, g_post):
    batch, seq, _ = x.shape
    x2d = x.reshape(batch * seq, D_MODEL)
    k, u_f, z_f, q_t, v_t, za_t = _in_proj(x2d, g_pre, w_nat, w_tr)
    o_t = _attention(q_t, k, v_t, bias, batch, seq)
    y_f = _fourier(u_f, w_c, b_f, batch, seq)
    out = _out_proj(o_t, za_t, y_f, z_f, x2d, w_oa, w_of, g_na_col, g_f, g_post)
    return out.reshape(batch, seq, D_MODEL)


def kernel(x_prompt, x_sample, w_in, rpb, w_fourier, b_fourier, g_pre, g_na, g_f, w_out, g_post):
    depth = w_in.shape[0]
    y_prompt, y_sample = x_prompt, x_sample
    scale = NA_HEAD_DIM ** -0.5
    for l in range(depth):
        w = w_in[l]
        w_q, w_k, w_v, w_za, w_uf, w_zf = (w[:, i * NA_WIDTH:(i + 1) * NA_WIDTH] for i in range(6))
        w_nat = jnp.concatenate([w_k, w_uf, w_zf], axis=1).astype(BF16)
        w_tr = jnp.concatenate([w_q * scale, w_v, w_za], axis=1).T.astype(BF16)
        bias = _bias_table(rpb[l])
        row = lambda v: v.reshape(1, -1).astype(F32)
        args = (w_nat, w_tr, bias, _fold_channel_dft(w_fourier[l]), row(b_fourier[l]),
                row(g_pre[l]), g_na[l].reshape(-1, 1).astype(F32), row(g_f[l]),
                w_out[l][:NA_WIDTH].astype(BF16), w_out[l][NA_WIDTH:].astype(BF16), row(g_post[l]))
        y_prompt = _encoder_layer(y_prompt, *args)
        y_sample = _encoder_layer(y_sample, *args)
    return (y_prompt, y_sample)
```

```python
import functools

import numpy as np
import jax
import jax.numpy as jnp
from jax import lax
from jax.experimental import pallas as pl
from jax.experimental.pallas import tpu as pltpu

D_MODEL = 1024
GRID_W = 64
WIN_ROWS = 8
WIN_COLS = 16
NA_HEADS = 8
NA_HEAD_DIM = 64
NA_WIDTH = NA_HEADS * NA_HEAD_DIM
F_GROUPS = 8
F_GROUP_DIM = 64
F_WIDTH = F_GROUPS * F_GROUP_DIM
RMS_EPS = 1e-6
NEG_INF = -1e30

Q_ROWS = 4
Q_TOK = Q_ROWS * GRID_W
KV_GROUPS = 3
KV_TOK = KV_GROUPS * Q_TOK
KEY_ROWS = KV_GROUPS * Q_ROWS
ROW_SLOTS = 2 * WIN_ROWS
DFT_MINOR = 128
TOKEN_BLOCK = 1024
VMEM_LIMIT = 56 * 1024 * 1024

BF16 = jnp.bfloat16
F32 = jnp.float32


def _rms(x, g):
    inv = lax.rsqrt(jnp.mean(x * x, axis=-1, keepdims=True) + RMS_EPS)
    return x * inv * g


def _silu(z):
    return z * (1.0 / (1.0 + jnp.exp(-z)))


def _gate(y_normed, z):
    return y_normed.astype(BF16) * _silu(z)


def _in_proj_kernel(x_ref, g_ref, wn_ref, wt_ref,
                    k_ref, uf_ref, zf_ref, qt_ref, vt_ref, zat_ref):
    h = _rms(x_ref[...], g_ref[...]).astype(BF16)
    nat = jnp.dot(h, wn_ref[...], preferred_element_type=F32)
    k_ref[...] = nat[:, 0 * NA_WIDTH:1 * NA_WIDTH].astype(BF16)
    uf_ref[...] = nat[:, 1 * NA_WIDTH:2 * NA_WIDTH].astype(BF16)
    zf_ref[...] = nat[:, 2 * NA_WIDTH:3 * NA_WIDTH].astype(BF16)
    tr = lax.dot_general(wt_ref[...], h, (((1,), (1,)), ((), ())),
                         preferred_element_type=F32)
    for i in range(qt_ref.shape[0]):
        tok = slice(i * Q_TOK, (i + 1) * Q_TOK)
        qt_ref[i] = tr[0 * NA_WIDTH:1 * NA_WIDTH, tok].astype(BF16)
        vt_ref[i] = tr[1 * NA_WIDTH:2 * NA_WIDTH, tok].astype(BF16)
    zat_ref[0] = tr[2 * NA_WIDTH:3 * NA_WIDTH].astype(BF16)


def _in_proj(x2d, g_pre, w_nat, w_tr, tm=TOKEN_BLOCK):
    t = x2d.shape[0]
    tok = lambda i: (i, 0)
    blk = lambda i: (i, 0, 0)
    const = lambda i: (0, 0)
    nat_shape = jax.ShapeDtypeStruct((t, NA_WIDTH), BF16)
    qv_shape = jax.ShapeDtypeStruct((t // Q_TOK, NA_WIDTH, Q_TOK), BF16)
    za_shape = jax.ShapeDtypeStruct((t // tm, NA_WIDTH, tm), BF16)
    return pl.pallas_call(
        _in_proj_kernel,
        grid=(t // tm,),
        in_specs=[pl.BlockSpec((tm, D_MODEL), tok),
                  pl.BlockSpec((1, D_MODEL), const),
                  pl.BlockSpec(w_nat.shape, const),
                  pl.BlockSpec(w_tr.shape, const)],
        out_specs=[pl.BlockSpec((tm, NA_WIDTH), tok)] * 3
                  + [pl.BlockSpec((tm // Q_TOK, NA_WIDTH, Q_TOK), blk)] * 2
                  + [pl.BlockSpec((1, NA_WIDTH, tm), blk)],
        out_shape=[nat_shape] * 3 + [qv_shape] * 2 + [za_shape],
        compiler_params=pltpu.CompilerParams(
            dimension_semantics=("arbitrary",), vmem_limit_bytes=VMEM_LIMIT),
        name="in_proj",
    )(x2d, g_pre, w_nat, w_tr)


def _bias_table_kernel(rpb_ref, t2_ref):
    head = pl.program_id(0)
    shape = (GRID_W, 2 * GRID_W)
    kc = lax.broadcasted_iota(jnp.int32, shape, 0)
    lane = lax.broadcasted_iota(jnp.int32, shape, 1)
    second = lane >= GRID_W
    qc = jnp.where(second, lane - GRID_W, lane)
    col_off = kc - qc + (WIN_COLS - 1)
    win_start = jnp.clip(qc - WIN_COLS // 2, 0, GRID_W - WIN_COLS)
    col_valid = (kc >= win_start) & (kc < win_start + WIN_COLS)
    n_off = 2 * WIN_COLS - 1
    n_row = 2 * WIN_ROWS - 1

    def entry(d, o):
        if 0 <= d < n_row:
            return rpb_ref[(head * n_row + d) * n_off + o]
        return jnp.float32(NEG_INF)

    for d in range(ROW_SLOTS):
        tile = jnp.full(shape, NEG_INF, F32)
        for o in range(n_off):
            val = jnp.where(second, entry(d - 1, o), entry(d, o))
            tile = jnp.where(col_off == o, val, tile)
        t2_ref[0, d] = jnp.where(col_valid, tile, NEG_INF)


def _bias_table(rpb):
    return pl.pallas_call(
        _bias_table_kernel,
        grid=(NA_HEADS,),
        in_specs=[pl.BlockSpec(memory_space=pltpu.SMEM)],
        out_specs=pl.BlockSpec((1, ROW_SLOTS, GRID_W, 2 * GRID_W), lambda h: (h, 0, 0, 0)),
        out_shape=jax.ShapeDtypeStruct((NA_HEADS, ROW_SLOTS, GRID_W, 2 * GRID_W), F32),
        compiler_params=pltpu.CompilerParams(dimension_semantics=("arbitrary",)),
        name="bias_table",
    )(rpb.astype(F32).reshape(-1))


def _token_range(refs, lo, hi, other, axis):
    pieces = []
    for idx, ref in enumerate(refs):
        a, b = max(lo, idx * Q_TOK), min(hi, (idx + 1) * Q_TOK)
        if a < b:
            tok = slice(a - idx * Q_TOK, b - idx * Q_TOK)
            pieces.append(ref[tok, other] if axis == 0 else ref[0, other, tok])
    return pieces[0] if len(pieces) == 1 else jnp.concatenate(pieces, axis=axis)


def _window_plan(variant, jp):
    if variant == 0:
        return [(i, i - 2 * jp + WIN_ROWS - 1, None) for i in range(WIN_ROWS)]
    if variant == 2:
        q_row = KEY_ROWS - Q_ROWS + 2 * jp
        return [(i, i - q_row + WIN_ROWS - 1, None) for i in range(KEY_ROWS - WIN_ROWS, KEY_ROWS)]
    q_row = Q_ROWS + 2 * jp
    plan = []
    for i in range(q_row - WIN_ROWS // 2, q_row + WIN_ROWS // 2 + 1):
        half = "low" if i == q_row - WIN_ROWS // 2 else "high" if i == q_row + WIN_ROWS // 2 else None
        plan.append((i, i - q_row + WIN_ROWS - 1, half))
    return plan


def _attention_kernel(qt_ref, k0_ref, k1_ref, k2_ref, v0_ref, v1_ref, v2_ref,
                      t2_ref, out_ref, sb_ref, p_ref, *, groups):
    k_refs = (k0_ref, k1_ref, k2_ref)
    v_refs = (v0_ref, v1_ref, v2_ref)
    group = pl.program_id(0)
    pair_lanes = 2 * NA_HEAD_DIM
    q_lanes = 2 * GRID_W
    n_pairs = NA_HEADS // 2

    def body(variant):
        plans = [_window_plan(variant, jp) for jp in range(Q_ROWS // 2)]
        key_lo = min(p[0][0] for p in plans)
        key_top = max(p[-1][0] for p in plans) + 1
        key_hi = key_top + (key_top - key_lo) % 2
        lane = lax.broadcasted_iota(jnp.int32, (1, q_lanes), 1)
        half_mask = {"low": jnp.where(lane < GRID_W, 0.0, NEG_INF).astype(F32),
                     "high": jnp.where(lane >= GRID_W, 0.0, NEG_INF).astype(F32)}
        ones_rows = jnp.ones((16, (key_hi - key_lo) * GRID_W), BF16)
        zero_blk = jnp.zeros((GRID_W, q_lanes), BF16)

        def scores(pair, slot):
            cols = slice(pair * pair_lanes, (pair + 1) * pair_lanes)
            k_pair = _token_range(k_refs, key_lo * GRID_W, key_top * GRID_W, cols, 0)
            qt_pair = qt_ref[0, cols, :]
            zero = jnp.zeros((NA_HEAD_DIM, Q_TOK), BF16)
            qt_both = jnp.concatenate(
                [jnp.concatenate([qt_pair[:NA_HEAD_DIM], zero], axis=0),
                 jnp.concatenate([zero, qt_pair[NA_HEAD_DIM:]], axis=0)], axis=1)
            s = jnp.dot(k_pair, qt_both, preferred_element_type=F32)
            maxima = []
            for sub in range(2):
                for jp, plan in enumerate(plans):
                    lanes = slice(sub * Q_TOK + jp * q_lanes, sub * Q_TOK + (jp + 1) * q_lanes)
                    m_acc = None
                    for i, slot_d, half in plan:
                        rows = slice((i - key_lo) * GRID_W, (i - key_lo + 1) * GRID_W)
                        blk = s[rows, lanes] + t2_ref[2 * pair + sub, slot_d]
                        if half is not None:
                            blk = blk + half_mask[half]
                        sb_ref[slot, rows, lanes] = blk
                        m_acc = blk if m_acc is None else jnp.maximum(m_acc, blk)
                    maxima.append(jnp.max(m_acc, axis=0, keepdims=True))
            return maxima

        def softmax_pv(pair, slot, maxima):
            n_keys = (key_hi - key_lo) * GRID_W
            for sub in range(2):
                head = 2 * pair + sub
                for jp, plan in enumerate(plans):
                    lanes = slice(sub * Q_TOK + jp * q_lanes, sub * Q_TOK + (jp + 1) * q_lanes)
                    m = maxima[2 * sub + jp]
                    live = {i for i, _, _ in plan}
                    for i in range(key_lo, key_hi):
                        rows = slice((i - key_lo) * GRID_W, (i - key_lo + 1) * GRID_W)
                        if i in live:
                            p_ref[slot, rows, lanes] = jnp.exp((sb_ref[slot, rows, lanes] - m).astype(BF16))
                        else:
                            p_ref[slot, rows, lanes] = zero_blk
                rows = slice(head * NA_HEAD_DIM, (head + 1) * NA_HEAD_DIM)
                vt_head = _token_range(v_refs, key_lo * GRID_W, key_hi * GRID_W, rows, 1)
                vt_ext = jnp.concatenate([vt_head, ones_rows], axis=0)
                o = jnp.dot(vt_ext, p_ref[slot, :n_keys, sub * Q_TOK:(sub + 1) * Q_TOK],
                            preferred_element_type=F32)
                out_ref[0, rows, :] = (o[:NA_HEAD_DIM] * (1.0 / o[NA_HEAD_DIM:NA_HEAD_DIM + 1])).astype(BF16)

        maxima = scores(0, 0)
        for pair in range(n_pairs):
            nxt = scores(pair + 1, (pair + 1) % 2) if pair + 1 < n_pairs else None
            softmax_pv(pair, pair % 2, maxima)
            maxima = nxt

    is_top = group == 0
    is_bot = group == groups - 1
    pl.when(is_top)(lambda: body(0))
    pl.when(jnp.logical_not(is_top | is_bot))(lambda: body(1))
    pl.when(is_bot)(lambda: body(2))


def _attention(qt, k, vt, t2, batch, seq):
    t = batch * seq
    groups = seq // Q_TOK
    assert groups >= KV_GROUPS

    def first_kv(u):
        return jnp.clip(u - 1, 0, groups - KV_GROUPS)

    def blk(u, b):
        return (b * groups + u, 0, 0)

    def kv(i):
        return lambda u, b: (b * groups + first_kv(u) + i, 0)

    def kv_blk(i):
        return lambda u, b: (b * groups + first_kv(u) + i, 0, 0)

    return pl.pallas_call(
        functools.partial(_attention_kernel, groups=groups),
        grid=(groups, batch),
        in_specs=[pl.BlockSpec((1, NA_WIDTH, Q_TOK), blk)]
                 + [pl.BlockSpec((Q_TOK, NA_WIDTH), kv(i)) for i in range(KV_GROUPS)]
                 + [pl.BlockSpec((1, NA_WIDTH, Q_TOK), kv_blk(i)) for i in range(KV_GROUPS)]
                 + [pl.BlockSpec(t2.shape, lambda u, b: (0, 0, 0, 0))],
        out_specs=pl.BlockSpec((1, NA_WIDTH, Q_TOK), blk),
        out_shape=jax.ShapeDtypeStruct((t // Q_TOK, NA_WIDTH, Q_TOK), BF16),
        scratch_shapes=[pltpu.VMEM((2, KV_TOK, 2 * Q_TOK), F32),
                        pltpu.VMEM((2, KV_TOK, 2 * Q_TOK), BF16)],
        compiler_params=pltpu.CompilerParams(
            dimension_semantics=("arbitrary", "arbitrary"),
            vmem_limit_bytes=VMEM_LIMIT),
        name="attention",
    )(qt, k, k, k, vt, vt, vt, t2)


DFT_CHUNK = 16
LANES = 128


def _to_lane_tiles(scr, x):
    for c in range(scr.shape[0]):
        scr[c] = x[:, c * LANES:(c + 1) * LANES]


def _from_lane_tiles(scr):
    return jnp.concatenate([scr[c] for c in range(scr.shape[0])], axis=1)


def _strided_rows(scr, start, size, stride):
    return jnp.concatenate([scr[c, pl.ds(start, size, stride=stride), :] for c in range(scr.shape[0])], axis=1)


def _store_strided_rows(scr, start, stride, x):
    for c in range(scr.shape[0]):
        scr[c, pl.ds(start, x.shape[0], stride=stride), :] = x[:, c * LANES:(c + 1) * LANES]


def _fourier_1_kernel(w_ref, u_ref, a_ref, *scratch, major, chunks):
    if not scratch:
        for c in range(chunks):
            rows = slice(c * DFT_CHUNK, (c + 1) * DFT_CHUNK)
            u2d = u_ref[0, :, rows, :].reshape(major * DFT_CHUNK, F_WIDTH)
            res = jnp.dot(w_ref[...], u2d, preferred_element_type=F32)
            a_ref[0, :, :, rows, :] = res.astype(BF16).reshape(major, 2, DFT_CHUNK, F_WIDTH)
        return
    in_w, out_w = scratch
    pairs = DFT_CHUNK // 2
    _to_lane_tiles(in_w, pltpu.bitcast(u_ref[0].reshape(major * DFT_CHUNK, F_WIDTH), jnp.uint32))
    for j in range(pairs):
        rhs = pltpu.bitcast(_strided_rows(in_w, j, major, pairs), BF16)
        res = jnp.dot(w_ref[...], rhs, preferred_element_type=F32)
        _store_strided_rows(out_w, j, pairs, pltpu.bitcast(res.astype(BF16), jnp.uint32))
    a_ref[0] = pltpu.bitcast(_from_lane_tiles(out_w), BF16).reshape(major, 2, DFT_CHUNK, F_WIDTH)


def _fourier_1(u4, w1, kron):
    batch, major, minor, _ = u4.shape
    chunks = minor // DFT_CHUNK if kron else 1
    tb = chunks * DFT_CHUNK
    scratch = [] if kron else [pltpu.VMEM((F_WIDTH // LANES, major * DFT_CHUNK // 2, LANES), jnp.uint32),
                               pltpu.VMEM((F_WIDTH // LANES, major * DFT_CHUNK, LANES), jnp.uint32)]
    return pl.pallas_call(
        functools.partial(_fourier_1_kernel, major=major, chunks=chunks),
        grid=(batch, minor // tb),
        in_specs=[pl.BlockSpec(w1.shape, lambda b, j: (0, 0)),
                  pl.BlockSpec((1, major, tb, F_WIDTH), lambda b, j: (b, 0, j, 0))],
        out_specs=pl.BlockSpec((1, major, 2, tb, F_WIDTH), lambda b, j: (b, 0, 0, j, 0)),
        out_shape=jax.ShapeDtypeStruct((batch, major, 2, minor, F_WIDTH), BF16),
        scratch_shapes=scratch,
        compiler_params=pltpu.CompilerParams(
            dimension_semantics=("arbitrary", "arbitrary"),
            vmem_limit_bytes=VMEM_LIMIT),
        name="fourier_1",
    )(w1, u4)


def _fourier_2_kernel(a_ref, m_ref, wc_ref, bf_ref, y_ref, x_scr, y_scr):
    for j in range(DFT_CHUNK):
        rhs = a_ref[0, j].reshape(2 * DFT_MINOR, F_WIDTH)
        x = jnp.dot(m_ref[j], rhs, preferred_element_type=F32)
        rows = slice(j * DFT_MINOR, (j + 1) * DFT_MINOR)
        x_scr[rows, :F_WIDTH] = x[:DFT_MINOR].astype(BF16)
        x_scr[rows, F_WIDTH:] = x[DFT_MINOR:].astype(BF16)
    y = jnp.dot(x_scr[...], wc_ref[...], preferred_element_type=F32) + bf_ref[...]
    for j in range(DFT_CHUNK):
        _store_strided_rows(y_scr, j, DFT_CHUNK, y[j * DFT_MINOR:(j + 1) * DFT_MINOR])
    y_ref[0] = _from_lane_tiles(y_scr).astype(BF16).reshape(DFT_MINOR, DFT_CHUNK, F_WIDTH)


def _fold_channel_dft_kernel(cs_ref, wf_ref, wc_ref):
    wc_ref[...] = jnp.dot(cs_ref[...], wf_ref[...], preferred_element_type=F32,
                          precision=lax.Precision.HIGHEST).astype(BF16)


def _fold_channel_dft(w_f):
    c = np.arange(F_GROUP_DIM)
    ang = 2.0 * np.pi * np.outer(c, c) / F_GROUP_DIM
    eye = np.eye(F_GROUPS)
    cs = np.concatenate([np.kron(eye, np.cos(ang)), np.kron(eye, np.sin(ang))], axis=0)
    return pl.pallas_call(
        _fold_channel_dft_kernel,
        out_shape=jax.ShapeDtypeStruct((2 * F_WIDTH, F_WIDTH), BF16),
        name="fold_channel_dft",
    )(jnp.asarray(cs, F32), w_f.astype(F32))


def _fourier_2(a5, m_tab, w_c, b_f):
    batch, major, _, minor, _ = a5.shape
    tr = DFT_CHUNK
    return pl.pallas_call(
        _fourier_2_kernel,
        grid=(batch, major // tr),
        in_specs=[pl.BlockSpec((1, tr, 2, minor, F_WIDTH), lambda b, r: (b, r, 0, 0, 0)),
                  pl.BlockSpec((tr, 2 * minor, 2 * minor), lambda b, r: (r, 0, 0)),
                  pl.BlockSpec(w_c.shape, lambda b, r: (0, 0)),
                  pl.BlockSpec((1, F_WIDTH), lambda b, r: (0, 0))],
        out_specs=pl.BlockSpec((1, minor, tr, F_WIDTH), lambda b, r: (b, 0, r, 0)),
        out_shape=jax.ShapeDtypeStruct((batch, minor, major, F_WIDTH), BF16),
        scratch_shapes=[pltpu.VMEM((tr * minor, 2 * F_WIDTH), BF16),
                        pltpu.VMEM((F_WIDTH // LANES, minor * tr, LANES), F32)],
        compiler_params=pltpu.CompilerParams(
            dimension_semantics=("arbitrary", "arbitrary"),
            vmem_limit_bytes=VMEM_LIMIT),
        name="fourier_2",
    )(a5, m_tab, w_c, b_f)


@functools.lru_cache(maxsize=None)
def _dft_constants(seq):
    major = seq // DFT_MINOR
    kron = major * DFT_CHUNK <= 256
    a = np.arange(major)
    ang1 = 2.0 * np.pi * np.outer(a, a) / major
    w1 = np.stack([np.cos(ang1), -np.sin(ang1)], axis=1).reshape(2 * major, major)
    w1 = np.kron(w1, np.eye(DFT_CHUNK if kron else 2))
    r = np.arange(major)[:, None, None]
    p = np.arange(DFT_MINOR)[None, :, None]
    b = np.arange(DFT_MINOR)[None, None, :]
    ang2 = 2.0 * np.pi * ((b * (r + major * p)) % seq) / seq
    scale = 1.0 / np.sqrt(seq * F_GROUP_DIM)
    e_re, e_im = np.cos(ang2) * scale, -np.sin(ang2) * scale
    m_tab = np.concatenate([np.concatenate([e_re, -e_im], axis=2),
                            np.concatenate([e_im, e_re], axis=2)], axis=1)
    return kron, np.asarray(w1, np.float32), np.asarray(m_tab, np.float32)


def _fourier(u, w_c, b_f, batch, seq):
    major = seq // DFT_MINOR
    kron, w1, m_tab = _dft_constants(seq)
    a = _fourier_1(u.reshape(batch, major, DFT_MINOR, F_WIDTH), jnp.asarray(w1).astype(BF16), kron)
    y = _fourier_2(a, jnp.asarray(m_tab).astype(BF16), w_c, b_f)
    return y.reshape(batch * seq, F_WIDTH)


def _out_proj_kernel(ot_ref, zat_ref, yf_ref, zf_ref, x_ref, wa_ref, wf_ref,
                     gna_ref, gf_ref, gp_ref, o_ref):
    y_t = jnp.concatenate([ot_ref[i] for i in range(ot_ref.shape[0])], axis=1).astype(F32)
    inv = lax.rsqrt(jnp.mean(y_t * y_t, axis=0, keepdims=True) + RMS_EPS)
    mixed_at = _gate(y_t * inv * gna_ref[...], zat_ref[0])
    mixed_f = _gate(_rms(yf_ref[...].astype(F32), gf_ref[...]), zf_ref[...])
    out = lax.dot_general(mixed_at, wa_ref[...], (((0,), (0,)), ((), ())),
                          preferred_element_type=F32)
    out = out + jnp.dot(mixed_f, wf_ref[...], preferred_element_type=F32)
    o_ref[...] = x_ref[...] + _rms(out, gp_ref[...])


def _out_proj(o_t, za_t, y_f, z_f, x2d, w_a, w_fo, g_na_col, g_f, g_post, tm=TOKEN_BLOCK):
    t = x2d.shape[0]
    tok = lambda i: (i, 0)
    blk = lambda i: (i, 0, 0)
    const = lambda i: (0, 0)
    return pl.pallas_call(
        _out_proj_kernel,
        grid=(t // tm,),
        in_specs=[pl.BlockSpec((tm // Q_TOK, NA_WIDTH, Q_TOK), blk),
                  pl.BlockSpec((1, NA_WIDTH, tm), blk),
                  pl.BlockSpec((tm, F_WIDTH), tok),
                  pl.BlockSpec((tm, F_WIDTH), tok),
                  pl.BlockSpec((tm, D_MODEL), tok),
                  pl.BlockSpec(w_a.shape, const),
                  pl.BlockSpec(w_fo.shape, const),
                  pl.BlockSpec((NA_WIDTH, 1), const),
                  pl.BlockSpec((1, F_WIDTH), const),
                  pl.BlockSpec((1, D_MODEL), const)],
        out_specs=pl.BlockSpec((tm, D_MODEL), tok),
        out_shape=jax.ShapeDtypeStruct((t, D_MODEL), F32),
        compiler_params=pltpu.CompilerParams(
            dimension_semantics=("arbitrary",), vmem_limit_bytes=VMEM_LIMIT),
        name="out_proj",
    )(o_t, za_t, y_f, z_f, x2d, w_a, w_fo, g_na_col, g_f, g_post)


def _encoder_layer(x, w_nat, w_tr, bias, w_c, b_f, g_pre, g_na_col, g_f, w_oa, w_of, g_post):
    batch, seq, _ = x.shape
    x2d = x.reshape(batch * seq, D_MODEL)
    k, u_f, z_f, q_t, v_t, za_t = _in_proj(x2d, g_pre, w_nat, w_tr)
    o_t = _attention(q_t, k, v_t, bias, batch, seq)
    y_f = _fourier(u_f, w_c, b_f, batch, seq)
    out = _out_proj(o_t, za_t, y_f, z_f, x2d, w_oa, w_of, g_na_col, g_f, g_post)
    return out.reshape(batch, seq, D_MODEL)


def kernel(x_prompt, x_sample, w_in, rpb, w_fourier, b_fourier, g_pre, g_na, g_f, w_out, g_post):
    depth = w_in.shape[0]
    y_prompt, y_sample = x_prompt, x_sample
    scale = NA_HEAD_DIM ** -0.5
    for l in range(depth):
        w = w_in[l]
        w_q, w_k, w_v, w_za, w_uf, w_zf = (w[:, i * NA_WIDTH:(i + 1) * NA_WIDTH] for i in range(6))
        w_nat = jnp.concatenate([w_k, w_uf, w_zf], axis=1).astype(BF16)
        w_tr = jnp.concatenate([w_q * scale, w_v, w_za], axis=1).T.astype(BF16)
        bias = _bias_table(rpb[l])
        row = lambda v: v.reshape(1, -1).astype(F32)
        args = (w_nat, w_tr, bias, _fold_channel_dft(w_fourier[l]), row(b_fourier[l]),
                row(g_pre[l]), g_na[l].reshape(-1, 1).astype(F32), row(g_f[l]),
                w_out[l][:NA_WIDTH].astype(BF16), w_out[l][NA_WIDTH:].astype(BF16), row(g_post[l]))
        y_prompt = _encoder_layer(y_prompt, *args)
        y_sample = _encoder_layer(y_sample, *args)
    return (y_prompt, y_sample)
```

```python
import functools

import numpy as np
import jax
import jax.numpy as jnp
from jax import lax
from jax.experimental import pallas as pl
from jax.experimental.pallas import tpu as pltpu

D_MODEL = 1024
GRID_W = 64
WIN_ROWS = 8
WIN_COLS = 16
NA_HEADS = 8
NA_HEAD_DIM = 64
NA_WIDTH = NA_HEADS * NA_HEAD_DIM
F_GROUPS = 8
F_GROUP_DIM = 64
F_WIDTH = F_GROUPS * F_GROUP_DIM
RMS_EPS = 1e-6
NEG_INF = -1e30

Q_ROWS = 4
Q_TOK = Q_ROWS * GRID_W
STEP_SUBS = 2
KV_BLOCKS = 4
KEY_ROWS = 3 * Q_ROWS
ROW_SLOTS = 2 * WIN_ROWS
DFT_MINOR = 128
TOKEN_BLOCK = 1024
VMEM_LIMIT = 56 * 1024 * 1024

BF16 = jnp.bfloat16
F32 = jnp.float32


def _rms(x, g):
    inv = lax.rsqrt(jnp.mean(x * x, axis=-1, keepdims=True) + RMS_EPS)
    return x * inv * g


def _silu(z):
    return z * (1.0 / (1.0 + jnp.exp(-z)))


def _gate(y_normed, z):
    return y_normed.astype(BF16) * _silu(z)


def _in_proj_kernel(x_ref, g_ref, wn_ref, wt_ref,
                    k_ref, uf_ref, zf_ref, qt_ref, vt_ref, zat_ref):
    h = _rms(x_ref[...], g_ref[...]).astype(BF16)
    nat = jnp.dot(h, wn_ref[...], preferred_element_type=F32)
    k_ref[...] = nat[:, 0 * NA_WIDTH:1 * NA_WIDTH].astype(BF16)
    uf_ref[...] = nat[:, 1 * NA_WIDTH:2 * NA_WIDTH].astype(BF16)
    zf_ref[...] = nat[:, 2 * NA_WIDTH:3 * NA_WIDTH].astype(BF16)
    tr = lax.dot_general(wt_ref[...], h, (((1,), (1,)), ((), ())),
                         preferred_element_type=F32)
    for i in range(qt_ref.shape[0]):
        tok = slice(i * Q_TOK, (i + 1) * Q_TOK)
        qt_ref[i] = tr[0 * NA_WIDTH:1 * NA_WIDTH, tok].astype(BF16)
        vt_ref[i] = tr[1 * NA_WIDTH:2 * NA_WIDTH, tok].astype(BF16)
    zat_ref[0] = tr[2 * NA_WIDTH:3 * NA_WIDTH].astype(BF16)


def _in_proj(x2d, g_pre, w_nat, w_tr, tm=TOKEN_BLOCK):
    t = x2d.shape[0]
    tok = lambda i: (i, 0)
    blk = lambda i: (i, 0, 0)
    const = lambda i: (0, 0)
    nat_shape = jax.ShapeDtypeStruct((t, NA_WIDTH), BF16)
    qv_shape = jax.ShapeDtypeStruct((t // Q_TOK, NA_WIDTH, Q_TOK), BF16)
    za_shape = jax.ShapeDtypeStruct((t // tm, NA_WIDTH, tm), BF16)
    return pl.pallas_call(
        _in_proj_kernel,
        grid=(t // tm,),
        in_specs=[pl.BlockSpec((tm, D_MODEL), tok),
                  pl.BlockSpec((1, D_MODEL), const),
                  pl.BlockSpec(w_nat.shape, const),
                  pl.BlockSpec(w_tr.shape, const)],
        out_specs=[pl.BlockSpec((tm, NA_WIDTH), tok)] * 3
                  + [pl.BlockSpec((tm // Q_TOK, NA_WIDTH, Q_TOK), blk)] * 2
                  + [pl.BlockSpec((1, NA_WIDTH, tm), blk)],
        out_shape=[nat_shape] * 3 + [qv_shape] * 2 + [za_shape],
        compiler_params=pltpu.CompilerParams(
            dimension_semantics=("arbitrary",), vmem_limit_bytes=VMEM_LIMIT),
        name="in_proj",
    )(x2d, g_pre, w_nat, w_tr)


def _bias_table_kernel(rpb_ref, t2_ref):
    head = pl.program_id(0)
    shape = (GRID_W, 2 * GRID_W)
    kc = lax.broadcasted_iota(jnp.int32, shape, 0)
    lane = lax.broadcasted_iota(jnp.int32, shape, 1)
    second = lane >= GRID_W
    qc = jnp.where(second, lane - GRID_W, lane)
    col_off = kc - qc + (WIN_COLS - 1)
    win_start = jnp.clip(qc - WIN_COLS // 2, 0, GRID_W - WIN_COLS)
    col_valid = (kc >= win_start) & (kc < win_start + WIN_COLS)
    n_off = 2 * WIN_COLS - 1
    n_row = 2 * WIN_ROWS - 1

    def entry(d, o):
        if 0 <= d < n_row:
            return rpb_ref[(head * n_row + d) * n_off + o]
        return jnp.float32(NEG_INF)

    for d in range(ROW_SLOTS):
        tile = jnp.full(shape, NEG_INF, F32)
        for o in range(n_off):
            val = jnp.where(second, entry(d - 1, o), entry(d, o))
            tile = jnp.where(col_off == o, val, tile)
        t2_ref[0, d] = jnp.where(col_valid, tile, NEG_INF)


def _bias_table(rpb):
    return pl.pallas_call(
        _bias_table_kernel,
        grid=(NA_HEADS,),
        in_specs=[pl.BlockSpec(memory_space=pltpu.SMEM)],
        out_specs=pl.BlockSpec((1, ROW_SLOTS, GRID_W, 2 * GRID_W), lambda h: (h, 0, 0, 0)),
        out_shape=jax.ShapeDtypeStruct((NA_HEADS, ROW_SLOTS, GRID_W, 2 * GRID_W), F32),
        compiler_params=pltpu.CompilerParams(dimension_semantics=("arbitrary",)),
        name="bias_table",
    )(rpb.astype(F32).reshape(-1))


def _token_range(refs, lo, hi, other, axis):
    pieces = []
    for idx, ref in enumerate(refs):
        a, b = max(lo, idx * Q_TOK), min(hi, (idx + 1) * Q_TOK)
        if a < b:
            tok = slice(a - idx * Q_TOK, b - idx * Q_TOK)
            pieces.append(ref[tok, other] if axis == 0 else ref[0, other, tok])
    return pieces[0] if len(pieces) == 1 else jnp.concatenate(pieces, axis=axis)


def _window_plan(variant, jp):
    if variant == 0:
        return [(i, i - 2 * jp + WIN_ROWS - 1, None) for i in range(WIN_ROWS)]
    if variant == 2:
        q_row = KEY_ROWS - Q_ROWS + 2 * jp
        return [(i, i - q_row + WIN_ROWS - 1, None) for i in range(KEY_ROWS - WIN_ROWS, KEY_ROWS)]
    q_row = Q_ROWS + 2 * jp
    plan = []
    for i in range(q_row - WIN_ROWS // 2, q_row + WIN_ROWS // 2 + 1):
        half = "low" if i == q_row - WIN_ROWS // 2 else "high" if i == q_row + WIN_ROWS // 2 else None
        plan.append((i, i - q_row + WIN_ROWS - 1, half))
    return plan


STEP_PLANS = (((0, 0), (1, 0)),
              ((1, 0), (1, Q_ROWS)),
              ((1, Q_ROWS), (2, Q_ROWS)))


def _attention_kernel(qt_ref, *refs, steps):
    k_refs, v_refs = refs[:KV_BLOCKS], refs[KV_BLOCKS:2 * KV_BLOCKS]
    t2_ref, out_ref, sb_ref, p_ref = refs[2 * KV_BLOCKS:]
    step = pl.program_id(0)
    pair_lanes = 2 * NA_HEAD_DIM
    q_lanes = 2 * GRID_W
    n_pairs = NA_HEADS // 2

    def sub_group(sub, variant, key_off):
        plans = [_window_plan(variant, jp) for jp in range(Q_ROWS // 2)]
        key_lo = min(p[0][0] for p in plans)
        key_top = max(p[-1][0] for p in plans) + 1
        key_hi = key_top + (key_top - key_lo) % 2
        tok_lo, tok_top, tok_hi = ((key_off + r) * GRID_W for r in (key_lo, key_top, key_hi))
        lane = lax.broadcasted_iota(jnp.int32, (1, q_lanes), 1)
        half_mask = {"low": jnp.where(lane < GRID_W, 0.0, NEG_INF).astype(F32),
                     "high": jnp.where(lane >= GRID_W, 0.0, NEG_INF).astype(F32)}
        ones_rows = jnp.ones((16, tok_hi - tok_lo), BF16)
        zero_blk = jnp.zeros((GRID_W, q_lanes), BF16)

        def scores(pair, slot):
            cols = slice(pair * pair_lanes, (pair + 1) * pair_lanes)
            k_pair = _token_range(k_refs, tok_lo, tok_top, cols, 0)
            qt_pair = qt_ref[sub, cols, :]
            zero = jnp.zeros((NA_HEAD_DIM, Q_TOK), BF16)
            qt_both = jnp.concatenate(
                [jnp.concatenate([qt_pair[:NA_HEAD_DIM], zero], axis=0),
                 jnp.concatenate([zero, qt_pair[NA_HEAD_DIM:]], axis=0)], axis=1)
            s = jnp.dot(k_pair, qt_both, preferred_element_type=F32)
            maxima = []
            for head_sub in range(2):
                for jp, plan in enumerate(plans):
                    first = head_sub * Q_TOK + jp * q_lanes
                    lanes = slice(first, first + q_lanes)
                    m_acc = None
                    for i, slot_d, half in plan:
                        rows = slice((i - key_lo) * GRID_W, (i - key_lo + 1) * GRID_W)
                        blk = s[rows, lanes] + t2_ref[2 * pair + head_sub, slot_d]
                        if half is not None:
                            blk = blk + half_mask[half]
                        sb_ref[slot, rows, lanes] = blk
                        m_acc = blk if m_acc is None else jnp.maximum(m_acc, blk)
                    maxima.append(jnp.max(m_acc, axis=0, keepdims=True))
            return maxima

        def softmax_pv(pair, slot, maxima):
            for head_sub in range(2):
                head = 2 * pair + head_sub
                for jp, plan in enumerate(plans):
                    first = head_sub * Q_TOK + jp * q_lanes
                    lanes = slice(first, first + q_lanes)
                    m = maxima[2 * head_sub + jp]
                    live = {i for i, _, _ in plan}
                    for i in range(key_lo, key_hi):
                        rows = slice((i - key_lo) * GRID_W, (i - key_lo + 1) * GRID_W)
                        if i in live:
                            p_ref[slot, rows, lanes] = jnp.exp((sb_ref[slot, rows, lanes] - m).astype(BF16))
                        else:
                            p_ref[slot, rows, lanes] = zero_blk
                rows = slice(head * NA_HEAD_DIM, (head + 1) * NA_HEAD_DIM)
                vt_head = _token_range(v_refs, tok_lo, tok_hi, rows, 1)
                vt_ext = jnp.concatenate([vt_head, ones_rows], axis=0)
                o = jnp.dot(vt_ext, p_ref[slot, :tok_hi - tok_lo, head_sub * Q_TOK:(head_sub + 1) * Q_TOK],
                            preferred_element_type=F32)
                out_ref[sub, rows, :] = (o[:NA_HEAD_DIM] * (1.0 / o[NA_HEAD_DIM:NA_HEAD_DIM + 1])).astype(BF16)

        return scores, softmax_pv

    def body(step_variant):
        fns = [sub_group(sub, *plan) for sub, plan in enumerate(STEP_PLANS[step_variant])]
        stages = [(sub, pair) for sub in range(len(fns)) for pair in range(n_pairs)]
        maxima = fns[0][0](0, 0)
        for idx, (sub, pair) in enumerate(stages):
            nxt = None
            if idx + 1 < len(stages):
                nxt_sub, nxt_pair = stages[idx + 1]
                nxt = fns[nxt_sub][0](nxt_pair, (idx + 1) % 2)
            fns[sub][1](pair, idx % 2, maxima)
            maxima = nxt

    is_top = step == 0
    is_bot = step == steps - 1
    pl.when(is_top)(lambda: body(0))
    pl.when(jnp.logical_not(is_top | is_bot))(lambda: body(1))
    pl.when(is_bot)(lambda: body(2))


def _attention(qt, k, vt, t2, batch, seq):
    t = batch * seq
    blocks = seq // Q_TOK
    steps = blocks // STEP_SUBS
    assert steps >= 2 and blocks >= KV_BLOCKS

    def first_kv(u):
        return jnp.clip(STEP_SUBS * u - 1, 0, blocks - KV_BLOCKS)

    def blk(u, b):
        return (b * steps + u, 0, 0)

    def kv(i):
        return lambda u, b: (b * blocks + first_kv(u) + i, 0)

    def kv_blk(i):
        return lambda u, b: (b * blocks + first_kv(u) + i, 0, 0)

    return pl.pallas_call(
        functools.partial(_attention_kernel, steps=steps),
        grid=(steps, batch),
        in_specs=[pl.BlockSpec((STEP_SUBS, NA_WIDTH, Q_TOK), blk)]
                 + [pl.BlockSpec((Q_TOK, NA_WIDTH), kv(i)) for i in range(KV_BLOCKS)]
                 + [pl.BlockSpec((1, NA_WIDTH, Q_TOK), kv_blk(i)) for i in range(KV_BLOCKS)]
                 + [pl.BlockSpec(t2.shape, lambda u, b: (0, 0, 0, 0))],
        out_specs=pl.BlockSpec((STEP_SUBS, NA_WIDTH, Q_TOK), blk),
        out_shape=jax.ShapeDtypeStruct((t // Q_TOK, NA_WIDTH, Q_TOK), BF16),
        scratch_shapes=[pltpu.VMEM((2, KEY_ROWS * GRID_W, 2 * Q_TOK), F32),
                        pltpu.VMEM((2, KEY_ROWS * GRID_W, 2 * Q_TOK), BF16)],
        compiler_params=pltpu.CompilerParams(
            dimension_semantics=("arbitrary", "arbitrary"),
            vmem_limit_bytes=VMEM_LIMIT),
        name="attention",
    )(qt, *([k] * KV_BLOCKS), *([vt] * KV_BLOCKS), t2)


DFT_CHUNK = 16
LANES = 128


def _to_lane_tiles(scr, x):
    for c in range(scr.shape[0]):
        scr[c] = x[:, c * LANES:(c + 1) * LANES]


def _from_lane_tiles(scr):
    return jnp.concatenate([scr[c] for c in range(scr.shape[0])], axis=1)


def _strided_rows(scr, start, size, stride):
    return jnp.concatenate([scr[c, pl.ds(start, size, stride=stride), :] for c in range(scr.shape[0])], axis=1)


def _store_strided_rows(scr, start, stride, x):
    for c in range(scr.shape[0]):
        scr[c, pl.ds(start, x.shape[0], stride=stride), :] = x[:, c * LANES:(c + 1) * LANES]


def _fourier_1_kernel(w_ref, u_ref, a_ref, *scratch, major, chunks):
    if not scratch:
        for c in range(chunks):
            rows = slice(c * DFT_CHUNK, (c + 1) * DFT_CHUNK)
            u2d = u_ref[0, :, rows, :].reshape(major * DFT_CHUNK, F_WIDTH)
            res = jnp.dot(w_ref[...], u2d, preferred_element_type=F32)
            a_ref[0, :, :, rows, :] = res.astype(BF16).reshape(major, 2, DFT_CHUNK, F_WIDTH)
        return
    in_w, out_w = scratch
    pairs = DFT_CHUNK // 2
    _to_lane_tiles(in_w, pltpu.bitcast(u_ref[0].reshape(major * DFT_CHUNK, F_WIDTH), jnp.uint32))
    for j in range(pairs):
        rhs = pltpu.bitcast(_strided_rows(in_w, j, major, pairs), BF16)
        res = jnp.dot(w_ref[...], rhs, preferred_element_type=F32)
        _store_strided_rows(out_w, j, pairs, pltpu.bitcast(res.astype(BF16), jnp.uint32))
    a_ref[0] = pltpu.bitcast(_from_lane_tiles(out_w), BF16).reshape(major, 2, DFT_CHUNK, F_WIDTH)


def _fourier_1(u4, w1, kron):
    batch, major, minor, _ = u4.shape
    chunks = minor // DFT_CHUNK if kron else 1
    tb = chunks * DFT_CHUNK
    scratch = [] if kron else [pltpu.VMEM((F_WIDTH // LANES, major * DFT_CHUNK // 2, LANES), jnp.uint32),
                               pltpu.VMEM((F_WIDTH // LANES, major * DFT_CHUNK, LANES), jnp.uint32)]
    return pl.pallas_call(
        functools.partial(_fourier_1_kernel, major=major, chunks=chunks),
        grid=(batch, minor // tb),
        in_specs=[pl.BlockSpec(w1.shape, lambda b, j: (0, 0)),
                  pl.BlockSpec((1, major, tb, F_WIDTH), lambda b, j: (b, 0, j, 0))],
        out_specs=pl.BlockSpec((1, major, 2, tb, F_WIDTH), lambda b, j: (b, 0, 0, j, 0)),
        out_shape=jax.ShapeDtypeStruct((batch, major, 2, minor, F_WIDTH), BF16),
        scratch_shapes=scratch,
        compiler_params=pltpu.CompilerParams(
            dimension_semantics=("arbitrary", "arbitrary"),
            vmem_limit_bytes=VMEM_LIMIT),
        name="fourier_1",
    )(w1, u4)


def _fourier_2_kernel(a_ref, m_ref, wc_ref, bf_ref, y_ref, x_scr, y_scr):
    for j in range(DFT_CHUNK):
        rhs = a_ref[0, j].reshape(2 * DFT_MINOR, F_WIDTH)
        x = jnp.dot(m_ref[j], rhs, preferred_element_type=F32)
        rows = slice(j * DFT_MINOR, (j + 1) * DFT_MINOR)
        x_scr[rows, :F_WIDTH] = x[:DFT_MINOR].astype(BF16)
        x_scr[rows, F_WIDTH:] = x[DFT_MINOR:].astype(BF16)
    y = jnp.dot(x_scr[...], wc_ref[...], preferred_element_type=F32) + bf_ref[...]
    for j in range(DFT_CHUNK):
        _store_strided_rows(y_scr, j, DFT_CHUNK, y[j * DFT_MINOR:(j + 1) * DFT_MINOR])
    y_ref[0] = _from_lane_tiles(y_scr).astype(BF16).reshape(DFT_MINOR, DFT_CHUNK, F_WIDTH)


def _fold_channel_dft_kernel(cs_ref, wf_ref, wc_ref):
    wc_ref[...] = jnp.dot(cs_ref[...], wf_ref[...], preferred_element_type=F32,
                          precision=lax.Precision.HIGHEST).astype(BF16)


def _fold_channel_dft(w_f):
    c = np.arange(F_GROUP_DIM)
    ang = 2.0 * np.pi * np.outer(c, c) / F_GROUP_DIM
    eye = np.eye(F_GROUPS)
    cs = np.concatenate([np.kron(eye, np.cos(ang)), np.kron(eye, np.sin(ang))], axis=0)
    return pl.pallas_call(
        _fold_channel_dft_kernel,
        out_shape=jax.ShapeDtypeStruct((2 * F_WIDTH, F_WIDTH), BF16),
        name="fold_channel_dft",
    )(jnp.asarray(cs, F32), w_f.astype(F32))


def _fourier_2(a5, m_tab, w_c, b_f):
    batch, major, _, minor, _ = a5.shape
    tr = DFT_CHUNK
    return pl.pallas_call(
        _fourier_2_kernel,
        grid=(batch, major // tr),
        in_specs=[pl.BlockSpec((1, tr, 2, minor, F_WIDTH), lambda b, r: (b, r, 0, 0, 0)),
                  pl.BlockSpec((tr, 2 * minor, 2 * minor), lambda b, r: (r, 0, 0)),
                  pl.BlockSpec(w_c.shape, lambda b, r: (0, 0)),
                  pl.BlockSpec((1, F_WIDTH), lambda b, r: (0, 0))],
        out_specs=pl.BlockSpec((1, minor, tr, F_WIDTH), lambda b, r: (b, 0, r, 0)),
        out_shape=jax.ShapeDtypeStruct((batch, minor, major, F_WIDTH), BF16),
        scratch_shapes=[pltpu.VMEM((tr * minor, 2 * F_WIDTH), BF16),
                        pltpu.VMEM((F_WIDTH // LANES, minor * tr, LANES), F32)],
        compiler_params=pltpu.CompilerParams(
            dimension_semantics=("arbitrary", "arbitrary"),
            vmem_limit_bytes=VMEM_LIMIT),
        name="fourier_2",
    )(a5, m_tab, w_c, b_f)


@functools.lru_cache(maxsize=None)
def _dft_constants(seq):
    major = seq // DFT_MINOR
    kron = major * DFT_CHUNK <= 256
    a = np.arange(major)
    ang1 = 2.0 * np.pi * np.outer(a, a) / major
    w1 = np.stack([np.cos(ang1), -np.sin(ang1)], axis=1).reshape(2 * major, major)
    w1 = np.kron(w1, np.eye(DFT_CHUNK if kron else 2))
    r = np.arange(major)[:, None, None]
    p = np.arange(DFT_MINOR)[None, :, None]
    b = np.arange(DFT_MINOR)[None, None, :]
    ang2 = 2.0 * np.pi * ((b * (r + major * p)) % seq) / seq
    scale = 1.0 / np.sqrt(seq * F_GROUP_DIM)
    e_re, e_im = np.cos(ang2) * scale, -np.sin(ang2) * scale
    m_tab = np.concatenate([np.concatenate([e_re, -e_im], axis=2),
                            np.concatenate([e_im, e_re], axis=2)], axis=1)
    return kron, np.asarray(w1, np.float32), np.asarray(m_tab, np.float32)


def _fourier(u, w_c, b_f, batch, seq):
    major = seq // DFT_MINOR
    kron, w1, m_tab = _dft_constants(seq)
    a = _fourier_1(u.reshape(batch, major, DFT_MINOR, F_WIDTH), jnp.asarray(w1).astype(BF16), kron)
    y = _fourier_2(a, jnp.asarray(m_tab).astype(BF16), w_c, b_f)
    return y.reshape(batch * seq, F_WIDTH)


def _out_proj_kernel(ot_ref, zat_ref, yf_ref, zf_ref, x_ref, wa_ref, wf_ref,
                     gna_ref, gf_ref, gp_ref, o_ref):
    y_t = jnp.concatenate([ot_ref[i] for i in range(ot_ref.shape[0])], axis=1).astype(F32)
    inv = lax.rsqrt(jnp.mean(y_t * y_t, axis=0, keepdims=True) + RMS_EPS)
    mixed_at = _gate(y_t * inv * gna_ref[...], zat_ref[0])
    mixed_f = _gate(_rms(yf_ref[...].astype(F32), gf_ref[...]), zf_ref[...])
    out = lax.dot_general(mixed_at, wa_ref[...], (((0,), (0,)), ((), ())),
                          preferred_element_type=F32)
    out = out + jnp.dot(mixed_f, wf_ref[...], preferred_element_type=F32)
    o_ref[...] = x_ref[...] + _rms(out, gp_ref[...])


def _out_proj(o_t, za_t, y_f, z_f, x2d, w_a, w_fo, g_na_col, g_f, g_post, tm=TOKEN_BLOCK):
    t = x2d.shape[0]
    tok = lambda i: (i, 0)
    blk = lambda i: (i, 0, 0)
    const = lambda i: (0, 0)
    return pl.pallas_call(
        _out_proj_kernel,
        grid=(t // tm,),
        in_specs=[pl.BlockSpec((tm // Q_TOK, NA_WIDTH, Q_TOK), blk),
                  pl.BlockSpec((1, NA_WIDTH, tm), blk),
                  pl.BlockSpec((tm, F_WIDTH), tok),
                  pl.BlockSpec((tm, F_WIDTH), tok),
                  pl.BlockSpec((tm, D_MODEL), tok),
                  pl.BlockSpec(w_a.shape, const),
                  pl.BlockSpec(w_fo.shape, const),
                  pl.BlockSpec((NA_WIDTH, 1), const),
                  pl.BlockSpec((1, F_WIDTH), const),
                  pl.BlockSpec((1, D_MODEL), const)],
        out_specs=pl.BlockSpec((tm, D_MODEL), tok),
        out_shape=jax.ShapeDtypeStruct((t, D_MODEL), F32),
        compiler_params=pltpu.CompilerParams(
            dimension_semantics=("arbitrary",), vmem_limit_bytes=VMEM_LIMIT),
        name="out_proj",
    )(o_t, za_t, y_f, z_f, x2d, w_a, w_fo, g_na_col, g_f, g_post)


def _encoder_layer(x, w_nat, w_tr, bias, w_c, b_f, g_pre, g_na_col, g_f, w_oa, w_of, g_post):
    batch, seq, _ = x.shape
    x2d = x.reshape(batch * seq, D_MODEL)
    k, u_f, z_f, q_t, v_t, za_t = _in_proj(x2d, g_pre, w_nat, w_tr)
    o_t = _attention(q_t, k, v_t, bias, batch, seq)
    y_f = _fourier(u_f, w_c, b_f, batch, seq)
    out = _out_proj(o_t, za_t, y_f, z_f, x2d, w_oa, w_of, g_na_col, g_f, g_post)
    return out.reshape(batch, seq, D_MODEL)


def kernel(x_prompt, x_sample, w_in, rpb, w_fourier, b_fourier, g_pre, g_na, g_f, w_out, g_post):
    depth = w_in.shape[0]
    y_prompt, y_sample = x_prompt, x_sample
    scale = NA_HEAD_DIM ** -0.5
    for l in range(depth):
        w = w_in[l]
        w_q, w_k, w_v, w_za, w_uf, w_zf = (w[:, i * NA_WIDTH:(i + 1) * NA_WIDTH] for i in range(6))
        w_nat = jnp.concatenate([w_k, w_uf, w_zf], axis=1).astype(BF16)
        w_tr = jnp.concatenate([w_q * scale, w_v, w_za], axis=1).T.astype(BF16)
        bias = _bias_table(rpb[l])
        row = lambda v: v.reshape(1, -1).astype(F32)
        args = (w_nat, w_tr, bias, _fold_channel_dft(w_fourier[l]), row(b_fourier[l]),
                row(g_pre[l]), g_na[l].reshape(-1, 1).astype(F32), row(g_f[l]),
                w_out[l][:NA_WIDTH].astype(BF16), w_out[l][NA_WIDTH:].astype(BF16), row(g_post[l]))
        y_prompt = _encoder_layer(y_prompt, *args)
        y_sample = _encoder_layer(y_sample, *args)
    return (y_prompt, y_sample)
```

```python
import functools

import numpy as np
import jax
import jax.numpy as jnp
from jax import lax
from jax.experimental import pallas as pl
from jax.experimental.pallas import tpu as pltpu

D_MODEL = 1024
GRID_W = 64
WIN_ROWS = 8
WIN_COLS = 16
NA_HEADS = 8
NA_HEAD_DIM = 64
NA_WIDTH = NA_HEADS * NA_HEAD_DIM
F_GROUPS = 8
F_GROUP_DIM = 64
F_WIDTH = F_GROUPS * F_GROUP_DIM
RMS_EPS = 1e-6
NEG_INF = -1e30

Q_ROWS = 4
Q_TOK = Q_ROWS * GRID_W
STEP_SUBS = 2
KV_BLOCKS = 4
KEY_ROWS = 3 * Q_ROWS
ROW_SLOTS = 2 * WIN_ROWS
DFT_MINOR = 128
TOKEN_BLOCK = 1024
VMEM_LIMIT = 56 * 1024 * 1024

BF16 = jnp.bfloat16
F32 = jnp.float32


def _rms(x, g):
    inv = lax.rsqrt(jnp.mean(x * x, axis=-1, keepdims=True) + RMS_EPS)
    return x * inv * g


def _silu(z):
    return z * (1.0 / (1.0 + jnp.exp(-z)))


def _gate(y_normed, z):
    return y_normed.astype(BF16) * _silu(z)


def _in_proj_kernel(x_ref, g_ref, wn_ref, wt_ref,
                    k_ref, uf_ref, zf_ref, qt_ref, vt_ref, zat_ref):
    h = _rms(x_ref[...], g_ref[...]).astype(BF16)
    nat = jnp.dot(h, wn_ref[...], preferred_element_type=F32)
    k_ref[...] = nat[:, 0 * NA_WIDTH:1 * NA_WIDTH].astype(BF16)
    uf_ref[...] = nat[:, 1 * NA_WIDTH:2 * NA_WIDTH].astype(BF16)
    zf_ref[...] = nat[:, 2 * NA_WIDTH:3 * NA_WIDTH].astype(BF16)
    tr = lax.dot_general(wt_ref[...], h, (((1,), (1,)), ((), ())),
                         preferred_element_type=F32)
    for i in range(qt_ref.shape[0]):
        tok = slice(i * Q_TOK, (i + 1) * Q_TOK)
        qt_ref[i] = tr[0 * NA_WIDTH:1 * NA_WIDTH, tok].astype(BF16)
        vt_ref[i] = tr[1 * NA_WIDTH:2 * NA_WIDTH, tok].astype(BF16)
    zat_ref[0] = tr[2 * NA_WIDTH:3 * NA_WIDTH].astype(BF16)


def _in_proj(x2d, g_pre, w_nat, w_tr, tm=TOKEN_BLOCK):
    t = x2d.shape[0]
    tok = lambda i: (i, 0)
    blk = lambda i: (i, 0, 0)
    const = lambda i: (0, 0)
    nat_shape = jax.ShapeDtypeStruct((t, NA_WIDTH), BF16)
    qv_shape = jax.ShapeDtypeStruct((t // Q_TOK, NA_WIDTH, Q_TOK), BF16)
    za_shape = jax.ShapeDtypeStruct((t // tm, NA_WIDTH, tm), BF16)
    return pl.pallas_call(
        _in_proj_kernel,
        grid=(t // tm,),
        in_specs=[pl.BlockSpec((tm, D_MODEL), tok),
                  pl.BlockSpec((1, D_MODEL), const),
                  pl.BlockSpec(w_nat.shape, const),
                  pl.BlockSpec(w_tr.shape, const)],
        out_specs=[pl.BlockSpec((tm, NA_WIDTH), tok)] * 3
                  + [pl.BlockSpec((tm // Q_TOK, NA_WIDTH, Q_TOK), blk)] * 2
                  + [pl.BlockSpec((1, NA_WIDTH, tm), blk)],
        out_shape=[nat_shape] * 3 + [qv_shape] * 2 + [za_shape],
        compiler_params=pltpu.CompilerParams(
            dimension_semantics=("arbitrary",), vmem_limit_bytes=VMEM_LIMIT),
        name="in_proj",
    )(x2d, g_pre, w_nat, w_tr)


def _bias_table_kernel(rpb_ref, t2_ref):
    head = pl.program_id(0)
    shape = (GRID_W, 2 * GRID_W)
    kc = lax.broadcasted_iota(jnp.int32, shape, 0)
    lane = lax.broadcasted_iota(jnp.int32, shape, 1)
    second = lane >= GRID_W
    qc = jnp.where(second, lane - GRID_W, lane)
    col_off = kc - qc + (WIN_COLS - 1)
    win_start = jnp.clip(qc - WIN_COLS // 2, 0, GRID_W - WIN_COLS)
    col_valid = (kc >= win_start) & (kc < win_start + WIN_COLS)
    n_off = 2 * WIN_COLS - 1
    n_row = 2 * WIN_ROWS - 1

    def entry(d, o):
        if 0 <= d < n_row:
            return rpb_ref[(head * n_row + d) * n_off + o]
        return jnp.float32(NEG_INF)

    for d in range(ROW_SLOTS):
        tile = jnp.full(shape, NEG_INF, F32)
        for o in range(n_off):
            val = jnp.where(second, entry(d - 1, o), entry(d, o))
            tile = jnp.where(col_off == o, val, tile)
        t2_ref[0, d] = jnp.where(col_valid, tile, NEG_INF)


def _bias_table(rpb):
    return pl.pallas_call(
        _bias_table_kernel,
        grid=(NA_HEADS,),
        in_specs=[pl.BlockSpec(memory_space=pltpu.SMEM)],
        out_specs=pl.BlockSpec((1, ROW_SLOTS, GRID_W, 2 * GRID_W), lambda h: (h, 0, 0, 0)),
        out_shape=jax.ShapeDtypeStruct((NA_HEADS, ROW_SLOTS, GRID_W, 2 * GRID_W), F32),
        compiler_params=pltpu.CompilerParams(dimension_semantics=("arbitrary",)),
        name="bias_table",
    )(rpb.astype(F32).reshape(-1))


def _token_range(refs, lo, hi, other, axis):
    pieces = []
    for idx, ref in enumerate(refs):
        a, b = max(lo, idx * Q_TOK), min(hi, (idx + 1) * Q_TOK)
        if a < b:
            tok = slice(a - idx * Q_TOK, b - idx * Q_TOK)
            pieces.append(ref[tok, other] if axis == 0 else ref[0, other, tok])
    return pieces[0] if len(pieces) == 1 else jnp.concatenate(pieces, axis=axis)


def _window_plan(variant, jp):
    if variant == 0:
        return [(i, i - 2 * jp + WIN_ROWS - 1, None) for i in range(WIN_ROWS)]
    if variant == 2:
        q_row = KEY_ROWS - Q_ROWS + 2 * jp
        return [(i, i - q_row + WIN_ROWS - 1, None) for i in range(KEY_ROWS - WIN_ROWS, KEY_ROWS)]
    q_row = Q_ROWS + 2 * jp
    plan = []
    for i in range(q_row - WIN_ROWS // 2, q_row + WIN_ROWS // 2 + 1):
        half = "low" if i == q_row - WIN_ROWS // 2 else "high" if i == q_row + WIN_ROWS // 2 else None
        plan.append((i, i - q_row + WIN_ROWS - 1, half))
    return plan


STEP_PLANS = (((0, 0), (1, 0)),
              ((1, 0), (1, Q_ROWS)),
              ((1, Q_ROWS), (2, Q_ROWS)))


def _attention_kernel(qt_ref, *refs, steps):
    k_refs, v_refs = refs[:KV_BLOCKS], refs[KV_BLOCKS:2 * KV_BLOCKS]
    t2_ref, out_ref, sb_ref, p_ref = refs[2 * KV_BLOCKS:]
    step = pl.program_id(0)
    pair_lanes = 2 * NA_HEAD_DIM
    q_lanes = 2 * GRID_W
    n_pairs = NA_HEADS // 2

    def sub_group(sub, variant, key_off):
        plans = [_window_plan(variant, jp) for jp in range(Q_ROWS // 2)]
        key_lo = min(p[0][0] for p in plans)
        key_top = max(p[-1][0] for p in plans) + 1
        key_hi = key_top + (key_top - key_lo) % 2
        tok_lo, tok_top, tok_hi = ((key_off + r) * GRID_W for r in (key_lo, key_top, key_hi))
        lane = lax.broadcasted_iota(jnp.int32, (1, q_lanes), 1)
        half_mask = {"low": jnp.where(lane < GRID_W, 0.0, NEG_INF).astype(F32),
                     "high": jnp.where(lane >= GRID_W, 0.0, NEG_INF).astype(F32)}
        ones_rows = jnp.ones((16, tok_hi - tok_lo), BF16)
        zero_blk = jnp.zeros((GRID_W, q_lanes), BF16)

        def scores(pair, slot):
            cols = slice(pair * pair_lanes, (pair + 1) * pair_lanes)
            k_pair = _token_range(k_refs, tok_lo, tok_top, cols, 0)
            qt_pair = qt_ref[sub, cols, :]
            zero = jnp.zeros((NA_HEAD_DIM, Q_TOK), BF16)
            qt_both = jnp.concatenate(
                [jnp.concatenate([qt_pair[:NA_HEAD_DIM], zero], axis=0),
                 jnp.concatenate([zero, qt_pair[NA_HEAD_DIM:]], axis=0)], axis=1)
            s = jnp.dot(k_pair, qt_both, preferred_element_type=F32)
            maxima = []
            for head_sub in range(2):
                for jp, plan in enumerate(plans):
                    first = head_sub * Q_TOK + jp * q_lanes
                    lanes = slice(first, first + q_lanes)
                    m_acc = None
                    for i, slot_d, half in plan:
                        rows = slice((i - key_lo) * GRID_W, (i - key_lo + 1) * GRID_W)
                        blk = s[rows, lanes] + t2_ref[2 * pair + head_sub, slot_d]
                        if half is not None:
                            blk = blk + half_mask[half]
                        sb_ref[slot, rows, lanes] = blk
                        m_acc = blk if m_acc is None else jnp.maximum(m_acc, blk)
                    maxima.append(jnp.max(m_acc, axis=0, keepdims=True))
            return maxima

        def softmax_pv(pair, slot, maxima):
            for head_sub in range(2):
                head = 2 * pair + head_sub
                for jp, plan in enumerate(plans):
                    first = head_sub * Q_TOK + jp * q_lanes
                    lanes = slice(first, first + q_lanes)
                    m = maxima[2 * head_sub + jp]
                    live = {i for i, _, _ in plan}
                    for i in range(key_lo, key_hi):
                        rows = slice((i - key_lo) * GRID_W, (i - key_lo + 1) * GRID_W)
                        if i in live:
                            p_ref[slot, rows, lanes] = jnp.exp(sb_ref[slot, rows, lanes] - m).astype(BF16)
                        else:
                            p_ref[slot, rows, lanes] = zero_blk
                rows = slice(head * NA_HEAD_DIM, (head + 1) * NA_HEAD_DIM)
                vt_head = _token_range(v_refs, tok_lo, tok_hi, rows, 1)
                vt_ext = jnp.concatenate([vt_head, ones_rows], axis=0)
                o = jnp.dot(vt_ext, p_ref[slot, :tok_hi - tok_lo, head_sub * Q_TOK:(head_sub + 1) * Q_TOK],
                            preferred_element_type=F32)
                out_ref[sub, rows, :] = (o[:NA_HEAD_DIM] * (1.0 / o[NA_HEAD_DIM:NA_HEAD_DIM + 1])).astype(BF16)

        return scores, softmax_pv

    def body(step_variant):
        fns = [sub_group(sub, *plan) for sub, plan in enumerate(STEP_PLANS[step_variant])]
        stages = [(sub, pair) for sub in range(len(fns)) for pair in range(n_pairs)]
        maxima = fns[0][0](0, 0)
        for idx, (sub, pair) in enumerate(stages):
            nxt = None
            if idx + 1 < len(stages):
                nxt_sub, nxt_pair = stages[idx + 1]
                nxt = fns[nxt_sub][0](nxt_pair, (idx + 1) % 2)
            fns[sub][1](pair, idx % 2, maxima)
            maxima = nxt

    is_top = step == 0
    is_bot = step == steps - 1
    pl.when(is_top)(lambda: body(0))
    pl.when(jnp.logical_not(is_top | is_bot))(lambda: body(1))
    pl.when(is_bot)(lambda: body(2))


def _attention(qt, k, vt, t2, batch, seq):
    t = batch * seq
    blocks = seq // Q_TOK
    steps = blocks // STEP_SUBS
    assert steps >= 2 and blocks >= KV_BLOCKS

    def first_kv(u):
        return jnp.clip(STEP_SUBS * u - 1, 0, blocks - KV_BLOCKS)

    def blk(u, b):
        return (b * steps + u, 0, 0)

    def kv(i):
        return lambda u, b: (b * blocks + first_kv(u) + i, 0)

    def kv_blk(i):
        return lambda u, b: (b * blocks + first_kv(u) + i, 0, 0)

    return pl.pallas_call(
        functools.partial(_attention_kernel, steps=steps),
        grid=(steps, batch),
        in_specs=[pl.BlockSpec((STEP_SUBS, NA_WIDTH, Q_TOK), blk)]
                 + [pl.BlockSpec((Q_TOK, NA_WIDTH), kv(i)) for i in range(KV_BLOCKS)]
                 + [pl.BlockSpec((1, NA_WIDTH, Q_TOK), kv_blk(i)) for i in range(KV_BLOCKS)]
                 + [pl.BlockSpec(t2.shape, lambda u, b: (0, 0, 0, 0))],
        out_specs=pl.BlockSpec((STEP_SUBS, NA_WIDTH, Q_TOK), blk),
        out_shape=jax.ShapeDtypeStruct((t // Q_TOK, NA_WIDTH, Q_TOK), BF16),
        scratch_shapes=[pltpu.VMEM((2, KEY_ROWS * GRID_W, 2 * Q_TOK), F32),
                        pltpu.VMEM((2, KEY_ROWS * GRID_W, 2 * Q_TOK), BF16)],
        compiler_params=pltpu.CompilerParams(
            dimension_semantics=("arbitrary", "arbitrary"),
            vmem_limit_bytes=VMEM_LIMIT),
        name="attention",
    )(qt, *([k] * KV_BLOCKS), *([vt] * KV_BLOCKS), t2)


DFT_CHUNK = 16
LANES = 128


def _to_lane_tiles(scr, x):
    for c in range(scr.shape[0]):
        scr[c] = x[:, c * LANES:(c + 1) * LANES]


def _from_lane_tiles(scr):
    return jnp.concatenate([scr[c] for c in range(scr.shape[0])], axis=1)


def _strided_rows(scr, start, size, stride):
    return jnp.concatenate([scr[c, pl.ds(start, size, stride=stride), :] for c in range(scr.shape[0])], axis=1)


def _store_strided_rows(scr, start, stride, x):
    for c in range(scr.shape[0]):
        scr[c, pl.ds(start, x.shape[0], stride=stride), :] = x[:, c * LANES:(c + 1) * LANES]


def _fourier_1_kernel(w_ref, u_ref, a_ref, *scratch, major, chunks):
    if not scratch:
        for c in range(chunks):
            rows = slice(c * DFT_CHUNK, (c + 1) * DFT_CHUNK)
            u2d = u_ref[0, :, rows, :].reshape(major * DFT_CHUNK, F_WIDTH)
            res = jnp.dot(w_ref[...], u2d, preferred_element_type=F32)
            a_ref[0, :, :, rows, :] = res.astype(BF16).reshape(major, 2, DFT_CHUNK, F_WIDTH)
        return
    in_w, out_w = scratch
    pairs = DFT_CHUNK // 2
    _to_lane_tiles(in_w, pltpu.bitcast(u_ref[0].reshape(major * DFT_CHUNK, F_WIDTH), jnp.uint32))
    for j in range(pairs):
        rhs = pltpu.bitcast(_strided_rows(in_w, j, major, pairs), BF16)
        res = jnp.dot(w_ref[...], rhs, preferred_element_type=F32)
        _store_strided_rows(out_w, j, pairs, pltpu.bitcast(res.astype(BF16), jnp.uint32))
    a_ref[0] = pltpu.bitcast(_from_lane_tiles(out_w), BF16).reshape(major, 2, DFT_CHUNK, F_WIDTH)


def _fourier_1(u4, w1, kron):
    batch, major, minor, _ = u4.shape
    chunks = minor // DFT_CHUNK if kron else 1
    tb = chunks * DFT_CHUNK
    scratch = [] if kron else [pltpu.VMEM((F_WIDTH // LANES, major * DFT_CHUNK // 2, LANES), jnp.uint32),
                               pltpu.VMEM((F_WIDTH // LANES, major * DFT_CHUNK, LANES), jnp.uint32)]
    return pl.pallas_call(
        functools.partial(_fourier_1_kernel, major=major, chunks=chunks),
        grid=(batch, minor // tb),
        in_specs=[pl.BlockSpec(w1.shape, lambda b, j: (0, 0)),
                  pl.BlockSpec((1, major, tb, F_WIDTH), lambda b, j: (b, 0, j, 0))],
        out_specs=pl.BlockSpec((1, major, 2, tb, F_WIDTH), lambda b, j: (b, 0, 0, j, 0)),
        out_shape=jax.ShapeDtypeStruct((batch, major, 2, minor, F_WIDTH), BF16),
        scratch_shapes=scratch,
        compiler_params=pltpu.CompilerParams(
            dimension_semantics=("arbitrary", "arbitrary"),
            vmem_limit_bytes=VMEM_LIMIT),
        name="fourier_1",
    )(w1, u4)


def _fourier_2_kernel(a_ref, m_ref, wc_ref, bf_ref, y_ref, x_scr, y_scr):
    for j in range(DFT_CHUNK):
        rhs = a_ref[0, j].reshape(2 * DFT_MINOR, F_WIDTH)
        x = jnp.dot(m_ref[j], rhs, preferred_element_type=F32)
        rows = slice(j * DFT_MINOR, (j + 1) * DFT_MINOR)
        x_scr[rows, :F_WIDTH] = x[:DFT_MINOR].astype(BF16)
        x_scr[rows, F_WIDTH:] = x[DFT_MINOR:].astype(BF16)
    y = jnp.dot(x_scr[...], wc_ref[...], preferred_element_type=F32) + bf_ref[...]
    for j in range(DFT_CHUNK):
        _store_strided_rows(y_scr, j, DFT_CHUNK, y[j * DFT_MINOR:(j + 1) * DFT_MINOR])
    y_ref[0] = _from_lane_tiles(y_scr).astype(BF16).reshape(DFT_MINOR, DFT_CHUNK, F_WIDTH)


def _fold_channel_dft_kernel(cs_ref, wf_ref, wc_ref):
    wc_ref[...] = jnp.dot(cs_ref[...], wf_ref[...], preferred_element_type=F32,
                          precision=lax.Precision.HIGHEST).astype(BF16)


def _fold_channel_dft(w_f):
    c = np.arange(F_GROUP_DIM)
    ang = 2.0 * np.pi * np.outer(c, c) / F_GROUP_DIM
    eye = np.eye(F_GROUPS)
    cs = np.concatenate([np.kron(eye, np.cos(ang)), np.kron(eye, np.sin(ang))], axis=0)
    return pl.pallas_call(
        _fold_channel_dft_kernel,
        out_shape=jax.ShapeDtypeStruct((2 * F_WIDTH, F_WIDTH), BF16),
        name="fold_channel_dft",
    )(jnp.asarray(cs, F32), w_f.astype(F32))


def _fourier_2(a5, m_tab, w_c, b_f):
    batch, major, _, minor, _ = a5.shape
    tr = DFT_CHUNK
    return pl.pallas_call(
        _fourier_2_kernel,
        grid=(batch, major // tr),
        in_specs=[pl.BlockSpec((1, tr, 2, minor, F_WIDTH), lambda b, r: (b, r, 0, 0, 0)),
                  pl.BlockSpec((tr, 2 * minor, 2 * minor), lambda b, r: (r, 0, 0)),
                  pl.BlockSpec(w_c.shape, lambda b, r: (0, 0)),
                  pl.BlockSpec((1, F_WIDTH), lambda b, r: (0, 0))],
        out_specs=pl.BlockSpec((1, minor, tr, F_WIDTH), lambda b, r: (b, 0, r, 0)),
        out_shape=jax.ShapeDtypeStruct((batch, minor, major, F_WIDTH), BF16),
        scratch_shapes=[pltpu.VMEM((tr * minor, 2 * F_WIDTH), BF16),
                        pltpu.VMEM((F_WIDTH // LANES, minor * tr, LANES), F32)],
        compiler_params=pltpu.CompilerParams(
            dimension_semantics=("arbitrary", "arbitrary"),
            vmem_limit_bytes=VMEM_LIMIT),
        name="fourier_2",
    )(a5, m_tab, w_c, b_f)


@functools.lru_cache(maxsize=None)
def _dft_constants(seq):
    major = seq // DFT_MINOR
    kron = major * DFT_CHUNK <= 256
    a = np.arange(major)
    ang1 = 2.0 * np.pi * np.outer(a, a) / major
    w1 = np.stack([np.cos(ang1), -np.sin(ang1)], axis=1).reshape(2 * major, major)
    w1 = np.kron(w1, np.eye(DFT_CHUNK if kron else 2))
    r = np.arange(major)[:, None, None]
    p = np.arange(DFT_MINOR)[None, :, None]
    b = np.arange(DFT_MINOR)[None, None, :]
    ang2 = 2.0 * np.pi * ((b * (r + major * p)) % seq) / seq
    scale = 1.0 / np.sqrt(seq * F_GROUP_DIM)
    e_re, e_im = np.cos(ang2) * scale, -np.sin(ang2) * scale
    m_tab = np.concatenate([np.concatenate([e_re, -e_im], axis=2),
                            np.concatenate([e_im, e_re], axis=2)], axis=1)
    return kron, np.asarray(w1, np.float32), np.asarray(m_tab, np.float32)


def _fourier(u, w_c, b_f, batch, seq):
    major = seq // DFT_MINOR
    kron, w1, m_tab = _dft_constants(seq)
    a = _fourier_1(u.reshape(batch, major, DFT_MINOR, F_WIDTH), jnp.asarray(w1).astype(BF16), kron)
    y = _fourier_2(a, jnp.asarray(m_tab).astype(BF16), w_c, b_f)
    return y.reshape(batch * seq, F_WIDTH)


def _out_proj_kernel(ot_ref, zat_ref, yf_ref, zf_ref, x_ref, wa_ref, wf_ref,
                     gna_ref, gf_ref, gp_ref, o_ref):
    y_t = jnp.concatenate([ot_ref[i] for i in range(ot_ref.shape[0])], axis=1).astype(F32)
    inv = lax.rsqrt(jnp.mean(y_t * y_t, axis=0, keepdims=True) + RMS_EPS)
    mixed_at = _gate(y_t * inv * gna_ref[...], zat_ref[0])
    mixed_f = _gate(_rms(yf_ref[...].astype(F32), gf_ref[...]), zf_ref[...])
    out = lax.dot_general(mixed_at, wa_ref[...], (((0,), (0,)), ((), ())),
                          preferred_element_type=F32)
    out = out + jnp.dot(mixed_f, wf_ref[...], preferred_element_type=F32)
    o_ref[...] = x_ref[...] + _rms(out, gp_ref[...])


def _out_proj(o_t, za_t, y_f, z_f, x2d, w_a, w_fo, g_na_col, g_f, g_post, tm=TOKEN_BLOCK):
    t = x2d.shape[0]
    tok = lambda i: (i, 0)
    blk = lambda i: (i, 0, 0)
    const = lambda i: (0, 0)
    return pl.pallas_call(
        _out_proj_kernel,
        grid=(t // tm,),
        in_specs=[pl.BlockSpec((tm // Q_TOK, NA_WIDTH, Q_TOK), blk),
                  pl.BlockSpec((1, NA_WIDTH, tm), blk),
                  pl.BlockSpec((tm, F_WIDTH), tok),
                  pl.BlockSpec((tm, F_WIDTH), tok),
                  pl.BlockSpec((tm, D_MODEL), tok),
                  pl.BlockSpec(w_a.shape, const),
                  pl.BlockSpec(w_fo.shape, const),
                  pl.BlockSpec((NA_WIDTH, 1), const),
                  pl.BlockSpec((1, F_WIDTH), const),
                  pl.BlockSpec((1, D_MODEL), const)],
        out_specs=pl.BlockSpec((tm, D_MODEL), tok),
        out_shape=jax.ShapeDtypeStruct((t, D_MODEL), F32),
        compiler_params=pltpu.CompilerParams(
            dimension_semantics=("arbitrary",), vmem_limit_bytes=VMEM_LIMIT),
        name="out_proj",
    )(o_t, za_t, y_f, z_f, x2d, w_a, w_fo, g_na_col, g_f, g_post)


def _encoder_layer(x, w_nat, w_tr, bias, w_c, b_f, g_pre, g_na_col, g_f, w_oa, w_of, g_post):
    batch, seq, _ = x.shape
    x2d = x.reshape(batch * seq, D_MODEL)
    k, u_f, z_f, q_t, v_t, za_t = _in_proj(x2d, g_pre, w_nat, w_tr)
    o_t = _attention(q_t, k, v_t, bias, batch, seq)
    y_f = _fourier(u_f, w_c, b_f, batch, seq)
    out = _out_proj(o_t, za_t, y_f, z_f, x2d, w_oa, w_of, g_na_col, g_f, g_post)
    return out.reshape(batch, seq, D_MODEL)


def kernel(x_prompt, x_sample, w_in, rpb, w_fourier, b_fourier, g_pre, g_na, g_f, w_out, g_post):
    depth = w_in.shape[0]
    y_prompt, y_sample = x_prompt, x_sample
    scale = NA_HEAD_DIM ** -0.5
    for l in range(depth):
        w = w_in[l]
        w_q, w_k, w_v, w_za, w_uf, w_zf = (w[:, i * NA_WIDTH:(i + 1) * NA_WIDTH] for i in range(6))
        w_nat = jnp.concatenate([w_k, w_uf, w_zf], axis=1).astype(BF16)
        w_tr = jnp.concatenate([w_q * scale, w_v, w_za], axis=1).T.astype(BF16)
        bias = _bias_table(rpb[l])
        row = lambda v: v.reshape(1, -1).astype(F32)
        args = (w_nat, w_tr, bias, _fold_channel_dft(w_fourier[l]), row(b_fourier[l]),
                row(g_pre[l]), g_na[l].reshape(-1, 1).astype(F32), row(g_f[l]),
                w_out[l][:NA_WIDTH].astype(BF16), w_out[l][NA_WIDTH:].astype(BF16), row(g_post[l]))
        y_prompt = _encoder_layer(y_prompt, *args)
        y_sample = _encoder_layer(y_sample, *args)
    return (y_prompt, y_sample)
```

```python
import functools

import numpy as np
import jax
import jax.numpy as jnp
from jax import lax
from jax.experimental import pallas as pl
from jax.experimental.pallas import tpu as pltpu

D_MODEL = 1024
GRID_W = 64
WIN_ROWS = 8
WIN_COLS = 16
NA_HEADS = 8
NA_HEAD_DIM = 64
NA_WIDTH = NA_HEADS * NA_HEAD_DIM
F_GROUPS = 8
F_GROUP_DIM = 64
F_WIDTH = F_GROUPS * F_GROUP_DIM
RMS_EPS = 1e-6
NEG_INF = -1e30

Q_ROWS = 4
Q_TOK = Q_ROWS * GRID_W
STEP_SUBS = 2
KV_BLOCKS = 4
KEY_ROWS = 3 * Q_ROWS
ROW_SLOTS = 2 * WIN_ROWS
DFT_MINOR = 128
TOKEN_BLOCK = 1024
VMEM_LIMIT = 56 * 1024 * 1024

BF16 = jnp.bfloat16
F32 = jnp.float32


def _rms(x, g):
    inv = lax.rsqrt(jnp.mean(x * x, axis=-1, keepdims=True) + RMS_EPS)
    return x * inv * g


def _silu(z):
    return z * (1.0 / (1.0 + jnp.exp(-z)))


def _gate(y_normed, z):
    return y_normed.astype(BF16) * _silu(z)


def _in_proj_kernel(x_ref, g_ref, wn_ref, wt_ref,
                    k_ref, uf_ref, zf_ref, qt_ref, vt_ref, zat_ref):
    h = _rms(x_ref[...], g_ref[...]).astype(BF16)
    nat = jnp.dot(h, wn_ref[...], preferred_element_type=F32)
    k_ref[...] = nat[:, 0 * NA_WIDTH:1 * NA_WIDTH].astype(BF16)
    uf_ref[...] = nat[:, 1 * NA_WIDTH:2 * NA_WIDTH].astype(BF16)
    zf_ref[...] = nat[:, 2 * NA_WIDTH:3 * NA_WIDTH].astype(BF16)
    tr = lax.dot_general(wt_ref[...], h, (((1,), (1,)), ((), ())),
                         preferred_element_type=F32)
    for i in range(qt_ref.shape[0]):
        tok = slice(i * Q_TOK, (i + 1) * Q_TOK)
        qt_ref[i] = tr[0 * NA_WIDTH:1 * NA_WIDTH, tok].astype(BF16)
        vt_ref[i] = tr[1 * NA_WIDTH:2 * NA_WIDTH, tok].astype(BF16)
    zat_ref[0] = tr[2 * NA_WIDTH:3 * NA_WIDTH].astype(BF16)


def _in_proj(x2d, g_pre, w_nat, w_tr, tm=TOKEN_BLOCK):
    t = x2d.shape[0]
    tok = lambda i: (i, 0)
    blk = lambda i: (i, 0, 0)
    const = lambda i: (0, 0)
    nat_shape = jax.ShapeDtypeStruct((t, NA_WIDTH), BF16)
    qv_shape = jax.ShapeDtypeStruct((t // Q_TOK, NA_WIDTH, Q_TOK), BF16)
    za_shape = jax.ShapeDtypeStruct((t // tm, NA_WIDTH, tm), BF16)
    return pl.pallas_call(
        _in_proj_kernel,
        grid=(t // tm,),
        in_specs=[pl.BlockSpec((tm, D_MODEL), tok),
                  pl.BlockSpec((1, D_MODEL), const),
                  pl.BlockSpec(w_nat.shape, const),
                  pl.BlockSpec(w_tr.shape, const)],
        out_specs=[pl.BlockSpec((tm, NA_WIDTH), tok)] * 3
                  + [pl.BlockSpec((tm // Q_TOK, NA_WIDTH, Q_TOK), blk)] * 2
                  + [pl.BlockSpec((1, NA_WIDTH, tm), blk)],
        out_shape=[nat_shape] * 3 + [qv_shape] * 2 + [za_shape],
        compiler_params=pltpu.CompilerParams(
            dimension_semantics=("arbitrary",), vmem_limit_bytes=VMEM_LIMIT),
        name="in_proj",
    )(x2d, g_pre, w_nat, w_tr)


def _bias_table_kernel(rpb_ref, t2_ref):
    head = pl.program_id(0)
    shape = (GRID_W, 2 * GRID_W)
    kc = lax.broadcasted_iota(jnp.int32, shape, 0)
    lane = lax.broadcasted_iota(jnp.int32, shape, 1)
    second = lane >= GRID_W
    qc = jnp.where(second, lane - GRID_W, lane)
    col_off = kc - qc + (WIN_COLS - 1)
    win_start = jnp.clip(qc - WIN_COLS // 2, 0, GRID_W - WIN_COLS)
    col_valid = (kc >= win_start) & (kc < win_start + WIN_COLS)
    n_off = 2 * WIN_COLS - 1
    n_row = 2 * WIN_ROWS - 1

    def entry(d, o):
        if 0 <= d < n_row:
            return rpb_ref[(head * n_row + d) * n_off + o]
        return jnp.float32(NEG_INF)

    for d in range(ROW_SLOTS):
        tile = jnp.full(shape, NEG_INF, F32)
        for o in range(n_off):
            val = jnp.where(second, entry(d - 1, o), entry(d, o))
            tile = jnp.where(col_off == o, val, tile)
        t2_ref[0, d] = jnp.where(col_valid, tile, NEG_INF)


def _bias_table(rpb):
    return pl.pallas_call(
        _bias_table_kernel,
        grid=(NA_HEADS,),
        in_specs=[pl.BlockSpec(memory_space=pltpu.SMEM)],
        out_specs=pl.BlockSpec((1, ROW_SLOTS, GRID_W, 2 * GRID_W), lambda h: (h, 0, 0, 0)),
        out_shape=jax.ShapeDtypeStruct((NA_HEADS, ROW_SLOTS, GRID_W, 2 * GRID_W), F32),
        compiler_params=pltpu.CompilerParams(dimension_semantics=("arbitrary",)),
        name="bias_table",
    )(rpb.astype(F32).reshape(-1))


def _token_range(refs, lo, hi, other, axis):
    pieces = []
    for idx, ref in enumerate(refs):
        a, b = max(lo, idx * Q_TOK), min(hi, (idx + 1) * Q_TOK)
        if a < b:
            tok = slice(a - idx * Q_TOK, b - idx * Q_TOK)
            pieces.append(ref[tok, other] if axis == 0 else ref[0, other, tok])
    return pieces[0] if len(pieces) == 1 else jnp.concatenate(pieces, axis=axis)


def _window_plan(variant, jp):
    if variant == 0:
        return [(i, i - 2 * jp + WIN_ROWS - 1, None) for i in range(WIN_ROWS)]
    if variant == 2:
        q_row = KEY_ROWS - Q_ROWS + 2 * jp
        return [(i, i - q_row + WIN_ROWS - 1, None) for i in range(KEY_ROWS - WIN_ROWS, KEY_ROWS)]
    q_row = Q_ROWS + 2 * jp
    plan = []
    for i in range(q_row - WIN_ROWS // 2, q_row + WIN_ROWS // 2 + 1):
        half = "low" if i == q_row - WIN_ROWS // 2 else "high" if i == q_row + WIN_ROWS // 2 else None
        plan.append((i, i - q_row + WIN_ROWS - 1, half))
    return plan


STEP_PLANS = (((0, 0), (1, 0)),
              ((1, 0), (1, Q_ROWS)),
              ((1, Q_ROWS), (2, Q_ROWS)))


def _attention_kernel(qt_ref, *refs, steps):
    k_refs, v_refs = refs[:KV_BLOCKS], refs[KV_BLOCKS:2 * KV_BLOCKS]
    t2_ref, out_ref = refs[2 * KV_BLOCKS:]
    step = pl.program_id(0)
    pair_lanes = 2 * NA_HEAD_DIM
    q_lanes = 2 * GRID_W
    n_pairs = NA_HEADS // 2

    def sub_group(sub, variant, key_off):
        plans = [_window_plan(variant, jp) for jp in range(Q_ROWS // 2)]
        key_lo = min(p[0][0] for p in plans)
        key_top = max(p[-1][0] for p in plans) + 1
        key_hi = key_top + (key_top - key_lo) % 2
        tok_lo, tok_top, tok_hi = ((key_off + r) * GRID_W for r in (key_lo, key_top, key_hi))
        lane = lax.broadcasted_iota(jnp.int32, (1, q_lanes), 1)
        half_mask = {"low": jnp.where(lane < GRID_W, 0.0, NEG_INF).astype(F32),
                     "high": jnp.where(lane >= GRID_W, 0.0, NEG_INF).astype(F32)}
        ones_rows = jnp.ones((16, tok_hi - tok_lo), BF16)
        zero_blk = jnp.zeros((GRID_W, q_lanes), BF16)

        def scores(pair):
            cols = slice(pair * pair_lanes, (pair + 1) * pair_lanes)
            k_pair = _token_range(k_refs, tok_lo, tok_top, cols, 0)
            qt_pair = qt_ref[sub, cols, :]
            zero = jnp.zeros((NA_HEAD_DIM, Q_TOK), BF16)
            qt_both = jnp.concatenate(
                [jnp.concatenate([qt_pair[:NA_HEAD_DIM], zero], axis=0),
                 jnp.concatenate([zero, qt_pair[NA_HEAD_DIM:]], axis=0)], axis=1)
            s = jnp.dot(k_pair, qt_both, preferred_element_type=F32)
            tiles = [(head_sub, jp) for head_sub in range(2) for jp in range(len(plans))]
            m_acc = [None] * len(tiles)
            biased = [{} for _ in tiles]
            for i in range(key_lo, key_top):
                rows = slice((i - key_lo) * GRID_W, (i - key_lo + 1) * GRID_W)
                for t, (head_sub, jp) in enumerate(tiles):
                    entry = [e for e in plans[jp] if e[0] == i]
                    if not entry:
                        continue
                    _, slot_d, half = entry[0]
                    first = head_sub * Q_TOK + jp * q_lanes
                    lanes = slice(first, first + q_lanes)
                    blk = s[rows, lanes] + t2_ref[2 * pair + head_sub, slot_d]
                    if half is not None:
                        blk = blk + half_mask[half]
                    biased[t][i] = blk
                    blk_max = jnp.max(blk.reshape(GRID_W // 8, 8, q_lanes), axis=0)
                    m_acc[t] = blk_max if m_acc[t] is None else jnp.maximum(m_acc[t], blk_max)
            maxima = [jnp.max(m, axis=0, keepdims=True) for m in m_acc]
            return maxima, biased

        def softmax_pv(pair, state):
            maxima, biased = state
            for head_sub in range(2):
                head = 2 * pair + head_sub
                cols = []
                for jp in range(len(plans)):
                    t = 2 * head_sub + jp
                    blocks = [jnp.exp(biased[t][i] - maxima[t]).astype(BF16) if i in biased[t] else zero_blk
                              for i in range(key_lo, key_hi)]
                    cols.append(jnp.concatenate(blocks, axis=0))
                p_head = jnp.concatenate(cols, axis=1)
                rows = slice(head * NA_HEAD_DIM, (head + 1) * NA_HEAD_DIM)
                vt_head = _token_range(v_refs, tok_lo, tok_hi, rows, 1)
                vt_ext = jnp.concatenate([vt_head, ones_rows], axis=0)
                o = jnp.dot(vt_ext, p_head, preferred_element_type=F32)
                out_ref[sub, rows, :] = (o[:NA_HEAD_DIM] * (1.0 / o[NA_HEAD_DIM:NA_HEAD_DIM + 1])).astype(BF16)

        return scores, softmax_pv

    def body(step_variant):
        fns = [sub_group(sub, *plan) for sub, plan in enumerate(STEP_PLANS[step_variant])]
        stages = [(sub, pair) for sub in range(len(fns)) for pair in range(n_pairs)]
        state = fns[0][0](0)
        for idx, (sub, pair) in enumerate(stages):
            nxt = None
            if idx + 1 < len(stages):
                nxt_sub, nxt_pair = stages[idx + 1]
                nxt = fns[nxt_sub][0](nxt_pair)
            fns[sub][1](pair, state)
            state = nxt

    is_top = step == 0
    is_bot = step == steps - 1
    pl.when(is_top)(lambda: body(0))
    pl.when(jnp.logical_not(is_top | is_bot))(lambda: body(1))
    pl.when(is_bot)(lambda: body(2))


def _attention(qt, k, vt, t2, batch, seq):
    t = batch * seq
    blocks = seq // Q_TOK
    steps = blocks // STEP_SUBS
    assert steps >= 2 and blocks >= KV_BLOCKS

    def first_kv(u):
        return jnp.clip(STEP_SUBS * u - 1, 0, blocks - KV_BLOCKS)

    def blk(u, b):
        return (b * steps + u, 0, 0)

    def kv(i):
        return lambda u, b: (b * blocks + first_kv(u) + i, 0)

    def kv_blk(i):
        return lambda u, b: (b * blocks + first_kv(u) + i, 0, 0)

    return pl.pallas_call(
        functools.partial(_attention_kernel, steps=steps),
        grid=(steps, batch),
        in_specs=[pl.BlockSpec((STEP_SUBS, NA_WIDTH, Q_TOK), blk)]
                 + [pl.BlockSpec((Q_TOK, NA_WIDTH), kv(i)) for i in range(KV_BLOCKS)]
                 + [pl.BlockSpec((1, NA_WIDTH, Q_TOK), kv_blk(i)) for i in range(KV_BLOCKS)]
                 + [pl.BlockSpec(t2.shape, lambda u, b: (0, 0, 0, 0))],
        out_specs=pl.BlockSpec((STEP_SUBS, NA_WIDTH, Q_TOK), blk),
        out_shape=jax.ShapeDtypeStruct((t // Q_TOK, NA_WIDTH, Q_TOK), BF16),
        compiler_params=pltpu.CompilerParams(
            dimension_semantics=("arbitrary", "arbitrary"),
            vmem_limit_bytes=VMEM_LIMIT),
        name="attention",
    )(qt, *([k] * KV_BLOCKS), *([vt] * KV_BLOCKS), t2)


DFT_CHUNK = 16
LANES = 128


def _to_lane_tiles(scr, x):
    for c in range(scr.shape[0]):
        scr[c] = x[:, c * LANES:(c + 1) * LANES]


def _from_lane_tiles(scr):
    return jnp.concatenate([scr[c] for c in range(scr.shape[0])], axis=1)


def _strided_rows(scr, start, size, stride):
    return jnp.concatenate([scr[c, pl.ds(start, size, stride=stride), :] for c in range(scr.shape[0])], axis=1)


def _store_strided_rows(scr, start, stride, x):
    for c in range(scr.shape[0]):
        scr[c, pl.ds(start, x.shape[0], stride=stride), :] = x[:, c * LANES:(c + 1) * LANES]


def _fourier_1_kernel(w_ref, u_ref, a_ref, *scratch, major, chunks):
    if not scratch:
        for c in range(chunks):
            rows = slice(c * DFT_CHUNK, (c + 1) * DFT_CHUNK)
            u2d = u_ref[0, :, rows, :].reshape(major * DFT_CHUNK, F_WIDTH)
            res = jnp.dot(w_ref[...], u2d, preferred_element_type=F32)
            a_ref[0, :, :, rows, :] = res.astype(BF16).reshape(major, 2, DFT_CHUNK, F_WIDTH)
        return
    in_w, out_w = scratch
    pairs = DFT_CHUNK // 2
    _to_lane_tiles(in_w, pltpu.bitcast(u_ref[0].reshape(major * DFT_CHUNK, F_WIDTH), jnp.uint32))
    for j in range(pairs):
        rhs = pltpu.bitcast(_strided_rows(in_w, j, major, pairs), BF16)
        res = jnp.dot(w_ref[...], rhs, preferred_element_type=F32)
        _store_strided_rows(out_w, j, pairs, pltpu.bitcast(res.astype(BF16), jnp.uint32))
    a_ref[0] = pltpu.bitcast(_from_lane_tiles(out_w), BF16).reshape(major, 2, DFT_CHUNK, F_WIDTH)


def _fourier_1(u4, w1, kron):
    batch, major, minor, _ = u4.shape
    chunks = minor // DFT_CHUNK if kron else 1
    tb = chunks * DFT_CHUNK
    scratch = [] if kron else [pltpu.VMEM((F_WIDTH // LANES, major * DFT_CHUNK // 2, LANES), jnp.uint32),
                               pltpu.VMEM((F_WIDTH // LANES, major * DFT_CHUNK, LANES), jnp.uint32)]
    return pl.pallas_call(
        functools.partial(_fourier_1_kernel, major=major, chunks=chunks),
        grid=(batch, minor // tb),
        in_specs=[pl.BlockSpec(w1.shape, lambda b, j: (0, 0)),
                  pl.BlockSpec((1, major, tb, F_WIDTH), lambda b, j: (b, 0, j, 0))],
        out_specs=pl.BlockSpec((1, major, 2, tb, F_WIDTH), lambda b, j: (b, 0, 0, j, 0)),
        out_shape=jax.ShapeDtypeStruct((batch, major, 2, minor, F_WIDTH), BF16),
        scratch_shapes=scratch,
        compiler_params=pltpu.CompilerParams(
            dimension_semantics=("arbitrary", "arbitrary"),
            vmem_limit_bytes=VMEM_LIMIT),
        name="fourier_1",
    )(w1, u4)


def _fourier_2_kernel(a_ref, m_ref, wc_ref, bf_ref, y_ref, x_scr, y_scr):
    for j in range(DFT_CHUNK):
        rhs = a_ref[0, j].reshape(2 * DFT_MINOR, F_WIDTH)
        x = jnp.dot(m_ref[j], rhs, preferred_element_type=F32)
        rows = slice(j * DFT_MINOR, (j + 1) * DFT_MINOR)
        x_scr[rows, :F_WIDTH] = x[:DFT_MINOR].astype(BF16)
        x_scr[rows, F_WIDTH:] = x[DFT_MINOR:].astype(BF16)
    y = jnp.dot(x_scr[...], wc_ref[...], preferred_element_type=F32) + bf_ref[...]
    for j in range(DFT_CHUNK):
        _store_strided_rows(y_scr, j, DFT_CHUNK, y[j * DFT_MINOR:(j + 1) * DFT_MINOR])
    y_ref[0] = _from_lane_tiles(y_scr).astype(BF16).reshape(DFT_MINOR, DFT_CHUNK, F_WIDTH)


def _fold_channel_dft_kernel(cs_ref, wf_ref, wc_ref):
    wc_ref[...] = jnp.dot(cs_ref[...], wf_ref[...], preferred_element_type=F32,
                          precision=lax.Precision.HIGHEST).astype(BF16)


def _fold_channel_dft(w_f):
    c = np.arange(F_GROUP_DIM)
    ang = 2.0 * np.pi * np.outer(c, c) / F_GROUP_DIM
    eye = np.eye(F_GROUPS)
    cs = np.concatenate([np.kron(eye, np.cos(ang)), np.kron(eye, np.sin(ang))], axis=0)
    return pl.pallas_call(
        _fold_channel_dft_kernel,
        out_shape=jax.ShapeDtypeStruct((2 * F_WIDTH, F_WIDTH), BF16),
        name="fold_channel_dft",
    )(jnp.asarray(cs, F32), w_f.astype(F32))


def _fourier_2(a5, m_tab, w_c, b_f):
    batch, major, _, minor, _ = a5.shape
    tr = DFT_CHUNK
    return pl.pallas_call(
        _fourier_2_kernel,
        grid=(batch, major // tr),
        in_specs=[pl.BlockSpec((1, tr, 2, minor, F_WIDTH), lambda b, r: (b, r, 0, 0, 0)),
                  pl.BlockSpec((tr, 2 * minor, 2 * minor), lambda b, r: (r, 0, 0)),
                  pl.BlockSpec(w_c.shape, lambda b, r: (0, 0)),
                  pl.BlockSpec((1, F_WIDTH), lambda b, r: (0, 0))],
        out_specs=pl.BlockSpec((1, minor, tr, F_WIDTH), lambda b, r: (b, 0, r, 0)),
        out_shape=jax.ShapeDtypeStruct((batch, minor, major, F_WIDTH), BF16),
        scratch_shapes=[pltpu.VMEM((tr * minor, 2 * F_WIDTH), BF16),
                        pltpu.VMEM((F_WIDTH // LANES, minor * tr, LANES), F32)],
        compiler_params=pltpu.CompilerParams(
            dimension_semantics=("arbitrary", "arbitrary"),
            vmem_limit_bytes=VMEM_LIMIT),
        name="fourier_2",
    )(a5, m_tab, w_c, b_f)


@functools.lru_cache(maxsize=None)
def _dft_constants(seq):
    major = seq // DFT_MINOR
    kron = major * DFT_CHUNK <= 256
    a = np.arange(major)
    ang1 = 2.0 * np.pi * np.outer(a, a) / major
    w1 = np.stack([np.cos(ang1), -np.sin(ang1)], axis=1).reshape(2 * major, major)
    w1 = np.kron(w1, np.eye(DFT_CHUNK if kron else 2))
    r = np.arange(major)[:, None, None]
    p = np.arange(DFT_MINOR)[None, :, None]
    b = np.arange(DFT_MINOR)[None, None, :]
    ang2 = 2.0 * np.pi * ((b * (r + major * p)) % seq) / seq
    scale = 1.0 / np.sqrt(seq * F_GROUP_DIM)
    e_re, e_im = np.cos(ang2) * scale, -np.sin(ang2) * scale
    m_tab = np.concatenate([np.concatenate([e_re, -e_im], axis=2),
                            np.concatenate([e_im, e_re], axis=2)], axis=1)
    return kron, np.asarray(w1, np.float32), np.asarray(m_tab, np.float32)


def _fourier(u, w_c, b_f, batch, seq):
    major = seq // DFT_MINOR
    kron, w1, m_tab = _dft_constants(seq)
    a = _fourier_1(u.reshape(batch, major, DFT_MINOR, F_WIDTH), jnp.asarray(w1).astype(BF16), kron)
    y = _fourier_2(a, jnp.asarray(m_tab).astype(BF16), w_c, b_f)
    return y.reshape(batch * seq, F_WIDTH)


def _out_proj_kernel(ot_ref, zat_ref, yf_ref, zf_ref, x_ref, wa_ref, wf_ref,
                     gna_ref, gf_ref, gp_ref, o_ref):
    y_t = jnp.concatenate([ot_ref[i] for i in range(ot_ref.shape[0])], axis=1).astype(F32)
    inv = lax.rsqrt(jnp.mean(y_t * y_t, axis=0, keepdims=True) + RMS_EPS)
    mixed_at = _gate(y_t * inv * gna_ref[...], zat_ref[0])
    mixed_f = _gate(_rms(yf_ref[...].astype(F32), gf_ref[...]), zf_ref[...])
    out = lax.dot_general(mixed_at, wa_ref[...], (((0,), (0,)), ((), ())),
                          preferred_element_type=F32)
    out = out + jnp.dot(mixed_f, wf_ref[...], preferred_element_type=F32)
    o_ref[...] = x_ref[...] + _rms(out, gp_ref[...])


def _out_proj(o_t, za_t, y_f, z_f, x2d, w_a, w_fo, g_na_col, g_f, g_post, tm=TOKEN_BLOCK):
    t = x2d.shape[0]
    tok = lambda i: (i, 0)
    blk = lambda i: (i, 0, 0)
    const = lambda i: (0, 0)
    return pl.pallas_call(
        _out_proj_kernel,
        grid=(t // tm,),
        in_specs=[pl.BlockSpec((tm // Q_TOK, NA_WIDTH, Q_TOK), blk),
                  pl.BlockSpec((1, NA_WIDTH, tm), blk),
                  pl.BlockSpec((tm, F_WIDTH), tok),
                  pl.BlockSpec((tm, F_WIDTH), tok),
                  pl.BlockSpec((tm, D_MODEL), tok),
                  pl.BlockSpec(w_a.shape, const),
                  pl.BlockSpec(w_fo.shape, const),
                  pl.BlockSpec((NA_WIDTH, 1), const),
                  pl.BlockSpec((1, F_WIDTH), const),
                  pl.BlockSpec((1, D_MODEL), const)],
        out_specs=pl.BlockSpec((tm, D_MODEL), tok),
        out_shape=jax.ShapeDtypeStruct((t, D_MODEL), F32),
        compiler_params=pltpu.CompilerParams(
            dimension_semantics=("arbitrary",), vmem_limit_bytes=VMEM_LIMIT),
        name="out_proj",
    )(o_t, za_t, y_f, z_f, x2d, w_a, w_fo, g_na_col, g_f, g_post)


def _encoder_layer(x, w_nat, w_tr, bias, w_c, b_f, g_pre, g_na_col, g_f, w_oa, w_of, g_post):
    batch, seq, _ = x.shape
    x2d = x.reshape(batch * seq, D_MODEL)
    k, u_f, z_f, q_t, v_t, za_t = _in_proj(x2d, g_pre, w_nat, w_tr)
    o_t = _attention(q_t, k, v_t, bias, batch, seq)
    y_f = _fourier(u_f, w_c, b_f, batch, seq)
    out = _out_proj(o_t, za_t, y_f, z_f, x2d, w_oa, w_of, g_na_col, g_f, g_post)
    return out.reshape(batch, seq, D_MODEL)


def kernel(x_prompt, x_sample, w_in, rpb, w_fourier, b_fourier, g_pre, g_na, g_f, w_out, g_post):
    depth = w_in.shape[0]
    y_prompt, y_sample = x_prompt, x_sample
    scale = NA_HEAD_DIM ** -0.5
    for l in range(depth):
        w = w_in[l]
        w_q, w_k, w_v, w_za, w_uf, w_zf = (w[:, i * NA_WIDTH:(i + 1) * NA_WIDTH] for i in range(6))
        w_nat = jnp.concatenate([w_k, w_uf, w_zf], axis=1).astype(BF16)
        w_tr = jnp.concatenate([w_q * scale, w_v, w_za], axis=1).astype(BF16).T
        bias = _bias_table(rpb[l])
        row = lambda v: v.reshape(1, -1).astype(F32)
        args = (w_nat, w_tr, bias, _fold_channel_dft(w_fourier[l]), row(b_fourier[l]),
                row(g_pre[l]), g_na[l].reshape(-1, 1).astype(F32), row(g_f[l]),
                w_out[l][:NA_WIDTH].astype(BF16), w_out[l][NA_WIDTH:].astype(BF16), row(g_post[l]))
        y_prompt = _encoder_layer(y_prompt, *args)
        y_sample = _encoder_layer(y_sample, *args)
    return (y_prompt, y_sample)
```

```python
import functools

import numpy as np
import jax
import jax.numpy as jnp
from jax import lax
from jax.experimental import pallas as pl
from jax.experimental.pallas import tpu as pltpu

D_MODEL = 1024
GRID_W = 64
WIN_ROWS = 8
WIN_COLS = 16
NA_HEADS = 8
NA_HEAD_DIM = 64
NA_WIDTH = NA_HEADS * NA_HEAD_DIM
F_GROUPS = 8
F_GROUP_DIM = 64
F_WIDTH = F_GROUPS * F_GROUP_DIM
RMS_EPS = 1e-6
NEG_INF = -1e30

Q_ROWS = 4
Q_TOK = Q_ROWS * GRID_W
STEP_SUBS = 2
KV_BLOCKS = 4
KEY_ROWS = 3 * Q_ROWS
ROW_SLOTS = 2 * WIN_ROWS
DFT_MINOR = 128
TOKEN_BLOCK = 1024
VMEM_LIMIT = 56 * 1024 * 1024

BF16 = jnp.bfloat16
F32 = jnp.float32


def _rms(x, g):
    inv = lax.rsqrt(jnp.mean(x * x, axis=-1, keepdims=True) + RMS_EPS)
    return x * inv * g


def _silu(z):
    return z * (1.0 / (1.0 + jnp.exp(-z)))


def _gate(y_normed, z):
    return y_normed.astype(BF16) * _silu(z)


def _in_proj_kernel(x_ref, g_ref, wn_ref, wt_ref,
                    k_ref, uf_ref, zf_ref, qt_ref, vt_ref, zat_ref):
    h = _rms(x_ref[...], g_ref[...]).astype(BF16)
    nat = jnp.dot(h, wn_ref[...], preferred_element_type=F32)
    k_ref[...] = nat[:, 0 * NA_WIDTH:1 * NA_WIDTH].astype(BF16)
    uf_ref[...] = nat[:, 1 * NA_WIDTH:2 * NA_WIDTH].astype(BF16)
    zf_ref[...] = nat[:, 2 * NA_WIDTH:3 * NA_WIDTH].astype(BF16)
    tr = lax.dot_general(wt_ref[...], h, (((1,), (1,)), ((), ())),
                         preferred_element_type=F32)
    for i in range(qt_ref.shape[0]):
        tok = slice(i * Q_TOK, (i + 1) * Q_TOK)
        qt_ref[i] = tr[0 * NA_WIDTH:1 * NA_WIDTH, tok].astype(BF16)
        vt_ref[i] = tr[1 * NA_WIDTH:2 * NA_WIDTH, tok].astype(BF16)
    zat_ref[0] = tr[2 * NA_WIDTH:3 * NA_WIDTH].astype(BF16)


def _transposed_weights_kernel(col_ref, w_ref, o_ref, *, scale):
    del col_ref
    factor = jnp.where(pl.program_id(0) == 0, scale, 1.0).astype(F32)
    o_ref[...] = (w_ref[...] * factor).T.astype(BF16)


def _transposed_weights(w, groups, scale):
    d, width = w.shape[0], NA_WIDTH
    col = jnp.asarray(groups, jnp.int32)
    return pl.pallas_call(
        functools.partial(_transposed_weights_kernel, scale=scale),
        grid_spec=pltpu.PrefetchScalarGridSpec(
            num_scalar_prefetch=1,
            grid=(len(groups),),
            in_specs=[pl.BlockSpec((d, width), lambda i, col: (0, col[i]))],
            out_specs=pl.BlockSpec((width, d), lambda i, col: (i, 0))),
        out_shape=jax.ShapeDtypeStruct((len(groups) * width, d), BF16),
        name="transposed_weights",
    )(col, w)


def _in_proj(x2d, g_pre, w_nat, w_tr, tm=TOKEN_BLOCK):
    t = x2d.shape[0]
    tok = lambda i: (i, 0)
    blk = lambda i: (i, 0, 0)
    const = lambda i: (0, 0)
    nat_shape = jax.ShapeDtypeStruct((t, NA_WIDTH), BF16)
    qv_shape = jax.ShapeDtypeStruct((t // Q_TOK, NA_WIDTH, Q_TOK), BF16)
    za_shape = jax.ShapeDtypeStruct((t // tm, NA_WIDTH, tm), BF16)
    return pl.pallas_call(
        _in_proj_kernel,
        grid=(t // tm,),
        in_specs=[pl.BlockSpec((tm, D_MODEL), tok),
                  pl.BlockSpec((1, D_MODEL), const),
                  pl.BlockSpec(w_nat.shape, const),
                  pl.BlockSpec(w_tr.shape, const)],
        out_specs=[pl.BlockSpec((tm, NA_WIDTH), tok)] * 3
                  + [pl.BlockSpec((tm // Q_TOK, NA_WIDTH, Q_TOK), blk)] * 2
                  + [pl.BlockSpec((1, NA_WIDTH, tm), blk)],
        out_shape=[nat_shape] * 3 + [qv_shape] * 2 + [za_shape],
        compiler_params=pltpu.CompilerParams(
            dimension_semantics=("arbitrary",), vmem_limit_bytes=VMEM_LIMIT),
        name="in_proj",
    )(x2d, g_pre, w_nat, w_tr)


def _bias_table_kernel(rpb_ref, t2_ref):
    head = pl.program_id(0)
    shape = (GRID_W, 2 * GRID_W)
    kc = lax.broadcasted_iota(jnp.int32, shape, 0)
    lane = lax.broadcasted_iota(jnp.int32, shape, 1)
    second = lane >= GRID_W
    qc = jnp.where(second, lane - GRID_W, lane)
    col_off = kc - qc + (WIN_COLS - 1)
    win_start = jnp.clip(qc - WIN_COLS // 2, 0, GRID_W - WIN_COLS)
    col_valid = (kc >= win_start) & (kc < win_start + WIN_COLS)
    n_off = 2 * WIN_COLS - 1
    n_row = 2 * WIN_ROWS - 1

    def entry(d, o):
        if 0 <= d < n_row:
            return rpb_ref[(head * n_row + d) * n_off + o]
        return jnp.float32(NEG_INF)

    for d in range(ROW_SLOTS):
        tile = jnp.full(shape, NEG_INF, F32)
        for o in range(n_off):
            val = jnp.where(second, entry(d - 1, o), entry(d, o))
            tile = jnp.where(col_off == o, val, tile)
        t2_ref[0, d] = jnp.where(col_valid, tile, NEG_INF)


def _bias_table(rpb):
    return pl.pallas_call(
        _bias_table_kernel,
        grid=(NA_HEADS,),
        in_specs=[pl.BlockSpec(memory_space=pltpu.SMEM)],
        out_specs=pl.BlockSpec((1, ROW_SLOTS, GRID_W, 2 * GRID_W), lambda h: (h, 0, 0, 0)),
        out_shape=jax.ShapeDtypeStruct((NA_HEADS, ROW_SLOTS, GRID_W, 2 * GRID_W), F32),
        compiler_params=pltpu.CompilerParams(dimension_semantics=("arbitrary",)),
        name="bias_table",
    )(rpb.astype(F32).reshape(-1))


def _token_range(refs, lo, hi, other, axis):
    pieces = []
    for idx, ref in enumerate(refs):
        a, b = max(lo, idx * Q_TOK), min(hi, (idx + 1) * Q_TOK)
        if a < b:
            tok = slice(a - idx * Q_TOK, b - idx * Q_TOK)
            pieces.append(ref[tok, other] if axis == 0 else ref[0, other, tok])
    return pieces[0] if len(pieces) == 1 else jnp.concatenate(pieces, axis=axis)


def _window_plan(variant, jp):
    if variant == 0:
        return [(i, i - 2 * jp + WIN_ROWS - 1, None) for i in range(WIN_ROWS)]
    if variant == 2:
        q_row = KEY_ROWS - Q_ROWS + 2 * jp
        return [(i, i - q_row + WIN_ROWS - 1, None) for i in range(KEY_ROWS - WIN_ROWS, KEY_ROWS)]
    q_row = Q_ROWS + 2 * jp
    plan = []
    for i in range(q_row - WIN_ROWS // 2, q_row + WIN_ROWS // 2 + 1):
        half = "low" if i == q_row - WIN_ROWS // 2 else "high" if i == q_row + WIN_ROWS // 2 else None
        plan.append((i, i - q_row + WIN_ROWS - 1, half))
    return plan


STEP_PLANS = (((0, 0), (1, 0)),
              ((1, 0), (1, Q_ROWS)),
              ((1, Q_ROWS), (2, Q_ROWS)))


def _attention_kernel(qt_ref, *refs, steps):
    k_refs, v_refs = refs[:KV_BLOCKS], refs[KV_BLOCKS:2 * KV_BLOCKS]
    t2_ref, out_ref = refs[2 * KV_BLOCKS:]
    step = pl.program_id(0)
    pair_lanes = 2 * NA_HEAD_DIM
    q_lanes = 2 * GRID_W
    n_pairs = NA_HEADS // 2

    def sub_group(sub, variant, key_off):
        plans = [_window_plan(variant, jp) for jp in range(Q_ROWS // 2)]
        key_lo = min(p[0][0] for p in plans)
        key_top = max(p[-1][0] for p in plans) + 1
        key_hi = key_top + (key_top - key_lo) % 2
        tok_lo, tok_top, tok_hi = ((key_off + r) * GRID_W for r in (key_lo, key_top, key_hi))
        lane = lax.broadcasted_iota(jnp.int32, (1, q_lanes), 1)
        half_mask = {"low": jnp.where(lane < GRID_W, 0.0, NEG_INF).astype(F32),
                     "high": jnp.where(lane >= GRID_W, 0.0, NEG_INF).astype(F32)}
        ones_rows = jnp.ones((16, tok_hi - tok_lo), BF16)
        zero_blk = jnp.zeros((GRID_W, q_lanes), BF16)

        def scores(pair):
            cols = slice(pair * pair_lanes, (pair + 1) * pair_lanes)
            k_pair = _token_range(k_refs, tok_lo, tok_top, cols, 0)
            qt_pair = qt_ref[sub, cols, :]
            zero = jnp.zeros((NA_HEAD_DIM, Q_TOK), BF16)
            qt_both = jnp.concatenate(
                [jnp.concatenate([qt_pair[:NA_HEAD_DIM], zero], axis=0),
                 jnp.concatenate([zero, qt_pair[NA_HEAD_DIM:]], axis=0)], axis=1)
            s = jnp.dot(k_pair, qt_both, preferred_element_type=F32)
            tiles = [(head_sub, jp) for head_sub in range(2) for jp in range(len(plans))]
            m_acc = [None] * len(tiles)
            biased = [{} for _ in tiles]
            for i in range(key_lo, key_top):
                rows = slice((i - key_lo) * GRID_W, (i - key_lo + 1) * GRID_W)
                for t, (head_sub, jp) in enumerate(tiles):
                    entry = [e for e in plans[jp] if e[0] == i]
                    if not entry:
                        continue
                    _, slot_d, half = entry[0]
                    first = head_sub * Q_TOK + jp * q_lanes
                    lanes = slice(first, first + q_lanes)
                    blk = s[rows, lanes] + t2_ref[2 * pair + head_sub, slot_d]
                    if half is not None:
                        blk = blk + half_mask[half]
                    biased[t][i] = blk
                    blk_max = jnp.max(blk.reshape(GRID_W // 8, 8, q_lanes), axis=0)
                    m_acc[t] = blk_max if m_acc[t] is None else jnp.maximum(m_acc[t], blk_max)
            maxima = [jnp.max(m, axis=0, keepdims=True) for m in m_acc]
            return maxima, biased

        def softmax_pv(pair, state):
            maxima, biased = state
            for head_sub in range(2):
                head = 2 * pair + head_sub
                cols = []
                for jp in range(len(plans)):
                    t = 2 * head_sub + jp
                    blocks = [jnp.exp(biased[t][i] - maxima[t]).astype(BF16) if i in biased[t] else zero_blk
                              for i in range(key_lo, key_hi)]
                    cols.append(jnp.concatenate(blocks, axis=0))
                p_head = jnp.concatenate(cols, axis=1)
                rows = slice(head * NA_HEAD_DIM, (head + 1) * NA_HEAD_DIM)
                vt_head = _token_range(v_refs, tok_lo, tok_hi, rows, 1)
                vt_ext = jnp.concatenate([vt_head, ones_rows], axis=0)
                o = jnp.dot(vt_ext, p_head, preferred_element_type=F32)
                out_ref[sub, rows, :] = (o[:NA_HEAD_DIM] * (1.0 / o[NA_HEAD_DIM:NA_HEAD_DIM + 1])).astype(BF16)

        return scores, softmax_pv

    def body(step_variant):
        fns = [sub_group(sub, *plan) for sub, plan in enumerate(STEP_PLANS[step_variant])]
        stages = [(sub, pair) for sub in range(len(fns)) for pair in range(n_pairs)]
        state = fns[0][0](0)
        for idx, (sub, pair) in enumerate(stages):
            nxt = None
            if idx + 1 < len(stages):
                nxt_sub, nxt_pair = stages[idx + 1]
                nxt = fns[nxt_sub][0](nxt_pair)
            fns[sub][1](pair, state)
            state = nxt

    is_top = step == 0
    is_bot = step == steps - 1
    pl.when(is_top)(lambda: body(0))
    pl.when(jnp.logical_not(is_top | is_bot))(lambda: body(1))
    pl.when(is_bot)(lambda: body(2))


def _attention(qt, k, vt, t2, batch, seq):
    t = batch * seq
    blocks = seq // Q_TOK
    steps = blocks // STEP_SUBS
    assert steps >= 2 and blocks >= KV_BLOCKS

    def first_kv(u):
        return jnp.clip(STEP_SUBS * u - 1, 0, blocks - KV_BLOCKS)

    def blk(u, b):
        return (b * steps + u, 0, 0)

    def kv(i):
        return lambda u, b: (b * blocks + first_kv(u) + i, 0)

    def kv_blk(i):
        return lambda u, b: (b * blocks + first_kv(u) + i, 0, 0)

    return pl.pallas_call(
        functools.partial(_attention_kernel, steps=steps),
        grid=(steps, batch),
        in_specs=[pl.BlockSpec((STEP_SUBS, NA_WIDTH, Q_TOK), blk)]
                 + [pl.BlockSpec((Q_TOK, NA_WIDTH), kv(i)) for i in range(KV_BLOCKS)]
                 + [pl.BlockSpec((1, NA_WIDTH, Q_TOK), kv_blk(i)) for i in range(KV_BLOCKS)]
                 + [pl.BlockSpec(t2.shape, lambda u, b: (0, 0, 0, 0))],
        out_specs=pl.BlockSpec((STEP_SUBS, NA_WIDTH, Q_TOK), blk),
        out_shape=jax.ShapeDtypeStruct((t // Q_TOK, NA_WIDTH, Q_TOK), BF16),
        compiler_params=pltpu.CompilerParams(
            dimension_semantics=("arbitrary", "arbitrary"),
            vmem_limit_bytes=VMEM_LIMIT),
        name="attention",
    )(qt, *([k] * KV_BLOCKS), *([vt] * KV_BLOCKS), t2)


DFT_CHUNK = 16
LANES = 128


def _to_lane_tiles(scr, x):
    for c in range(scr.shape[0]):
        scr[c] = x[:, c * LANES:(c + 1) * LANES]


def _from_lane_tiles(scr):
    return jnp.concatenate([scr[c] for c in range(scr.shape[0])], axis=1)


def _strided_rows(scr, start, size, stride):
    return jnp.concatenate([scr[c, pl.ds(start, size, stride=stride), :] for c in range(scr.shape[0])], axis=1)


def _store_strided_rows(scr, start, stride, x):
    for c in range(scr.shape[0]):
        scr[c, pl.ds(start, x.shape[0], stride=stride), :] = x[:, c * LANES:(c + 1) * LANES]


def _fourier_1_kernel(w_ref, u_ref, a_ref, *scratch, major, chunks):
    if not scratch:
        for c in range(chunks):
            rows = slice(c * DFT_CHUNK, (c + 1) * DFT_CHUNK)
            u2d = u_ref[0, :, rows, :].reshape(major * DFT_CHUNK, F_WIDTH)
            res = jnp.dot(w_ref[...], u2d, preferred_element_type=F32)
            a_ref[0, :, :, rows, :] = res.astype(BF16).reshape(major, 2, DFT_CHUNK, F_WIDTH)
        return
    in_w, out_w = scratch
    pairs = DFT_CHUNK // 2
    _to_lane_tiles(in_w, pltpu.bitcast(u_ref[0].reshape(major * DFT_CHUNK, F_WIDTH), jnp.uint32))
    for j in range(pairs):
        rhs = pltpu.bitcast(_strided_rows(in_w, j, major, pairs), BF16)
        res = jnp.dot(w_ref[...], rhs, preferred_element_type=F32)
        _store_strided_rows(out_w, j, pairs, pltpu.bitcast(res.astype(BF16), jnp.uint32))
    a_ref[0] = pltpu.bitcast(_from_lane_tiles(out_w), BF16).reshape(major, 2, DFT_CHUNK, F_WIDTH)


def _fourier_1(u4, w1, kron):
    batch, major, minor, _ = u4.shape
    chunks = minor // DFT_CHUNK if kron else 1
    tb = chunks * DFT_CHUNK
    scratch = [] if kron else [pltpu.VMEM((F_WIDTH // LANES, major * DFT_CHUNK // 2, LANES), jnp.uint32),
                               pltpu.VMEM((F_WIDTH // LANES, major * DFT_CHUNK, LANES), jnp.uint32)]
    return pl.pallas_call(
        functools.partial(_fourier_1_kernel, major=major, chunks=chunks),
        grid=(batch, minor // tb),
        in_specs=[pl.BlockSpec(w1.shape, lambda b, j: (0, 0)),
                  pl.BlockSpec((1, major, tb, F_WIDTH), lambda b, j: (b, 0, j, 0))],
        out_specs=pl.BlockSpec((1, major, 2, tb, F_WIDTH), lambda b, j: (b, 0, 0, j, 0)),
        out_shape=jax.ShapeDtypeStruct((batch, major, 2, minor, F_WIDTH), BF16),
        scratch_shapes=scratch,
        compiler_params=pltpu.CompilerParams(
            dimension_semantics=("arbitrary", "arbitrary"),
            vmem_limit_bytes=VMEM_LIMIT),
        name="fourier_1",
    )(w1, u4)


def _fourier_2_kernel(a_ref, m_ref, wc_ref, bf_ref, y_ref, x_scr, y_scr):
    for j in range(DFT_CHUNK):
        rhs = a_ref[0, j].reshape(2 * DFT_MINOR, F_WIDTH)
        x = jnp.dot(m_ref[j], rhs, preferred_element_type=F32)
        rows = slice(j * DFT_MINOR, (j + 1) * DFT_MINOR)
        x_scr[rows, :F_WIDTH] = x[:DFT_MINOR].astype(BF16)
        x_scr[rows, F_WIDTH:] = x[DFT_MINOR:].astype(BF16)
    y = jnp.dot(x_scr[...], wc_ref[...], preferred_element_type=F32) + bf_ref[...]
    pitch = DFT_CHUNK + 1
    for j in range(DFT_CHUNK):
        _store_strided_rows(y_scr, j, pitch, y[j * DFT_MINOR:(j + 1) * DFT_MINOR])
    packed = jnp.concatenate(
        [jnp.concatenate([y_scr[c, p * pitch:p * pitch + DFT_CHUNK, :] for p in range(DFT_MINOR)], axis=0)
         for c in range(y_scr.shape[0])], axis=1)
    y_ref[0] = packed.astype(BF16).reshape(DFT_MINOR, DFT_CHUNK, F_WIDTH)


def _fold_channel_dft_kernel(cs_ref, wf_ref, wc_ref):
    wc_ref[...] = jnp.dot(cs_ref[...], wf_ref[...], preferred_element_type=F32,
                          precision=lax.Precision.HIGHEST).astype(BF16)


def _fold_channel_dft(w_f):
    c = np.arange(F_GROUP_DIM)
    ang = 2.0 * np.pi * np.outer(c, c) / F_GROUP_DIM
    eye = np.eye(F_GROUPS)
    cs = np.concatenate([np.kron(eye, np.cos(ang)), np.kron(eye, np.sin(ang))], axis=0)
    return pl.pallas_call(
        _fold_channel_dft_kernel,
        out_shape=jax.ShapeDtypeStruct((2 * F_WIDTH, F_WIDTH), BF16),
        name="fold_channel_dft",
    )(jnp.asarray(cs, F32), w_f.astype(F32))


def _fourier_2(a5, m_tab, w_c, b_f):
    batch, major, _, minor, _ = a5.shape
    tr = DFT_CHUNK
    return pl.pallas_call(
        _fourier_2_kernel,
        grid=(batch, major // tr),
        in_specs=[pl.BlockSpec((1, tr, 2, minor, F_WIDTH), lambda b, r: (b, r, 0, 0, 0)),
                  pl.BlockSpec((tr, 2 * minor, 2 * minor), lambda b, r: (r, 0, 0)),
                  pl.BlockSpec(w_c.shape, lambda b, r: (0, 0)),
                  pl.BlockSpec((1, F_WIDTH), lambda b, r: (0, 0))],
        out_specs=pl.BlockSpec((1, minor, tr, F_WIDTH), lambda b, r: (b, 0, r, 0)),
        out_shape=jax.ShapeDtypeStruct((batch, minor, major, F_WIDTH), BF16),
        scratch_shapes=[pltpu.VMEM((tr * minor, 2 * F_WIDTH), BF16),
                        pltpu.VMEM((F_WIDTH // LANES, minor * (tr + 1), LANES), F32)],
        compiler_params=pltpu.CompilerParams(
            dimension_semantics=("arbitrary", "arbitrary"),
            vmem_limit_bytes=VMEM_LIMIT),
        name="fourier_2",
    )(a5, m_tab, w_c, b_f)


@functools.lru_cache(maxsize=None)
def _dft_constants(seq):
    major = seq // DFT_MINOR
    kron = major * DFT_CHUNK <= 256
    a = np.arange(major)
    ang1 = 2.0 * np.pi * np.outer(a, a) / major
    w1 = np.stack([np.cos(ang1), -np.sin(ang1)], axis=1).reshape(2 * major, major)
    w1 = np.kron(w1, np.eye(DFT_CHUNK if kron else 2))
    r = np.arange(major)[:, None, None]
    p = np.arange(DFT_MINOR)[None, :, None]
    b = np.arange(DFT_MINOR)[None, None, :]
    ang2 = 2.0 * np.pi * ((b * (r + major * p)) % seq) / seq
    scale = 1.0 / np.sqrt(seq * F_GROUP_DIM)
    e_re, e_im = np.cos(ang2) * scale, -np.sin(ang2) * scale
    m_tab = np.concatenate([np.concatenate([e_re, -e_im], axis=2),
                            np.concatenate([e_im, e_re], axis=2)], axis=1)
    return kron, np.asarray(w1, np.float32), np.asarray(m_tab, np.float32)


def _fourier(u, w_c, b_f, batch, seq):
    major = seq // DFT_MINOR
    kron, w1, m_tab = _dft_constants(seq)
    a = _fourier_1(u.reshape(batch, major, DFT_MINOR, F_WIDTH), jnp.asarray(w1).astype(BF16), kron)
    y = _fourier_2(a, jnp.asarray(m_tab).astype(BF16), w_c, b_f)
    return y.reshape(batch * seq, F_WIDTH)


def _out_proj_kernel(ot_ref, zat_ref, yf_ref, zf_ref, x_ref, wa_ref, wf_ref,
                     gna_ref, gf_ref, gp_ref, o_ref):
    y_t = jnp.concatenate([ot_ref[i] for i in range(ot_ref.shape[0])], axis=1).astype(F32)
    inv = lax.rsqrt(jnp.mean(y_t * y_t, axis=0, keepdims=True) + RMS_EPS)
    mixed_at = _gate(y_t * inv * gna_ref[...], zat_ref[0])
    mixed_f = _gate(_rms(yf_ref[...].astype(F32), gf_ref[...]), zf_ref[...])
    out = lax.dot_general(mixed_at, wa_ref[...], (((0,), (0,)), ((), ())),
                          preferred_element_type=F32)
    out = out + jnp.dot(mixed_f, wf_ref[...], preferred_element_type=F32)
    o_ref[...] = x_ref[...] + _rms(out, gp_ref[...])


def _out_proj(o_t, za_t, y_f, z_f, x2d, w_a, w_fo, g_na_col, g_f, g_post, tm=TOKEN_BLOCK):
    t = x2d.shape[0]
    tok = lambda i: (i, 0)
    blk = lambda i: (i, 0, 0)
    const = lambda i: (0, 0)
    return pl.pallas_call(
        _out_proj_kernel,
        grid=(t // tm,),
        in_specs=[pl.BlockSpec((tm // Q_TOK, NA_WIDTH, Q_TOK), blk),
                  pl.BlockSpec((1, NA_WIDTH, tm), blk),
                  pl.BlockSpec((tm, F_WIDTH), tok),
                  pl.BlockSpec((tm, F_WIDTH), tok),
                  pl.BlockSpec((tm, D_MODEL), tok),
                  pl.BlockSpec(w_a.shape, const),
                  pl.BlockSpec(w_fo.shape, const),
                  pl.BlockSpec((NA_WIDTH, 1), const),
                  pl.BlockSpec((1, F_WIDTH), const),
                  pl.BlockSpec((1, D_MODEL), const)],
        out_specs=pl.BlockSpec((tm, D_MODEL), tok),
        out_shape=jax.ShapeDtypeStruct((t, D_MODEL), F32),
        compiler_params=pltpu.CompilerParams(
            dimension_semantics=("arbitrary",), vmem_limit_bytes=VMEM_LIMIT),
        name="out_proj",
    )(o_t, za_t, y_f, z_f, x2d, w_a, w_fo, g_na_col, g_f, g_post)


def _encoder_layer(x, w_nat, w_tr, bias, w_c, b_f, g_pre, g_na_col, g_f, w_oa, w_of, g_post):
    batch, seq, _ = x.shape
    x2d = x.reshape(batch * seq, D_MODEL)
    k, u_f, z_f, q_t, v_t, za_t = _in_proj(x2d, g_pre, w_nat, w_tr)
    o_t = _attention(q_t, k, v_t, bias, batch, seq)
    y_f = _fourier(u_f, w_c, b_f, batch, seq)
    out = _out_proj(o_t, za_t, y_f, z_f, x2d, w_oa, w_of, g_na_col, g_f, g_post)
    return out.reshape(batch, seq, D_MODEL)


def kernel(x_prompt, x_sample, w_in, rpb, w_fourier, b_fourier, g_pre, g_na, g_f, w_out, g_post):
    depth = w_in.shape[0]
    y_prompt, y_sample = x_prompt, x_sample
    scale = NA_HEAD_DIM ** -0.5
    for l in range(depth):
        w = w_in[l]
        w_nat = jnp.concatenate([w[:, i * NA_WIDTH:(i + 1) * NA_WIDTH] for i in (1, 4, 5)],
                                axis=1).astype(BF16)
        w_tr = _transposed_weights(w, (0, 2, 3), scale)
        bias = _bias_table(rpb[l])
        row = lambda v: v.reshape(1, -1).astype(F32)
        args = (w_nat, w_tr, bias, _fold_channel_dft(w_fourier[l]), row(b_fourier[l]),
                row(g_pre[l]), g_na[l].reshape(-1, 1).astype(F32), row(g_f[l]),
                w_out[l][:NA_WIDTH].astype(BF16), w_out[l][NA_WIDTH:].astype(BF16), row(g_post[l]))
        y_prompt = _encoder_layer(y_prompt, *args)
        y_sample = _encoder_layer(y_sample, *args)
    return (y_prompt, y_sample)
```

```python
import functools

import numpy as np
import jax
import jax.numpy as jnp
from jax import lax
from jax.experimental import pallas as pl
from jax.experimental.pallas import tpu as pltpu

D_MODEL = 1024
GRID_W = 64
WIN_ROWS = 8
WIN_COLS = 16
NA_HEADS = 8
NA_HEAD_DIM = 64
NA_WIDTH = NA_HEADS * NA_HEAD_DIM
F_GROUPS = 8
F_GROUP_DIM = 64
F_WIDTH = F_GROUPS * F_GROUP_DIM
RMS_EPS = 1e-6
NEG_INF = -1e30

Q_ROWS = 4
Q_TOK = Q_ROWS * GRID_W
STEP_SUBS = 4
KV_BLOCKS = STEP_SUBS + 2
KEY_ROWS = 3 * Q_ROWS
ROW_SLOTS = 2 * WIN_ROWS
DFT_MINOR = 128
TOKEN_BLOCK = 1024
VMEM_LIMIT = 56 * 1024 * 1024

BF16 = jnp.bfloat16
F32 = jnp.float32


def _rms(x, g):
    inv = lax.rsqrt(jnp.mean(x * x, axis=-1, keepdims=True) + RMS_EPS)
    return x * inv * g


def _silu(z):
    return z * (1.0 / (1.0 + jnp.exp(-z)))


def _gate(y_normed, z):
    return y_normed.astype(BF16) * _silu(z)


def _in_proj_kernel(x_ref, g_ref, wn_ref, wt_ref,
                    k_ref, uf_ref, zf_ref, qt_ref, vt_ref, zat_ref):
    h = _rms(x_ref[...], g_ref[...]).astype(BF16)
    nat = jnp.dot(h, wn_ref[...], preferred_element_type=F32)
    k_ref[...] = nat[:, 0 * NA_WIDTH:1 * NA_WIDTH].astype(BF16)
    uf_ref[...] = nat[:, 1 * NA_WIDTH:2 * NA_WIDTH].astype(BF16)
    zf_ref[...] = nat[:, 2 * NA_WIDTH:3 * NA_WIDTH].astype(BF16)
    tr = lax.dot_general(wt_ref[...], h, (((1,), (1,)), ((), ())),
                         preferred_element_type=F32)
    for i in range(qt_ref.shape[0]):
        tok = slice(i * Q_TOK, (i + 1) * Q_TOK)
        qt_ref[i] = tr[0 * NA_WIDTH:1 * NA_WIDTH, tok].astype(BF16)
        vt_ref[i] = tr[1 * NA_WIDTH:2 * NA_WIDTH, tok].astype(BF16)
    zat_ref[0] = tr[2 * NA_WIDTH:3 * NA_WIDTH].astype(BF16)


def _transposed_weights_kernel(col_ref, w_ref, o_ref, *, scale):
    del col_ref
    factor = jnp.where(pl.program_id(0) == 0, scale, 1.0).astype(F32)
    o_ref[...] = (w_ref[...] * factor).T.astype(BF16)


def _transposed_weights(w, groups, scale):
    d, width = w.shape[0], NA_WIDTH
    col = jnp.asarray(groups, jnp.int32)
    return pl.pallas_call(
        functools.partial(_transposed_weights_kernel, scale=scale),
        grid_spec=pltpu.PrefetchScalarGridSpec(
            num_scalar_prefetch=1,
            grid=(len(groups),),
            in_specs=[pl.BlockSpec((d, width), lambda i, col: (0, col[i]))],
            out_specs=pl.BlockSpec((width, d), lambda i, col: (i, 0))),
        out_shape=jax.ShapeDtypeStruct((len(groups) * width, d), BF16),
        name="transposed_weights",
    )(col, w)


def _in_proj(x2d, g_pre, w_nat, w_tr, tm=TOKEN_BLOCK):
    t = x2d.shape[0]
    tok = lambda i: (i, 0)
    blk = lambda i: (i, 0, 0)
    const = lambda i: (0, 0)
    nat_shape = jax.ShapeDtypeStruct((t, NA_WIDTH), BF16)
    qv_shape = jax.ShapeDtypeStruct((t // Q_TOK, NA_WIDTH, Q_TOK), BF16)
    za_shape = jax.ShapeDtypeStruct((t // tm, NA_WIDTH, tm), BF16)
    return pl.pallas_call(
        _in_proj_kernel,
        grid=(t // tm,),
        in_specs=[pl.BlockSpec((tm, D_MODEL), tok),
                  pl.BlockSpec((1, D_MODEL), const),
                  pl.BlockSpec(w_nat.shape, const),
                  pl.BlockSpec(w_tr.shape, const)],
        out_specs=[pl.BlockSpec((tm, NA_WIDTH), tok)] * 3
                  + [pl.BlockSpec((tm // Q_TOK, NA_WIDTH, Q_TOK), blk)] * 2
                  + [pl.BlockSpec((1, NA_WIDTH, tm), blk)],
        out_shape=[nat_shape] * 3 + [qv_shape] * 2 + [za_shape],
        compiler_params=pltpu.CompilerParams(
            dimension_semantics=("arbitrary",), vmem_limit_bytes=VMEM_LIMIT),
        name="in_proj",
    )(x2d, g_pre, w_nat, w_tr)


def _bias_table_kernel(rpb_ref, t2_ref):
    head = pl.program_id(0)
    shape = (GRID_W, 2 * GRID_W)
    kc = lax.broadcasted_iota(jnp.int32, shape, 0)
    lane = lax.broadcasted_iota(jnp.int32, shape, 1)
    second = lane >= GRID_W
    qc = jnp.where(second, lane - GRID_W, lane)
    col_off = kc - qc + (WIN_COLS - 1)
    win_start = jnp.clip(qc - WIN_COLS // 2, 0, GRID_W - WIN_COLS)
    col_valid = (kc >= win_start) & (kc < win_start + WIN_COLS)
    n_off = 2 * WIN_COLS - 1
    n_row = 2 * WIN_ROWS - 1

    def entry(d, o):
        if 0 <= d < n_row:
            return rpb_ref[(head * n_row + d) * n_off + o]
        return jnp.float32(NEG_INF)

    for d in range(ROW_SLOTS):
        tile = jnp.full(shape, NEG_INF, F32)
        for o in range(n_off):
            val = jnp.where(second, entry(d - 1, o), entry(d, o))
            tile = jnp.where(col_off == o, val, tile)
        t2_ref[0, d] = jnp.where(col_valid, tile, NEG_INF)


def _bias_table(rpb):
    return pl.pallas_call(
        _bias_table_kernel,
        grid=(NA_HEADS,),
        in_specs=[pl.BlockSpec(memory_space=pltpu.SMEM)],
        out_specs=pl.BlockSpec((1, ROW_SLOTS, GRID_W, 2 * GRID_W), lambda h: (h, 0, 0, 0)),
        out_shape=jax.ShapeDtypeStruct((NA_HEADS, ROW_SLOTS, GRID_W, 2 * GRID_W), F32),
        compiler_params=pltpu.CompilerParams(dimension_semantics=("arbitrary",)),
        name="bias_table",
    )(rpb.astype(F32).reshape(-1))


def _token_range(refs, lo, hi, other, axis):
    pieces = []
    for idx, ref in enumerate(refs):
        a, b = max(lo, idx * Q_TOK), min(hi, (idx + 1) * Q_TOK)
        if a < b:
            tok = slice(a - idx * Q_TOK, b - idx * Q_TOK)
            pieces.append(ref[tok, other] if axis == 0 else ref[0, other, tok])
    return pieces[0] if len(pieces) == 1 else jnp.concatenate(pieces, axis=axis)


def _window_plan(variant, jp):
    if variant == 0:
        return [(i, i - 2 * jp + WIN_ROWS - 1, None) for i in range(WIN_ROWS)]
    if variant == 2:
        q_row = KEY_ROWS - Q_ROWS + 2 * jp
        return [(i, i - q_row + WIN_ROWS - 1, None) for i in range(KEY_ROWS - WIN_ROWS, KEY_ROWS)]
    q_row = Q_ROWS + 2 * jp
    plan = []
    for i in range(q_row - WIN_ROWS // 2, q_row + WIN_ROWS // 2 + 1):
        half = "low" if i == q_row - WIN_ROWS // 2 else "high" if i == q_row + WIN_ROWS // 2 else None
        plan.append((i, i - q_row + WIN_ROWS - 1, half))
    return plan


def _step_plan(position):
    plan = []
    for j in range(STEP_SUBS):
        if position == 0:
            plan.append((0, 0) if j == 0 else (1, (j - 1) * Q_ROWS))
        elif position == 1:
            plan.append((1, j * Q_ROWS))
        else:
            last = j == STEP_SUBS - 1
            plan.append((2 if last else 1, (j + 1 - last) * Q_ROWS))
    return tuple(plan)


def _attention_kernel(qt_ref, *refs, steps):
    k_refs, v_refs = refs[:KV_BLOCKS], refs[KV_BLOCKS:2 * KV_BLOCKS]
    t2_ref, out_ref = refs[2 * KV_BLOCKS:]
    step = pl.program_id(0)
    pair_lanes = 2 * NA_HEAD_DIM
    q_lanes = 2 * GRID_W
    n_pairs = NA_HEADS // 2

    def sub_group(sub, variant, key_off):
        plans = [_window_plan(variant, jp) for jp in range(Q_ROWS // 2)]
        key_lo = min(p[0][0] for p in plans)
        key_top = max(p[-1][0] for p in plans) + 1
        key_hi = key_top + (key_top - key_lo) % 2
        tok_lo, tok_top, tok_hi = ((key_off + r) * GRID_W for r in (key_lo, key_top, key_hi))
        lane = lax.broadcasted_iota(jnp.int32, (1, q_lanes), 1)
        half_mask = {"low": jnp.where(lane < GRID_W, 0.0, NEG_INF).astype(F32),
                     "high": jnp.where(lane >= GRID_W, 0.0, NEG_INF).astype(F32)}
        ones_rows = jnp.ones((16, tok_hi - tok_lo), BF16)
        zero_blk = jnp.zeros((GRID_W, q_lanes), BF16)

        def scores(pair):
            cols = slice(pair * pair_lanes, (pair + 1) * pair_lanes)
            k_pair = _token_range(k_refs, tok_lo, tok_top, cols, 0)
            qt_pair = qt_ref[sub, cols, :]
            zero = jnp.zeros((NA_HEAD_DIM, Q_TOK), BF16)
            qt_both = jnp.concatenate(
                [jnp.concatenate([qt_pair[:NA_HEAD_DIM], zero], axis=0),
                 jnp.concatenate([zero, qt_pair[NA_HEAD_DIM:]], axis=0)], axis=1)
            s = jnp.dot(k_pair, qt_both, preferred_element_type=F32)
            tiles = [(head_sub, jp) for head_sub in range(2) for jp in range(len(plans))]
            m_acc = [None] * len(tiles)
            biased = [{} for _ in tiles]
            for i in range(key_lo, key_top):
                rows = slice((i - key_lo) * GRID_W, (i - key_lo + 1) * GRID_W)
                for t, (head_sub, jp) in enumerate(tiles):
                    entry = [e for e in plans[jp] if e[0] == i]
                    if not entry:
                        continue
                    _, slot_d, half = entry[0]
                    first = head_sub * Q_TOK + jp * q_lanes
                    lanes = slice(first, first + q_lanes)
                    blk = s[rows, lanes] + t2_ref[2 * pair + head_sub, slot_d]
                    if half is not None:
                        blk = blk + half_mask[half]
                    biased[t][i] = blk
                    blk_max = jnp.max(blk.reshape(GRID_W // 8, 8, q_lanes), axis=0)
                    m_acc[t] = blk_max if m_acc[t] is None else jnp.maximum(m_acc[t], blk_max)
            maxima = [jnp.max(m, axis=0, keepdims=True) for m in m_acc]
            return maxima, biased

        def softmax_pv(pair, state):
            maxima, biased = state
            for head_sub in range(2):
                head = 2 * pair + head_sub
                cols = []
                for jp in range(len(plans)):
                    t = 2 * head_sub + jp
                    blocks = [jnp.exp(biased[t][i] - maxima[t]).astype(BF16) if i in biased[t] else zero_blk
                              for i in range(key_lo, key_hi)]
                    cols.append(jnp.concatenate(blocks, axis=0))
                p_head = jnp.concatenate(cols, axis=1)
                rows = slice(head * NA_HEAD_DIM, (head + 1) * NA_HEAD_DIM)
                vt_head = _token_range(v_refs, tok_lo, tok_hi, rows, 1)
                vt_ext = jnp.concatenate([vt_head, ones_rows], axis=0)
                o = jnp.dot(vt_ext, p_head, preferred_element_type=F32)
                out_ref[sub, rows, :] = (o[:NA_HEAD_DIM] * (1.0 / o[NA_HEAD_DIM:NA_HEAD_DIM + 1])).astype(BF16)

        return scores, softmax_pv

    def body(step_variant):
        fns = [sub_group(sub, *plan) for sub, plan in enumerate(_step_plan(step_variant))]
        stages = [(sub, pair) for sub in range(len(fns)) for pair in range(n_pairs)]
        state = fns[0][0](0)
        for idx, (sub, pair) in enumerate(stages):
            nxt = None
            if idx + 1 < len(stages):
                nxt_sub, nxt_pair = stages[idx + 1]
                nxt = fns[nxt_sub][0](nxt_pair)
            fns[sub][1](pair, state)
            state = nxt

    is_top = step == 0
    is_bot = step == steps - 1
    pl.when(is_top)(lambda: body(0))
    pl.when(jnp.logical_not(is_top | is_bot))(lambda: body(1))
    pl.when(is_bot)(lambda: body(2))


def _attention(qt, k, vt, t2, batch, seq):
    t = batch * seq
    blocks = seq // Q_TOK
    steps = blocks // STEP_SUBS
    assert steps >= 2 and blocks >= KV_BLOCKS

    def first_kv(u):
        return jnp.clip(STEP_SUBS * u - 1, 0, blocks - KV_BLOCKS)

    def blk(u, b):
        return (b * steps + u, 0, 0)

    def kv(i):
        return lambda u, b: (b * blocks + first_kv(u) + i, 0)

    def kv_blk(i):
        return lambda u, b: (b * blocks + first_kv(u) + i, 0, 0)

    return pl.pallas_call(
        functools.partial(_attention_kernel, steps=steps),
        grid=(steps, batch),
        in_specs=[pl.BlockSpec((STEP_SUBS, NA_WIDTH, Q_TOK), blk)]
                 + [pl.BlockSpec((Q_TOK, NA_WIDTH), kv(i)) for i in range(KV_BLOCKS)]
                 + [pl.BlockSpec((1, NA_WIDTH, Q_TOK), kv_blk(i)) for i in range(KV_BLOCKS)]
                 + [pl.BlockSpec(t2.shape, lambda u, b: (0, 0, 0, 0))],
        out_specs=pl.BlockSpec((STEP_SUBS, NA_WIDTH, Q_TOK), blk),
        out_shape=jax.ShapeDtypeStruct((t // Q_TOK, NA_WIDTH, Q_TOK), BF16),
        compiler_params=pltpu.CompilerParams(
            dimension_semantics=("arbitrary", "arbitrary"),
            vmem_limit_bytes=VMEM_LIMIT),
        name="attention",
    )(qt, *([k] * KV_BLOCKS), *([vt] * KV_BLOCKS), t2)


DFT_CHUNK = 16
LANES = 128


def _to_lane_tiles(scr, x):
    for c in range(scr.shape[0]):
        scr[c] = x[:, c * LANES:(c + 1) * LANES]


def _from_lane_tiles(scr):
    return jnp.concatenate([scr[c] for c in range(scr.shape[0])], axis=1)


def _strided_rows(scr, start, size, stride):
    return jnp.concatenate([scr[c, pl.ds(start, size, stride=stride), :] for c in range(scr.shape[0])], axis=1)


def _store_strided_rows(scr, start, stride, x):
    for c in range(scr.shape[0]):
        scr[c, pl.ds(start, x.shape[0], stride=stride), :] = x[:, c * LANES:(c + 1) * LANES]


def _fourier_1_kernel(w_ref, u_ref, a_ref, *scratch, major, chunks):
    if not scratch:
        for c in range(chunks):
            rows = slice(c * DFT_CHUNK, (c + 1) * DFT_CHUNK)
            u2d = u_ref[0, :, rows, :].reshape(major * DFT_CHUNK, F_WIDTH)
            res = jnp.dot(w_ref[...], u2d, preferred_element_type=F32)
            a_ref[0, :, :, rows, :] = res.astype(BF16).reshape(major, 2, DFT_CHUNK, F_WIDTH)
        return
    in_w, out_w = scratch
    pairs = DFT_CHUNK // 2
    _to_lane_tiles(in_w, pltpu.bitcast(u_ref[0].reshape(major * DFT_CHUNK, F_WIDTH), jnp.uint32))
    for j in range(pairs):
        rhs = pltpu.bitcast(_strided_rows(in_w, j, major, pairs), BF16)
        res = jnp.dot(w_ref[...], rhs, preferred_element_type=F32)
        _store_strided_rows(out_w, j, pairs, pltpu.bitcast(res.astype(BF16), jnp.uint32))
    a_ref[0] = pltpu.bitcast(_from_lane_tiles(out_w), BF16).reshape(major, 2, DFT_CHUNK, F_WIDTH)


def _fourier_1(u4, w1, kron):
    batch, major, minor, _ = u4.shape
    chunks = minor // DFT_CHUNK if kron else 1
    tb = chunks * DFT_CHUNK
    scratch = [] if kron else [pltpu.VMEM((F_WIDTH // LANES, major * DFT_CHUNK // 2, LANES), jnp.uint32),
                               pltpu.VMEM((F_WIDTH // LANES, major * DFT_CHUNK, LANES), jnp.uint32)]
    return pl.pallas_call(
        functools.partial(_fourier_1_kernel, major=major, chunks=chunks),
        grid=(batch, minor // tb),
        in_specs=[pl.BlockSpec(w1.shape, lambda b, j: (0, 0)),
                  pl.BlockSpec((1, major, tb, F_WIDTH), lambda b, j: (b, 0, j, 0))],
        out_specs=pl.BlockSpec((1, major, 2, tb, F_WIDTH), lambda b, j: (b, 0, 0, j, 0)),
        out_shape=jax.ShapeDtypeStruct((batch, major, 2, minor, F_WIDTH), BF16),
        scratch_shapes=scratch,
        compiler_params=pltpu.CompilerParams(
            dimension_semantics=("arbitrary", "arbitrary"),
            vmem_limit_bytes=VMEM_LIMIT),
        name="fourier_1",
    )(w1, u4)


def _fourier_2_kernel(a_ref, m_ref, wc_ref, bf_ref, y_ref, x_scr, y_scr):
    for j in range(DFT_CHUNK):
        rhs = a_ref[0, j].reshape(2 * DFT_MINOR, F_WIDTH)
        x = jnp.dot(m_ref[j], rhs, preferred_element_type=F32)
        rows = slice(j * DFT_MINOR, (j + 1) * DFT_MINOR)
        x_scr[rows, :F_WIDTH] = x[:DFT_MINOR].astype(BF16)
        x_scr[rows, F_WIDTH:] = x[DFT_MINOR:].astype(BF16)
    y = jnp.dot(x_scr[...], wc_ref[...], preferred_element_type=F32) + bf_ref[...]
    pitch = DFT_CHUNK + 1
    for j in range(DFT_CHUNK):
        _store_strided_rows(y_scr, j, pitch, y[j * DFT_MINOR:(j + 1) * DFT_MINOR])
    packed = jnp.concatenate(
        [jnp.concatenate([y_scr[c, p * pitch:p * pitch + DFT_CHUNK, :] for p in range(DFT_MINOR)], axis=0)
         for c in range(y_scr.shape[0])], axis=1)
    y_ref[0] = packed.astype(BF16).reshape(DFT_MINOR, DFT_CHUNK, F_WIDTH)


def _fold_channel_dft_kernel(cs_ref, wf_ref, wc_ref):
    wc_ref[...] = jnp.dot(cs_ref[...], wf_ref[...], preferred_element_type=F32,
                          precision=lax.Precision.HIGHEST).astype(BF16)


def _fold_channel_dft(w_f):
    c = np.arange(F_GROUP_DIM)
    ang = 2.0 * np.pi * np.outer(c, c) / F_GROUP_DIM
    eye = np.eye(F_GROUPS)
    cs = np.concatenate([np.kron(eye, np.cos(ang)), np.kron(eye, np.sin(ang))], axis=0)
    return pl.pallas_call(
        _fold_channel_dft_kernel,
        out_shape=jax.ShapeDtypeStruct((2 * F_WIDTH, F_WIDTH), BF16),
        name="fold_channel_dft",
    )(jnp.asarray(cs, F32), w_f.astype(F32))


def _fourier_2(a5, m_tab, w_c, b_f):
    batch, major, _, minor, _ = a5.shape
    tr = DFT_CHUNK
    return pl.pallas_call(
        _fourier_2_kernel,
        grid=(batch, major // tr),
        in_specs=[pl.BlockSpec((1, tr, 2, minor, F_WIDTH), lambda b, r: (b, r, 0, 0, 0)),
                  pl.BlockSpec((tr, 2 * minor, 2 * minor), lambda b, r: (r, 0, 0)),
                  pl.BlockSpec(w_c.shape, lambda b, r: (0, 0)),
                  pl.BlockSpec((1, F_WIDTH), lambda b, r: (0, 0))],
        out_specs=pl.BlockSpec((1, minor, tr, F_WIDTH), lambda b, r: (b, 0, r, 0)),
        out_shape=jax.ShapeDtypeStruct((batch, minor, major, F_WIDTH), BF16),
        scratch_shapes=[pltpu.VMEM((tr * minor, 2 * F_WIDTH), BF16),
                        pltpu.VMEM((F_WIDTH // LANES, minor * (tr + 1), LANES), F32)],
        compiler_params=pltpu.CompilerParams(
            dimension_semantics=("arbitrary", "arbitrary"),
            vmem_limit_bytes=VMEM_LIMIT),
        name="fourier_2",
    )(a5, m_tab, w_c, b_f)


@functools.lru_cache(maxsize=None)
def _dft_constants(seq):
    major = seq // DFT_MINOR
    kron = major * DFT_CHUNK <= 256
    a = np.arange(major)
    ang1 = 2.0 * np.pi * np.outer(a, a) / major
    w1 = np.stack([np.cos(ang1), -np.sin(ang1)], axis=1).reshape(2 * major, major)
    w1 = np.kron(w1, np.eye(DFT_CHUNK if kron else 2))
    r = np.arange(major)[:, None, None]
    p = np.arange(DFT_MINOR)[None, :, None]
    b = np.arange(DFT_MINOR)[None, None, :]
    ang2 = 2.0 * np.pi * ((b * (r + major * p)) % seq) / seq
    scale = 1.0 / np.sqrt(seq * F_GROUP_DIM)
    e_re, e_im = np.cos(ang2) * scale, -np.sin(ang2) * scale
    m_tab = np.concatenate([np.concatenate([e_re, -e_im], axis=2),
                            np.concatenate([e_im, e_re], axis=2)], axis=1)
    return kron, np.asarray(w1, np.float32), np.asarray(m_tab, np.float32)


def _fourier(u, w_c, b_f, batch, seq):
    major = seq // DFT_MINOR
    kron, w1, m_tab = _dft_constants(seq)
    a = _fourier_1(u.reshape(batch, major, DFT_MINOR, F_WIDTH), jnp.asarray(w1).astype(BF16), kron)
    y = _fourier_2(a, jnp.asarray(m_tab).astype(BF16), w_c, b_f)
    return y.reshape(batch * seq, F_WIDTH)


def _out_proj_kernel(ot_ref, zat_ref, yf_ref, zf_ref, x_ref, wa_ref, wf_ref,
                     gna_ref, gf_ref, gp_ref, o_ref):
    y_t = jnp.concatenate([ot_ref[i] for i in range(ot_ref.shape[0])], axis=1).astype(F32)
    inv = lax.rsqrt(jnp.mean(y_t * y_t, axis=0, keepdims=True) + RMS_EPS)
    mixed_at = _gate(y_t * inv * gna_ref[...], zat_ref[0])
    mixed_f = _gate(_rms(yf_ref[...].astype(F32), gf_ref[...]), zf_ref[...])
    out = lax.dot_general(mixed_at, wa_ref[...], (((0,), (0,)), ((), ())),
                          preferred_element_type=F32)
    out = out + jnp.dot(mixed_f, wf_ref[...], preferred_element_type=F32)
    o_ref[...] = x_ref[...] + _rms(out, gp_ref[...])


def _out_proj(o_t, za_t, y_f, z_f, x2d, w_a, w_fo, g_na_col, g_f, g_post, tm=TOKEN_BLOCK):
    t = x2d.shape[0]
    tok = lambda i: (i, 0)
    blk = lambda i: (i, 0, 0)
    const = lambda i: (0, 0)
    return pl.pallas_call(
        _out_proj_kernel,
        grid=(t // tm,),
        in_specs=[pl.BlockSpec((tm // Q_TOK, NA_WIDTH, Q_TOK), blk),
                  pl.BlockSpec((1, NA_WIDTH, tm), blk),
                  pl.BlockSpec((tm, F_WIDTH), tok),
                  pl.BlockSpec((tm, F_WIDTH), tok),
                  pl.BlockSpec((tm, D_MODEL), tok),
                  pl.BlockSpec(w_a.shape, const),
                  pl.BlockSpec(w_fo.shape, const),
                  pl.BlockSpec((NA_WIDTH, 1), const),
                  pl.BlockSpec((1, F_WIDTH), const),
                  pl.BlockSpec((1, D_MODEL), const)],
        out_specs=pl.BlockSpec((tm, D_MODEL), tok),
        out_shape=jax.ShapeDtypeStruct((t, D_MODEL), F32),
        compiler_params=pltpu.CompilerParams(
            dimension_semantics=("arbitrary",), vmem_limit_bytes=VMEM_LIMIT),
        name="out_proj",
    )(o_t, za_t, y_f, z_f, x2d, w_a, w_fo, g_na_col, g_f, g_post)


def _encoder_layer(x, w_nat, w_tr, bias, w_c, b_f, g_pre, g_na_col, g_f, w_oa, w_of, g_post):
    batch, seq, _ = x.shape
    x2d = x.reshape(batch * seq, D_MODEL)
    k, u_f, z_f, q_t, v_t, za_t = _in_proj(x2d, g_pre, w_nat, w_tr)
    o_t = _attention(q_t, k, v_t, bias, batch, seq)
    y_f = _fourier(u_f, w_c, b_f, batch, seq)
    out = _out_proj(o_t, za_t, y_f, z_f, x2d, w_oa, w_of, g_na_col, g_f, g_post)
    return out.reshape(batch, seq, D_MODEL)


def kernel(x_prompt, x_sample, w_in, rpb, w_fourier, b_fourier, g_pre, g_na, g_f, w_out, g_post):
    depth = w_in.shape[0]
    y_prompt, y_sample = x_prompt, x_sample
    scale = NA_HEAD_DIM ** -0.5
    for l in range(depth):
        w = w_in[l]
        w_nat = jnp.concatenate([w[:, i * NA_WIDTH:(i + 1) * NA_WIDTH] for i in (1, 4, 5)],
                                axis=1).astype(BF16)
        w_tr = _transposed_weights(w, (0, 2, 3), scale)
        bias = _bias_table(rpb[l])
        row = lambda v: v.reshape(1, -1).astype(F32)
        args = (w_nat, w_tr, bias, _fold_channel_dft(w_fourier[l]), row(b_fourier[l]),
                row(g_pre[l]), g_na[l].reshape(-1, 1).astype(F32), row(g_f[l]),
                w_out[l][:NA_WIDTH].astype(BF16), w_out[l][NA_WIDTH:].astype(BF16), row(g_post[l]))
        y_prompt = _encoder_layer(y_prompt, *args)
        y_sample = _encoder_layer(y_sample, *args)
    return (y_prompt, y_sample)
```

```python
import functools

import numpy as np
import jax
import jax.numpy as jnp
from jax import lax
from jax.experimental import pallas as pl
from jax.experimental.pallas import tpu as pltpu

D_MODEL = 1024
GRID_W = 64
WIN_ROWS = 8
WIN_COLS = 16
NA_HEADS = 8
NA_HEAD_DIM = 64
NA_WIDTH = NA_HEADS * NA_HEAD_DIM
F_GROUPS = 8
F_GROUP_DIM = 64
F_WIDTH = F_GROUPS * F_GROUP_DIM
RMS_EPS = 1e-6
NEG_INF = -1e30

Q_ROWS = 4
Q_TOK = Q_ROWS * GRID_W
STEP_SUBS = 4
KV_BLOCKS = STEP_SUBS + 2
KEY_ROWS = 3 * Q_ROWS
ROW_SLOTS = 2 * WIN_ROWS
DFT_MINOR = 128
TOKEN_BLOCK = 1024
FUSED_TOKEN_BLOCK = 512
VMEM_LIMIT = 56 * 1024 * 1024

BF16 = jnp.bfloat16
F32 = jnp.float32


def _rms(x, g):
    inv = lax.rsqrt(jnp.mean(x * x, axis=-1, keepdims=True) + RMS_EPS)
    return x * inv * g


def _silu(z):
    return z * (1.0 / (1.0 + jnp.exp(-z)))


def _gate(y_normed, z):
    return y_normed.astype(BF16) * _silu(z)


def _in_proj_kernel(x_ref, g_ref, wn_ref, wt_ref,
                    k_ref, uf_ref, zf_ref, qt_ref, vt_ref, zat_ref):
    h = _rms(x_ref[...], g_ref[...]).astype(BF16)
    nat = jnp.dot(h, wn_ref[...], preferred_element_type=F32)
    k_ref[...] = nat[:, 0 * NA_WIDTH:1 * NA_WIDTH].astype(BF16)
    uf_ref[...] = nat[:, 1 * NA_WIDTH:2 * NA_WIDTH].astype(BF16)
    zf_ref[...] = nat[:, 2 * NA_WIDTH:3 * NA_WIDTH].astype(BF16)
    tr = lax.dot_general(wt_ref[...], h, (((1,), (1,)), ((), ())),
                         preferred_element_type=F32)
    for i in range(qt_ref.shape[0]):
        tok = slice(i * Q_TOK, (i + 1) * Q_TOK)
        qt_ref[i] = tr[0 * NA_WIDTH:1 * NA_WIDTH, tok].astype(BF16)
        vt_ref[i] = tr[1 * NA_WIDTH:2 * NA_WIDTH, tok].astype(BF16)
        zat_ref[i] = tr[2 * NA_WIDTH:3 * NA_WIDTH, tok].astype(BF16)


def _transposed_weights_kernel(col_ref, w_ref, o_ref, *, scale):
    del col_ref
    factor = jnp.where(pl.program_id(0) == 0, scale, 1.0).astype(F32)
    o_ref[...] = (w_ref[...] * factor).T.astype(BF16)


def _transposed_weights(w, groups, scale):
    d, width = w.shape[0], NA_WIDTH
    col = jnp.asarray(groups, jnp.int32)
    return pl.pallas_call(
        functools.partial(_transposed_weights_kernel, scale=scale),
        grid_spec=pltpu.PrefetchScalarGridSpec(
            num_scalar_prefetch=1,
            grid=(len(groups),),
            in_specs=[pl.BlockSpec((d, width), lambda i, col: (0, col[i]))],
            out_specs=pl.BlockSpec((width, d), lambda i, col: (i, 0))),
        out_shape=jax.ShapeDtypeStruct((len(groups) * width, d), BF16),
        name="transposed_weights",
    )(col, w)


N_IN_PROJ_INPUTS = 4
N_IN_PROJ_OUTPUTS = 6


def _in_proj_specs(t, tm, w_nat, w_tr):
    tok = lambda i: (i, 0)
    blk = lambda i: (i, 0, 0)
    const = lambda i: (0, 0)
    nat_shape = jax.ShapeDtypeStruct((t, NA_WIDTH), BF16)
    tr_shape = jax.ShapeDtypeStruct((t // Q_TOK, NA_WIDTH, Q_TOK), BF16)
    in_specs = [pl.BlockSpec((tm, D_MODEL), tok),
                pl.BlockSpec((1, D_MODEL), const),
                pl.BlockSpec(w_nat.shape, const),
                pl.BlockSpec(w_tr.shape, const)]
    out_specs = ([pl.BlockSpec((tm, NA_WIDTH), tok)] * 3
                 + [pl.BlockSpec((tm // Q_TOK, NA_WIDTH, Q_TOK), blk)] * 3)
    return in_specs, out_specs, [nat_shape] * 3 + [tr_shape] * 3


def _in_proj(x2d, g_pre, w_nat, w_tr, tm=TOKEN_BLOCK):
    t = x2d.shape[0]
    in_specs, out_specs, out_shape = _in_proj_specs(t, tm, w_nat, w_tr)
    return pl.pallas_call(
        _in_proj_kernel,
        grid=(t // tm,),
        in_specs=in_specs,
        out_specs=out_specs,
        out_shape=out_shape,
        compiler_params=pltpu.CompilerParams(
            dimension_semantics=("arbitrary",), vmem_limit_bytes=VMEM_LIMIT),
        name="in_proj",
    )(x2d, g_pre, w_nat, w_tr)


def _bias_table_kernel(rpb_ref, t2_ref):
    head = pl.program_id(0)
    shape = (GRID_W, 2 * GRID_W)
    kc = lax.broadcasted_iota(jnp.int32, shape, 0)
    lane = lax.broadcasted_iota(jnp.int32, shape, 1)
    second = lane >= GRID_W
    qc = jnp.where(second, lane - GRID_W, lane)
    col_off = kc - qc + (WIN_COLS - 1)
    win_start = jnp.clip(qc - WIN_COLS // 2, 0, GRID_W - WIN_COLS)
    col_valid = (kc >= win_start) & (kc < win_start + WIN_COLS)
    n_off = 2 * WIN_COLS - 1
    n_row = 2 * WIN_ROWS - 1

    def entry(d, o):
        if 0 <= d < n_row:
            return rpb_ref[(head * n_row + d) * n_off + o]
        return jnp.float32(NEG_INF)

    for d in range(ROW_SLOTS):
        tile = jnp.full(shape, NEG_INF, F32)
        for o in range(n_off):
            val = jnp.where(second, entry(d - 1, o), entry(d, o))
            tile = jnp.where(col_off == o, val, tile)
        t2_ref[0, d] = jnp.where(col_valid, tile, NEG_INF)


def _bias_table(rpb):
    return pl.pallas_call(
        _bias_table_kernel,
        grid=(NA_HEADS,),
        in_specs=[pl.BlockSpec(memory_space=pltpu.SMEM)],
        out_specs=pl.BlockSpec((1, ROW_SLOTS, GRID_W, 2 * GRID_W), lambda h: (h, 0, 0, 0)),
        out_shape=jax.ShapeDtypeStruct((NA_HEADS, ROW_SLOTS, GRID_W, 2 * GRID_W), F32),
        compiler_params=pltpu.CompilerParams(dimension_semantics=("arbitrary",)),
        name="bias_table",
    )(rpb.astype(F32).reshape(-1))


def _token_range(refs, lo, hi, other, axis):
    pieces = []
    for idx, ref in enumerate(refs):
        a, b = max(lo, idx * Q_TOK), min(hi, (idx + 1) * Q_TOK)
        if a < b:
            tok = slice(a - idx * Q_TOK, b - idx * Q_TOK)
            pieces.append(ref[tok, other] if axis == 0 else ref[0, other, tok])
    return pieces[0] if len(pieces) == 1 else jnp.concatenate(pieces, axis=axis)


def _window_plan(variant, jp):
    if variant == 0:
        return [(i, i - 2 * jp + WIN_ROWS - 1, None) for i in range(WIN_ROWS)]
    if variant == 2:
        q_row = KEY_ROWS - Q_ROWS + 2 * jp
        return [(i, i - q_row + WIN_ROWS - 1, None) for i in range(KEY_ROWS - WIN_ROWS, KEY_ROWS)]
    q_row = Q_ROWS + 2 * jp
    plan = []
    for i in range(q_row - WIN_ROWS // 2, q_row + WIN_ROWS // 2 + 1):
        half = "low" if i == q_row - WIN_ROWS // 2 else "high" if i == q_row + WIN_ROWS // 2 else None
        plan.append((i, i - q_row + WIN_ROWS - 1, half))
    return plan


def _step_plan(position):
    plan = []
    for j in range(STEP_SUBS):
        if position == 0:
            plan.append((0, 0) if j == 0 else (1, (j - 1) * Q_ROWS))
        elif position == 1:
            plan.append((1, j * Q_ROWS))
        else:
            last = j == STEP_SUBS - 1
            plan.append((2 if last else 1, (j + 1 - last) * Q_ROWS))
    return tuple(plan)


def _attention_kernel(qt_ref, *refs, steps):
    k_refs, v_refs = refs[:KV_BLOCKS], refs[KV_BLOCKS:2 * KV_BLOCKS]
    t2_ref, out_ref = refs[2 * KV_BLOCKS:]
    step = pl.program_id(0)
    pair_lanes = 2 * NA_HEAD_DIM
    q_lanes = 2 * GRID_W
    n_pairs = NA_HEADS // 2

    def sub_group(sub, variant, key_off):
        plans = [_window_plan(variant, jp) for jp in range(Q_ROWS // 2)]
        key_lo = min(p[0][0] for p in plans)
        key_top = max(p[-1][0] for p in plans) + 1
        key_hi = key_top + (key_top - key_lo) % 2
        tok_lo, tok_top, tok_hi = ((key_off + r) * GRID_W for r in (key_lo, key_top, key_hi))
        lane = lax.broadcasted_iota(jnp.int32, (1, q_lanes), 1)
        half_mask = {"low": jnp.where(lane < GRID_W, 0.0, NEG_INF).astype(F32),
                     "high": jnp.where(lane >= GRID_W, 0.0, NEG_INF).astype(F32)}
        ones_rows = jnp.ones((16, tok_hi - tok_lo), BF16)
        zero_blk = jnp.zeros((GRID_W, q_lanes), BF16)

        def scores(pair):
            cols = slice(pair * pair_lanes, (pair + 1) * pair_lanes)
            k_pair = _token_range(k_refs, tok_lo, tok_top, cols, 0)
            qt_pair = qt_ref[sub, cols, :]
            zero = jnp.zeros((NA_HEAD_DIM, Q_TOK), BF16)
            qt_both = jnp.concatenate(
                [jnp.concatenate([qt_pair[:NA_HEAD_DIM], zero], axis=0),
                 jnp.concatenate([zero, qt_pair[NA_HEAD_DIM:]], axis=0)], axis=1)
            s = jnp.dot(k_pair, qt_both, preferred_element_type=F32)
            tiles = [(head_sub, jp) for head_sub in range(2) for jp in range(len(plans))]
            m_acc = [None] * len(tiles)
            biased = [{} for _ in tiles]
            for i in range(key_lo, key_top):
                rows = slice((i - key_lo) * GRID_W, (i - key_lo + 1) * GRID_W)
                for t, (head_sub, jp) in enumerate(tiles):
                    entry = [e for e in plans[jp] if e[0] == i]
                    if not entry:
                        continue
                    _, slot_d, half = entry[0]
                    first = head_sub * Q_TOK + jp * q_lanes
                    lanes = slice(first, first + q_lanes)
                    blk = s[rows, lanes] + t2_ref[2 * pair + head_sub, slot_d]
                    if half is not None:
                        blk = blk + half_mask[half]
                    biased[t][i] = blk
                    blk_max = jnp.max(blk.reshape(GRID_W // 8, 8, q_lanes), axis=0)
                    m_acc[t] = blk_max if m_acc[t] is None else jnp.maximum(m_acc[t], blk_max)
            maxima = [jnp.max(m, axis=0, keepdims=True) for m in m_acc]
            return maxima, biased

        def softmax_pv(pair, state):
            maxima, biased = state
            for head_sub in range(2):
                head = 2 * pair + head_sub
                cols = []
                for jp in range(len(plans)):
                    t = 2 * head_sub + jp
                    blocks = [jnp.exp(biased[t][i] - maxima[t]).astype(BF16) if i in biased[t] else zero_blk
                              for i in range(key_lo, key_hi)]
                    cols.append(jnp.concatenate(blocks, axis=0))
                p_head = jnp.concatenate(cols, axis=1)
                rows = slice(head * NA_HEAD_DIM, (head + 1) * NA_HEAD_DIM)
                vt_head = _token_range(v_refs, tok_lo, tok_hi, rows, 1)
                vt_ext = jnp.concatenate([vt_head, ones_rows], axis=0)
                o = jnp.dot(vt_ext, p_head, preferred_element_type=F32)
                out_ref[sub, rows, :] = (o[:NA_HEAD_DIM] * (1.0 / o[NA_HEAD_DIM:NA_HEAD_DIM + 1])).astype(BF16)

        return scores, softmax_pv

    def body(step_variant):
        fns = [sub_group(sub, *plan) for sub, plan in enumerate(_step_plan(step_variant))]
        stages = [(sub, pair) for sub in range(len(fns)) for pair in range(n_pairs)]
        state = fns[0][0](0)
        for idx, (sub, pair) in enumerate(stages):
            nxt = None
            if idx + 1 < len(stages):
                nxt_sub, nxt_pair = stages[idx + 1]
                nxt = fns[nxt_sub][0](nxt_pair)
            fns[sub][1](pair, state)
            state = nxt

    is_top = step == 0
    is_bot = step == steps - 1
    pl.when(is_top)(lambda: body(0))
    pl.when(jnp.logical_not(is_top | is_bot))(lambda: body(1))
    pl.when(is_bot)(lambda: body(2))


def _attention(qt, k, vt, t2, batch, seq):
    t = batch * seq
    blocks = seq // Q_TOK
    steps = blocks // STEP_SUBS
    assert steps >= 2 and blocks >= KV_BLOCKS

    def first_kv(u):
        return jnp.clip(STEP_SUBS * u - 1, 0, blocks - KV_BLOCKS)

    def blk(u, b):
        return (b * steps + u, 0, 0)

    def kv(i):
        return lambda u, b: (b * blocks + first_kv(u) + i, 0)

    def kv_blk(i):
        return lambda u, b: (b * blocks + first_kv(u) + i, 0, 0)

    return pl.pallas_call(
        functools.partial(_attention_kernel, steps=steps),
        grid=(steps, batch),
        in_specs=[pl.BlockSpec((STEP_SUBS, NA_WIDTH, Q_TOK), blk)]
                 + [pl.BlockSpec((Q_TOK, NA_WIDTH), kv(i)) for i in range(KV_BLOCKS)]
                 + [pl.BlockSpec((1, NA_WIDTH, Q_TOK), kv_blk(i)) for i in range(KV_BLOCKS)]
                 + [pl.BlockSpec(t2.shape, lambda u, b: (0, 0, 0, 0))],
        out_specs=pl.BlockSpec((STEP_SUBS, NA_WIDTH, Q_TOK), blk),
        out_shape=jax.ShapeDtypeStruct((t // Q_TOK, NA_WIDTH, Q_TOK), BF16),
        compiler_params=pltpu.CompilerParams(
            dimension_semantics=("arbitrary", "arbitrary"),
            vmem_limit_bytes=VMEM_LIMIT),
        name="attention",
    )(qt, *([k] * KV_BLOCKS), *([vt] * KV_BLOCKS), t2)


DFT_CHUNK = 16
LANES = 128


def _to_lane_tiles(scr, x):
    for c in range(scr.shape[0]):
        scr[c] = x[:, c * LANES:(c + 1) * LANES]


def _from_lane_tiles(scr):
    return jnp.concatenate([scr[c] for c in range(scr.shape[0])], axis=1)


def _strided_rows(scr, start, size, stride):
    return jnp.concatenate([scr[c, pl.ds(start, size, stride=stride), :] for c in range(scr.shape[0])], axis=1)


def _store_strided_rows(scr, start, stride, x):
    for c in range(scr.shape[0]):
        scr[c, pl.ds(start, x.shape[0], stride=stride), :] = x[:, c * LANES:(c + 1) * LANES]


def _fourier_1_kernel(w_ref, u_ref, a_ref, *scratch, major, chunks):
    if not scratch:
        for c in range(chunks):
            rows = slice(c * DFT_CHUNK, (c + 1) * DFT_CHUNK)
            u2d = u_ref[0, :, rows, :].reshape(major * DFT_CHUNK, F_WIDTH)
            res = jnp.dot(w_ref[...], u2d, preferred_element_type=F32)
            a_ref[0, :, :, rows, :] = res.astype(BF16).reshape(major, 2, DFT_CHUNK, F_WIDTH)
        return
    in_w, out_w = scratch
    pairs = DFT_CHUNK // 2
    _to_lane_tiles(in_w, pltpu.bitcast(u_ref[0].reshape(major * DFT_CHUNK, F_WIDTH), jnp.uint32))
    for j in range(pairs):
        rhs = pltpu.bitcast(_strided_rows(in_w, j, major, pairs), BF16)
        res = jnp.dot(w_ref[...], rhs, preferred_element_type=F32)
        _store_strided_rows(out_w, j, pairs, pltpu.bitcast(res.astype(BF16), jnp.uint32))
    a_ref[0] = pltpu.bitcast(_from_lane_tiles(out_w), BF16).reshape(major, 2, DFT_CHUNK, F_WIDTH)


def _fourier_1(u4, w1, kron):
    batch, major, minor, _ = u4.shape
    chunks = minor // DFT_CHUNK if kron else 1
    tb = chunks * DFT_CHUNK
    scratch = [] if kron else [pltpu.VMEM((F_WIDTH // LANES, major * DFT_CHUNK // 2, LANES), jnp.uint32),
                               pltpu.VMEM((F_WIDTH // LANES, major * DFT_CHUNK, LANES), jnp.uint32)]
    return pl.pallas_call(
        functools.partial(_fourier_1_kernel, major=major, chunks=chunks),
        grid=(batch, minor // tb),
        in_specs=[pl.BlockSpec(w1.shape, lambda b, j: (0, 0)),
                  pl.BlockSpec((1, major, tb, F_WIDTH), lambda b, j: (b, 0, j, 0))],
        out_specs=pl.BlockSpec((1, major, 2, tb, F_WIDTH), lambda b, j: (b, 0, 0, j, 0)),
        out_shape=jax.ShapeDtypeStruct((batch, major, 2, minor, F_WIDTH), BF16),
        scratch_shapes=scratch,
        compiler_params=pltpu.CompilerParams(
            dimension_semantics=("arbitrary", "arbitrary"),
            vmem_limit_bytes=VMEM_LIMIT),
        name="fourier_1",
    )(w1, u4)


def _fourier_2_kernel(a_ref, m_ref, wc_ref, bf_ref, y_ref, x_scr, y_scr):
    for j in range(DFT_CHUNK):
        rhs = a_ref[0, j].reshape(2 * DFT_MINOR, F_WIDTH)
        x = jnp.dot(m_ref[j], rhs, preferred_element_type=F32)
        rows = slice(j * DFT_MINOR, (j + 1) * DFT_MINOR)
        x_scr[rows, :F_WIDTH] = x[:DFT_MINOR].astype(BF16)
        x_scr[rows, F_WIDTH:] = x[DFT_MINOR:].astype(BF16)
    y = jnp.dot(x_scr[...], wc_ref[...], preferred_element_type=F32) + bf_ref[...]
    pitch = DFT_CHUNK + 1
    for j in range(DFT_CHUNK):
        _store_strided_rows(y_scr, j, pitch, y[j * DFT_MINOR:(j + 1) * DFT_MINOR])
    packed = jnp.concatenate(
        [jnp.concatenate([y_scr[c, p * pitch:p * pitch + DFT_CHUNK, :] for p in range(DFT_MINOR)], axis=0)
         for c in range(y_scr.shape[0])], axis=1)
    y_ref[0] = packed.astype(BF16).reshape(DFT_MINOR, DFT_CHUNK, F_WIDTH)


def _fold_channel_dft_kernel(cs_ref, wf_ref, wc_ref):
    wc_ref[...] = jnp.dot(cs_ref[...], wf_ref[...], preferred_element_type=F32,
                          precision=lax.Precision.HIGHEST).astype(BF16)


def _fold_channel_dft(w_f):
    c = np.arange(F_GROUP_DIM)
    ang = 2.0 * np.pi * np.outer(c, c) / F_GROUP_DIM
    eye = np.eye(F_GROUPS)
    cs = np.concatenate([np.kron(eye, np.cos(ang)), np.kron(eye, np.sin(ang))], axis=0)
    return pl.pallas_call(
        _fold_channel_dft_kernel,
        out_shape=jax.ShapeDtypeStruct((2 * F_WIDTH, F_WIDTH), BF16),
        name="fold_channel_dft",
    )(jnp.asarray(cs, F32), w_f.astype(F32))


def _fourier_2(a5, m_tab, w_c, b_f):
    batch, major, _, minor, _ = a5.shape
    tr = DFT_CHUNK
    return pl.pallas_call(
        _fourier_2_kernel,
        grid=(batch, major // tr),
        in_specs=[pl.BlockSpec((1, tr, 2, minor, F_WIDTH), lambda b, r: (b, r, 0, 0, 0)),
                  pl.BlockSpec((tr, 2 * minor, 2 * minor), lambda b, r: (r, 0, 0)),
                  pl.BlockSpec(w_c.shape, lambda b, r: (0, 0)),
                  pl.BlockSpec((1, F_WIDTH), lambda b, r: (0, 0))],
        out_specs=pl.BlockSpec((1, minor, tr, F_WIDTH), lambda b, r: (b, 0, r, 0)),
        out_shape=jax.ShapeDtypeStruct((batch, minor, major, F_WIDTH), BF16),
        scratch_shapes=[pltpu.VMEM((tr * minor, 2 * F_WIDTH), BF16),
                        pltpu.VMEM((F_WIDTH // LANES, minor * (tr + 1), LANES), F32)],
        compiler_params=pltpu.CompilerParams(
            dimension_semantics=("arbitrary", "arbitrary"),
            vmem_limit_bytes=VMEM_LIMIT),
        name="fourier_2",
    )(a5, m_tab, w_c, b_f)


@functools.lru_cache(maxsize=None)
def _dft_constants(seq):
    major = seq // DFT_MINOR
    kron = major * DFT_CHUNK <= 256
    a = np.arange(major)
    ang1 = 2.0 * np.pi * np.outer(a, a) / major
    w1 = np.stack([np.cos(ang1), -np.sin(ang1)], axis=1).reshape(2 * major, major)
    w1 = np.kron(w1, np.eye(DFT_CHUNK if kron else 2))
    r = np.arange(major)[:, None, None]
    p = np.arange(DFT_MINOR)[None, :, None]
    b = np.arange(DFT_MINOR)[None, None, :]
    ang2 = 2.0 * np.pi * ((b * (r + major * p)) % seq) / seq
    scale = 1.0 / np.sqrt(seq * F_GROUP_DIM)
    e_re, e_im = np.cos(ang2) * scale, -np.sin(ang2) * scale
    m_tab = np.concatenate([np.concatenate([e_re, -e_im], axis=2),
                            np.concatenate([e_im, e_re], axis=2)], axis=1)
    return kron, np.asarray(w1, np.float32), np.asarray(m_tab, np.float32)


def _fourier(u, w_c, b_f, batch, seq):
    major = seq // DFT_MINOR
    kron, w1, m_tab = _dft_constants(seq)
    a = _fourier_1(u.reshape(batch, major, DFT_MINOR, F_WIDTH), jnp.asarray(w1).astype(BF16), kron)
    y = _fourier_2(a, jnp.asarray(m_tab).astype(BF16), w_c, b_f)
    return y.reshape(batch * seq, F_WIDTH)


def _out_proj_kernel(ot_ref, zat_ref, yf_ref, zf_ref, x_ref, wa_ref, wf_ref,
                     gna_ref, gf_ref, gp_ref, o_ref):
    blocks = range(ot_ref.shape[0])
    y_t = jnp.concatenate([ot_ref[i] for i in blocks], axis=1).astype(F32)
    z_t = jnp.concatenate([zat_ref[i] for i in blocks], axis=1)
    inv = lax.rsqrt(jnp.mean(y_t * y_t, axis=0, keepdims=True) + RMS_EPS)
    mixed_at = _gate(y_t * inv * gna_ref[...], z_t)
    mixed_f = _gate(_rms(yf_ref[...].astype(F32), gf_ref[...]), zf_ref[...])
    out = lax.dot_general(mixed_at, wa_ref[...], (((0,), (0,)), ((), ())),
                          preferred_element_type=F32)
    out = out + jnp.dot(mixed_f, wf_ref[...], preferred_element_type=F32)
    o_ref[...] = x_ref[...] + _rms(out, gp_ref[...])


def _out_proj_specs(t, tm, w_a, w_fo):
    tok = lambda i: (i, 0)
    blk = lambda i: (i, 0, 0)
    const = lambda i: (0, 0)
    in_specs = [pl.BlockSpec((tm // Q_TOK, NA_WIDTH, Q_TOK), blk),
                pl.BlockSpec((tm // Q_TOK, NA_WIDTH, Q_TOK), blk),
                pl.BlockSpec((tm, F_WIDTH), tok),
                pl.BlockSpec((tm, F_WIDTH), tok),
                pl.BlockSpec((tm, D_MODEL), tok),
                pl.BlockSpec(w_a.shape, const),
                pl.BlockSpec(w_fo.shape, const),
                pl.BlockSpec((NA_WIDTH, 1), const),
                pl.BlockSpec((1, F_WIDTH), const),
                pl.BlockSpec((1, D_MODEL), const)]
    return in_specs, pl.BlockSpec((tm, D_MODEL), tok), jax.ShapeDtypeStruct((t, D_MODEL), F32)


def _out_proj(o_t, za_t, y_f, z_f, x2d, w_a, w_fo, g_na_col, g_f, g_post, tm=TOKEN_BLOCK):
    t = x2d.shape[0]
    in_specs, out_spec, out_shape = _out_proj_specs(t, tm, w_a, w_fo)
    return pl.pallas_call(
        _out_proj_kernel,
        grid=(t // tm,),
        in_specs=in_specs,
        out_specs=out_spec,
        out_shape=out_shape,
        compiler_params=pltpu.CompilerParams(
            dimension_semantics=("arbitrary",), vmem_limit_bytes=VMEM_LIMIT),
        name="out_proj",
    )(o_t, za_t, y_f, z_f, x2d, w_a, w_fo, g_na_col, g_f, g_post)


def _in_out_proj_kernel(x_ref, g_ref, wn_ref, wt_ref,
                        ot_ref, zat_in_ref, yf_ref, zf_in_ref, xo_ref, wa_ref, wf_ref, gna_ref, gf_ref, gp_ref,
                        k_ref, uf_ref, zf_ref, qt_ref, vt_ref, zat_ref, o_ref):
    h = _rms(x_ref[...], g_ref[...]).astype(BF16)
    blocks = range(ot_ref.shape[0])
    y_t = jnp.concatenate([ot_ref[i] for i in blocks], axis=1).astype(F32)
    z_t = jnp.concatenate([zat_in_ref[i] for i in blocks], axis=1)
    nat = jnp.dot(h, wn_ref[...], preferred_element_type=F32)
    inv = lax.rsqrt(jnp.mean(y_t * y_t, axis=0, keepdims=True) + RMS_EPS)
    mixed_at = _gate(y_t * inv * gna_ref[...], z_t)
    mixed_f = _gate(_rms(yf_ref[...].astype(F32), gf_ref[...]), zf_in_ref[...])
    k_ref[...] = nat[:, 0 * NA_WIDTH:1 * NA_WIDTH].astype(BF16)
    uf_ref[...] = nat[:, 1 * NA_WIDTH:2 * NA_WIDTH].astype(BF16)
    zf_ref[...] = nat[:, 2 * NA_WIDTH:3 * NA_WIDTH].astype(BF16)
    out = lax.dot_general(mixed_at, wa_ref[...], (((0,), (0,)), ((), ())), preferred_element_type=F32)
    out = out + jnp.dot(mixed_f, wf_ref[...], preferred_element_type=F32)
    tr = lax.dot_general(wt_ref[...], h, (((1,), (1,)), ((), ())), preferred_element_type=F32)
    o_ref[...] = xo_ref[...] + _rms(out, gp_ref[...])
    for i in range(qt_ref.shape[0]):
        tok = slice(i * Q_TOK, (i + 1) * Q_TOK)
        qt_ref[i] = tr[0 * NA_WIDTH:1 * NA_WIDTH, tok].astype(BF16)
        vt_ref[i] = tr[1 * NA_WIDTH:2 * NA_WIDTH, tok].astype(BF16)
        zat_ref[i] = tr[2 * NA_WIDTH:3 * NA_WIDTH, tok].astype(BF16)


def _in_out_proj(in_args, out_args, tm=FUSED_TOKEN_BLOCK):
    x_in, _, w_nat, w_tr = in_args
    x_out, w_a, w_fo = out_args[4], out_args[5], out_args[6]
    t = x_in.shape[0]
    assert x_out.shape[0] == t
    in_specs_a, out_specs_a, out_shape_a = _in_proj_specs(t, tm, w_nat, w_tr)
    in_specs_b, out_spec_b, out_shape_b = _out_proj_specs(t, tm, w_a, w_fo)
    res = pl.pallas_call(
        _in_out_proj_kernel,
        grid=(t // tm,),
        in_specs=in_specs_a + in_specs_b,
        out_specs=out_specs_a + [out_spec_b],
        out_shape=out_shape_a + [out_shape_b],
        compiler_params=pltpu.CompilerParams(
            dimension_semantics=("arbitrary",), vmem_limit_bytes=VMEM_LIMIT),
        name="in_out_proj",
    )(*in_args, *out_args)
    return res[:N_IN_PROJ_OUTPUTS], res[-1]


def _mixers(proj, bias, w_c, b_f, batch, seq):
    k, u_f, z_f, q_t, v_t, za_t = proj
    o_t = _attention(q_t, k, v_t, bias, batch, seq)
    y_f = _fourier(u_f, w_c, b_f, batch, seq)
    return o_t, za_t, y_f, z_f


def kernel(x_prompt, x_sample, w_in, rpb, w_fourier, b_fourier, g_pre, g_na, g_f, w_out, g_post):
    depth = w_in.shape[0]
    y_prompt, y_sample = x_prompt, x_sample
    scale = NA_HEAD_DIM ** -0.5
    for l in range(depth):
        w = w_in[l]
        w_nat = jnp.concatenate([w[:, i * NA_WIDTH:(i + 1) * NA_WIDTH] for i in (1, 4, 5)],
                                axis=1).astype(BF16)
        w_tr = _transposed_weights(w, (0, 2, 3), scale)
        bias = _bias_table(rpb[l])
        w_c = _fold_channel_dft(w_fourier[l])
        row = lambda v: v.reshape(1, -1).astype(F32)
        b_f, g_in = row(b_fourier[l]), row(g_pre[l])
        out_w = (w_out[l][:NA_WIDTH].astype(BF16), w_out[l][NA_WIDTH:].astype(BF16),
                 g_na[l].reshape(-1, 1).astype(F32), row(g_f[l]), row(g_post[l]))
        (bp, sp, _), (bs, ss, _) = y_prompt.shape, y_sample.shape
        xp, xs = y_prompt.reshape(bp * sp, D_MODEL), y_sample.reshape(bs * ss, D_MODEL)

        mixed_p = _mixers(_in_proj(xp, g_in, w_nat, w_tr), bias, w_c, b_f, bp, sp)
        if xp.shape[0] == xs.shape[0]:
            proj_s, out_p = _in_out_proj((xs, g_in, w_nat, w_tr), (*mixed_p, xp, *out_w))
        else:
            proj_s = _in_proj(xs, g_in, w_nat, w_tr)
            out_p = _out_proj(*mixed_p, xp, *out_w)
        mixed_s = _mixers(proj_s, bias, w_c, b_f, bs, ss)
        out_s = _out_proj(*mixed_s, xs, *out_w)
        y_prompt, y_sample = out_p.reshape(bp, sp, D_MODEL), out_s.reshape(bs, ss, D_MODEL)
    return (y_prompt, y_sample)
```

```python
import functools

import numpy as np
import jax
import jax.numpy as jnp
from jax import lax
from jax.experimental import pallas as pl
from jax.experimental.pallas import tpu as pltpu

D_MODEL = 1024
GRID_W = 64
WIN_ROWS = 8
WIN_COLS = 16
NA_HEADS = 8
NA_HEAD_DIM = 64
NA_WIDTH = NA_HEADS * NA_HEAD_DIM
F_GROUPS = 8
F_GROUP_DIM = 64
F_WIDTH = F_GROUPS * F_GROUP_DIM
RMS_EPS = 1e-6
NEG_INF = -1e30

Q_ROWS = 4
Q_TOK = Q_ROWS * GRID_W
STEP_SUBS = 4
KV_BLOCKS = STEP_SUBS + 2
KEY_ROWS = 3 * Q_ROWS
ROW_SLOTS = 2 * WIN_ROWS
DFT_MINOR = 128
TOKEN_BLOCK = 1024
FUSED_TOKEN_BLOCK = 512
VMEM_LIMIT = 56 * 1024 * 1024

BF16 = jnp.bfloat16
F32 = jnp.float32


def _rms(x, g):
    inv = lax.rsqrt(jnp.mean(x * x, axis=-1, keepdims=True) + RMS_EPS)
    return x * inv * g


def _silu(z):
    return z * (1.0 / (1.0 + jnp.exp(-z)))


def _gate(y_normed, z):
    return y_normed.astype(BF16) * _silu(z)


def _in_proj_kernel(x_ref, g_ref, wn_ref, wt_ref,
                    k_ref, uf_ref, zf_ref, qt_ref, vt_ref, zat_ref):
    h = _rms(x_ref[...], g_ref[...]).astype(BF16)
    nat = jnp.dot(h, wn_ref[...], preferred_element_type=F32)
    k_ref[...] = nat[:, 0 * NA_WIDTH:1 * NA_WIDTH].astype(BF16)
    uf_ref[...] = nat[:, 1 * NA_WIDTH:2 * NA_WIDTH].astype(BF16)
    zf_ref[...] = nat[:, 2 * NA_WIDTH:3 * NA_WIDTH].astype(BF16)
    tr = lax.dot_general(wt_ref[...], h, (((1,), (1,)), ((), ())),
                         preferred_element_type=F32)
    for i in range(qt_ref.shape[0]):
        tok = slice(i * Q_TOK, (i + 1) * Q_TOK)
        qt_ref[i] = tr[0 * NA_WIDTH:1 * NA_WIDTH, tok].astype(BF16)
        vt_ref[i] = tr[1 * NA_WIDTH:2 * NA_WIDTH, tok].astype(BF16)
        zat_ref[i] = tr[2 * NA_WIDTH:3 * NA_WIDTH, tok].astype(BF16)


def _transposed_weights_kernel(col_ref, w_ref, o_ref, *, scale):
    del col_ref
    factor = jnp.where(pl.program_id(0) == 0, scale, 1.0).astype(F32)
    o_ref[...] = (w_ref[...] * factor).T.astype(BF16)


def _transposed_weights(w, groups, scale):
    d, width = w.shape[0], NA_WIDTH
    col = jnp.asarray(groups, jnp.int32)
    return pl.pallas_call(
        functools.partial(_transposed_weights_kernel, scale=scale),
        grid_spec=pltpu.PrefetchScalarGridSpec(
            num_scalar_prefetch=1,
            grid=(len(groups),),
            in_specs=[pl.BlockSpec((d, width), lambda i, col: (0, col[i]))],
            out_specs=pl.BlockSpec((width, d), lambda i, col: (i, 0))),
        out_shape=jax.ShapeDtypeStruct((len(groups) * width, d), BF16),
        name="transposed_weights",
    )(col, w)


N_IN_PROJ_INPUTS = 4
N_IN_PROJ_OUTPUTS = 6


def _in_proj_specs(t, tm, w_nat, w_tr):
    tok = lambda i: (i, 0)
    blk = lambda i: (i, 0, 0)
    const = lambda i: (0, 0)
    nat_shape = jax.ShapeDtypeStruct((t, NA_WIDTH), BF16)
    tr_shape = jax.ShapeDtypeStruct((t // Q_TOK, NA_WIDTH, Q_TOK), BF16)
    in_specs = [pl.BlockSpec((tm, D_MODEL), tok),
                pl.BlockSpec((1, D_MODEL), const),
                pl.BlockSpec(w_nat.shape, const),
                pl.BlockSpec(w_tr.shape, const)]
    out_specs = ([pl.BlockSpec((tm, NA_WIDTH), tok)] * 3
                 + [pl.BlockSpec((tm // Q_TOK, NA_WIDTH, Q_TOK), blk)] * 3)
    return in_specs, out_specs, [nat_shape] * 3 + [tr_shape] * 3


def _in_proj(x2d, g_pre, w_nat, w_tr, tm=TOKEN_BLOCK):
    t = x2d.shape[0]
    in_specs, out_specs, out_shape = _in_proj_specs(t, tm, w_nat, w_tr)
    return pl.pallas_call(
        _in_proj_kernel,
        grid=(t // tm,),
        in_specs=in_specs,
        out_specs=out_specs,
        out_shape=out_shape,
        compiler_params=pltpu.CompilerParams(
            dimension_semantics=("arbitrary",), vmem_limit_bytes=VMEM_LIMIT),
        name="in_proj",
    )(x2d, g_pre, w_nat, w_tr)


def _bias_table_kernel(rpb_ref, t2_ref):
    head = pl.program_id(0)
    shape = (GRID_W, 2 * GRID_W)
    kc = lax.broadcasted_iota(jnp.int32, shape, 0)
    lane = lax.broadcasted_iota(jnp.int32, shape, 1)
    second = lane >= GRID_W
    qc = jnp.where(second, lane - GRID_W, lane)
    col_off = kc - qc + (WIN_COLS - 1)
    win_start = jnp.clip(qc - WIN_COLS // 2, 0, GRID_W - WIN_COLS)
    col_valid = (kc >= win_start) & (kc < win_start + WIN_COLS)
    n_off = 2 * WIN_COLS - 1
    n_row = 2 * WIN_ROWS - 1

    def entry(d, o):
        if 0 <= d < n_row:
            return rpb_ref[(head * n_row + d) * n_off + o]
        return jnp.float32(NEG_INF)

    for d in range(ROW_SLOTS):
        tile = jnp.full(shape, NEG_INF, F32)
        for o in range(n_off):
            val = jnp.where(second, entry(d - 1, o), entry(d, o))
            tile = jnp.where(col_off == o, val, tile)
        t2_ref[0, d] = jnp.where(col_valid, tile, NEG_INF)


def _bias_table(rpb):
    return pl.pallas_call(
        _bias_table_kernel,
        grid=(NA_HEADS,),
        in_specs=[pl.BlockSpec(memory_space=pltpu.SMEM)],
        out_specs=pl.BlockSpec((1, ROW_SLOTS, GRID_W, 2 * GRID_W), lambda h: (h, 0, 0, 0)),
        out_shape=jax.ShapeDtypeStruct((NA_HEADS, ROW_SLOTS, GRID_W, 2 * GRID_W), F32),
        compiler_params=pltpu.CompilerParams(dimension_semantics=("arbitrary",)),
        name="bias_table",
    )(rpb.astype(F32).reshape(-1))


def _token_range(refs, lo, hi, other, axis):
    pieces = []
    for idx, ref in enumerate(refs):
        a, b = max(lo, idx * Q_TOK), min(hi, (idx + 1) * Q_TOK)
        if a < b:
            tok = slice(a - idx * Q_TOK, b - idx * Q_TOK)
            pieces.append(ref[tok, other] if axis == 0 else ref[0, other, tok])
    return pieces[0] if len(pieces) == 1 else jnp.concatenate(pieces, axis=axis)


def _window_plan(variant, jp):
    if variant == 0:
        return [(i, i - 2 * jp + WIN_ROWS - 1, None) for i in range(WIN_ROWS)]
    if variant == 2:
        q_row = KEY_ROWS - Q_ROWS + 2 * jp
        return [(i, i - q_row + WIN_ROWS - 1, None) for i in range(KEY_ROWS - WIN_ROWS, KEY_ROWS)]
    q_row = Q_ROWS + 2 * jp
    plan = []
    for i in range(q_row - WIN_ROWS // 2, q_row + WIN_ROWS // 2 + 1):
        half = "low" if i == q_row - WIN_ROWS // 2 else "high" if i == q_row + WIN_ROWS // 2 else None
        plan.append((i, i - q_row + WIN_ROWS - 1, half))
    return plan


def _step_plan(position):
    plan = []
    for j in range(STEP_SUBS):
        if position == 0:
            plan.append((0, 0) if j == 0 else (1, (j - 1) * Q_ROWS))
        elif position == 1:
            plan.append((1, j * Q_ROWS))
        else:
            last = j == STEP_SUBS - 1
            plan.append((2 if last else 1, (j + 1 - last) * Q_ROWS))
    return tuple(plan)


def _attention_kernel(qt_ref, *refs, steps):
    k_refs, v_refs = refs[:KV_BLOCKS], refs[KV_BLOCKS:2 * KV_BLOCKS]
    t2_ref, out_ref = refs[2 * KV_BLOCKS:]
    step = pl.program_id(0)
    pair_lanes = 2 * NA_HEAD_DIM
    q_lanes = 2 * GRID_W
    n_pairs = NA_HEADS // 2

    def sub_group(sub, variant, key_off):
        plans = [_window_plan(variant, jp) for jp in range(Q_ROWS // 2)]
        key_lo = min(p[0][0] for p in plans)
        key_top = max(p[-1][0] for p in plans) + 1
        key_hi = key_top + (key_top - key_lo) % 2
        tok_lo, tok_top, tok_hi = ((key_off + r) * GRID_W for r in (key_lo, key_top, key_hi))
        lane = lax.broadcasted_iota(jnp.int32, (1, q_lanes), 1)
        half_mask = {"low": jnp.where(lane < GRID_W, 0.0, NEG_INF).astype(F32),
                     "high": jnp.where(lane >= GRID_W, 0.0, NEG_INF).astype(F32)}
        ones_rows = jnp.ones((16, tok_hi - tok_lo), BF16)
        zero_blk = jnp.zeros((GRID_W, q_lanes), BF16)

        def scores(pair):
            cols = slice(pair * pair_lanes, (pair + 1) * pair_lanes)
            k_pair = _token_range(k_refs, tok_lo, tok_top, cols, 0)
            qt_pair = qt_ref[sub, cols, :]
            zero = jnp.zeros((NA_HEAD_DIM, Q_TOK), BF16)
            qt_both = jnp.concatenate(
                [jnp.concatenate([qt_pair[:NA_HEAD_DIM], zero], axis=0),
                 jnp.concatenate([zero, qt_pair[NA_HEAD_DIM:]], axis=0)], axis=1)
            s = jnp.dot(k_pair, qt_both, preferred_element_type=F32)
            tiles = [(head_sub, jp) for head_sub in range(2) for jp in range(len(plans))]
            m_acc = [None] * len(tiles)
            biased = [{} for _ in tiles]
            for i in range(key_lo, key_top):
                rows = slice((i - key_lo) * GRID_W, (i - key_lo + 1) * GRID_W)
                for t, (head_sub, jp) in enumerate(tiles):
                    entry = [e for e in plans[jp] if e[0] == i]
                    if not entry:
                        continue
                    _, slot_d, half = entry[0]
                    first = head_sub * Q_TOK + jp * q_lanes
                    lanes = slice(first, first + q_lanes)
                    blk = s[rows, lanes] + t2_ref[2 * pair + head_sub, slot_d]
                    if half is not None:
                        blk = blk + half_mask[half]
                    biased[t][i] = blk
                    blk_max = jnp.max(blk.reshape(GRID_W // 8, 8, q_lanes), axis=0)
                    m_acc[t] = blk_max if m_acc[t] is None else jnp.maximum(m_acc[t], blk_max)
            maxima = [jnp.max(m, axis=0, keepdims=True) for m in m_acc]
            return maxima, biased

        def softmax_pv(pair, state):
            maxima, biased = state
            for head_sub in range(2):
                head = 2 * pair + head_sub
                cols = []
                for jp in range(len(plans)):
                    t = 2 * head_sub + jp
                    blocks = [jnp.exp(biased[t][i] - maxima[t]).astype(BF16) if i in biased[t] else zero_blk
                              for i in range(key_lo, key_hi)]
                    cols.append(jnp.concatenate(blocks, axis=0))
                p_head = jnp.concatenate(cols, axis=1)
                rows = slice(head * NA_HEAD_DIM, (head + 1) * NA_HEAD_DIM)
                vt_head = _token_range(v_refs, tok_lo, tok_hi, rows, 1)
                vt_ext = jnp.concatenate([vt_head, ones_rows], axis=0)
                o = jnp.dot(vt_ext, p_head, preferred_element_type=F32)
                out_ref[sub, rows, :] = (o[:NA_HEAD_DIM] * (1.0 / o[NA_HEAD_DIM:NA_HEAD_DIM + 1])).astype(BF16)

        return scores, softmax_pv

    def body(step_variant):
        fns = [sub_group(sub, *plan) for sub, plan in enumerate(_step_plan(step_variant))]
        stages = [(sub, pair) for sub in range(len(fns)) for pair in range(n_pairs)]
        state = fns[0][0](0)
        for idx, (sub, pair) in enumerate(stages):
            nxt = None
            if idx + 1 < len(stages):
                nxt_sub, nxt_pair = stages[idx + 1]
                nxt = fns[nxt_sub][0](nxt_pair)
            fns[sub][1](pair, state)
            state = nxt

    is_top = step == 0
    is_bot = step == steps - 1
    pl.when(is_top)(lambda: body(0))
    pl.when(jnp.logical_not(is_top | is_bot))(lambda: body(1))
    pl.when(is_bot)(lambda: body(2))


def _attention(qt, k, vt, t2, batch, seq):
    t = batch * seq
    blocks = seq // Q_TOK
    steps = blocks // STEP_SUBS
    assert steps >= 2 and blocks >= KV_BLOCKS

    def first_kv(u):
        return jnp.clip(STEP_SUBS * u - 1, 0, blocks - KV_BLOCKS)

    def blk(u, b):
        return (b * steps + u, 0, 0)

    def kv(i):
        return lambda u, b: (b * blocks + first_kv(u) + i, 0)

    def kv_blk(i):
        return lambda u, b: (b * blocks + first_kv(u) + i, 0, 0)

    return pl.pallas_call(
        functools.partial(_attention_kernel, steps=steps),
        grid=(steps, batch),
        in_specs=[pl.BlockSpec((STEP_SUBS, NA_WIDTH, Q_TOK), blk)]
                 + [pl.BlockSpec((Q_TOK, NA_WIDTH), kv(i)) for i in range(KV_BLOCKS)]
                 + [pl.BlockSpec((1, NA_WIDTH, Q_TOK), kv_blk(i)) for i in range(KV_BLOCKS)]
                 + [pl.BlockSpec(t2.shape, lambda u, b: (0, 0, 0, 0))],
        out_specs=pl.BlockSpec((STEP_SUBS, NA_WIDTH, Q_TOK), blk),
        out_shape=jax.ShapeDtypeStruct((t // Q_TOK, NA_WIDTH, Q_TOK), BF16),
        compiler_params=pltpu.CompilerParams(
            dimension_semantics=("arbitrary", "arbitrary"),
            vmem_limit_bytes=VMEM_LIMIT),
        name="attention",
    )(qt, *([k] * KV_BLOCKS), *([vt] * KV_BLOCKS), t2)


DFT_CHUNK = 16
LANES = 128


def _to_lane_tiles(scr, x):
    for c in range(scr.shape[0]):
        scr[c] = x[:, c * LANES:(c + 1) * LANES]


def _from_lane_tiles(scr):
    return jnp.concatenate([scr[c] for c in range(scr.shape[0])], axis=1)


def _strided_rows(scr, start, size, stride):
    return jnp.concatenate([scr[c, pl.ds(start, size, stride=stride), :] for c in range(scr.shape[0])], axis=1)


def _store_strided_rows(scr, start, stride, x):
    for c in range(scr.shape[0]):
        scr[c, pl.ds(start, x.shape[0], stride=stride), :] = x[:, c * LANES:(c + 1) * LANES]


def _fourier_1_kernel(w_ref, u_ref, a_ref, *scratch, major, chunks):
    if not scratch:
        for c in range(chunks):
            rows = slice(c * DFT_CHUNK, (c + 1) * DFT_CHUNK)
            u2d = u_ref[0, :, rows, :].reshape(major * DFT_CHUNK, F_WIDTH)
            res = jnp.dot(w_ref[...], u2d, preferred_element_type=F32)
            a_ref[0, :, :, rows, :] = res.astype(BF16).reshape(major, 2, DFT_CHUNK, F_WIDTH)
        return
    in_w, out_w = scratch
    pairs = DFT_CHUNK // 2
    _to_lane_tiles(in_w, pltpu.bitcast(u_ref[0].reshape(major * DFT_CHUNK, F_WIDTH), jnp.uint32))
    for j in range(pairs):
        rhs = pltpu.bitcast(_strided_rows(in_w, j, major, pairs), BF16)
        res = jnp.dot(w_ref[...], rhs, preferred_element_type=F32)
        _store_strided_rows(out_w, j, pairs, pltpu.bitcast(res.astype(BF16), jnp.uint32))
    a_ref[0] = pltpu.bitcast(_from_lane_tiles(out_w), BF16).reshape(major, 2, DFT_CHUNK, F_WIDTH)


def _fourier_1(u4, w1, kron):
    batch, major, minor, _ = u4.shape
    chunks = minor // DFT_CHUNK if kron else 1
    tb = chunks * DFT_CHUNK
    scratch = [] if kron else [pltpu.VMEM((F_WIDTH // LANES, major * DFT_CHUNK // 2, LANES), jnp.uint32),
                               pltpu.VMEM((F_WIDTH // LANES, major * DFT_CHUNK, LANES), jnp.uint32)]
    return pl.pallas_call(
        functools.partial(_fourier_1_kernel, major=major, chunks=chunks),
        grid=(batch, minor // tb),
        in_specs=[pl.BlockSpec(w1.shape, lambda b, j: (0, 0)),
                  pl.BlockSpec((1, major, tb, F_WIDTH), lambda b, j: (b, 0, j, 0))],
        out_specs=pl.BlockSpec((1, major, 2, tb, F_WIDTH), lambda b, j: (b, 0, 0, j, 0)),
        out_shape=jax.ShapeDtypeStruct((batch, major, 2, minor, F_WIDTH), BF16),
        scratch_shapes=scratch,
        compiler_params=pltpu.CompilerParams(
            dimension_semantics=("arbitrary", "arbitrary"),
            vmem_limit_bytes=VMEM_LIMIT),
        name="fourier_1",
    )(w1, u4)


def _fourier_2_kernel(a_ref, m_ref, wc_ref, bf_ref, y_ref, x_scr, y_scr):
    for j in range(DFT_CHUNK):
        rhs = a_ref[0, j].reshape(2 * DFT_MINOR, F_WIDTH)
        x = jnp.dot(m_ref[j], rhs, preferred_element_type=F32)
        rows = slice(j * DFT_MINOR, (j + 1) * DFT_MINOR)
        x_scr[rows, :F_WIDTH] = x[:DFT_MINOR].astype(BF16)
        x_scr[rows, F_WIDTH:] = x[DFT_MINOR:].astype(BF16)
    y = jnp.dot(x_scr[...], wc_ref[...], preferred_element_type=F32) + bf_ref[...]
    pitch = DFT_CHUNK + 1
    for j in range(DFT_CHUNK):
        _store_strided_rows(y_scr, j, pitch, y[j * DFT_MINOR:(j + 1) * DFT_MINOR])
    packed = jnp.concatenate(
        [jnp.concatenate([y_scr[c, p * pitch:p * pitch + DFT_CHUNK, :] for p in range(DFT_MINOR)], axis=0)
         for c in range(y_scr.shape[0])], axis=1)
    y_ref[0] = packed.astype(BF16).reshape(DFT_MINOR, DFT_CHUNK, F_WIDTH)


def _fold_channel_dft_kernel(cs_ref, wf_ref, wc_ref):
    wc_ref[...] = jnp.dot(cs_ref[...], wf_ref[...], preferred_element_type=F32,
                          precision=lax.Precision.HIGHEST).astype(BF16)


def _fold_channel_dft(w_f):
    c = np.arange(F_GROUP_DIM)
    ang = 2.0 * np.pi * np.outer(c, c) / F_GROUP_DIM
    eye = np.eye(F_GROUPS)
    cs = np.concatenate([np.kron(eye, np.cos(ang)), np.kron(eye, np.sin(ang))], axis=0)
    return pl.pallas_call(
        _fold_channel_dft_kernel,
        out_shape=jax.ShapeDtypeStruct((2 * F_WIDTH, F_WIDTH), BF16),
        name="fold_channel_dft",
    )(jnp.asarray(cs, F32), w_f.astype(F32))


def _fourier_2(a5, m_tab, w_c, b_f):
    batch, major, _, minor, _ = a5.shape
    tr = DFT_CHUNK
    return pl.pallas_call(
        _fourier_2_kernel,
        grid=(batch, major // tr),
        in_specs=[pl.BlockSpec((1, tr, 2, minor, F_WIDTH), lambda b, r: (b, r, 0, 0, 0)),
                  pl.BlockSpec((tr, 2 * minor, 2 * minor), lambda b, r: (r, 0, 0)),
                  pl.BlockSpec(w_c.shape, lambda b, r: (0, 0)),
                  pl.BlockSpec((1, F_WIDTH), lambda b, r: (0, 0))],
        out_specs=pl.BlockSpec((1, minor, tr, F_WIDTH), lambda b, r: (b, 0, r, 0)),
        out_shape=jax.ShapeDtypeStruct((batch, minor, major, F_WIDTH), BF16),
        scratch_shapes=[pltpu.VMEM((tr * minor, 2 * F_WIDTH), BF16),
                        pltpu.VMEM((F_WIDTH // LANES, minor * (tr + 1), LANES), F32)],
        compiler_params=pltpu.CompilerParams(
            dimension_semantics=("arbitrary", "arbitrary"),
            vmem_limit_bytes=VMEM_LIMIT),
        name="fourier_2",
    )(a5, m_tab, w_c, b_f)


@functools.lru_cache(maxsize=None)
def _dft_constants(seq):
    major = seq // DFT_MINOR
    kron = major * DFT_CHUNK <= 256
    a = np.arange(major)
    ang1 = 2.0 * np.pi * np.outer(a, a) / major
    w1 = np.stack([np.cos(ang1), -np.sin(ang1)], axis=1).reshape(2 * major, major)
    w1 = np.kron(w1, np.eye(DFT_CHUNK if kron else 2))
    r = np.arange(major)[:, None, None]
    p = np.arange(DFT_MINOR)[None, :, None]
    b = np.arange(DFT_MINOR)[None, None, :]
    ang2 = 2.0 * np.pi * ((b * (r + major * p)) % seq) / seq
    scale = 1.0 / np.sqrt(seq * F_GROUP_DIM)
    e_re, e_im = np.cos(ang2) * scale, -np.sin(ang2) * scale
    m_tab = np.concatenate([np.concatenate([e_re, -e_im], axis=2),
                            np.concatenate([e_im, e_re], axis=2)], axis=1)
    return kron, np.asarray(w1, np.float32), np.asarray(m_tab, np.float32)


def _fourier_fused_kernel(w1_ref, u_ref, m_ref, wc_ref, bf_ref, y_ref, a_scr, *scratch, major, chunks):
    _fourier_1_kernel(w1_ref, u_ref, a_scr, major=major, chunks=chunks)
    _fourier_2_kernel(a_scr, m_ref, wc_ref, bf_ref, y_ref, *scratch)


def _fourier_fused(u4, w1, m_tab, w_c, b_f):
    batch, major, minor, _ = u4.shape
    const2 = lambda b: (0, 0)
    return pl.pallas_call(
        functools.partial(_fourier_fused_kernel, major=major, chunks=minor // DFT_CHUNK),
        grid=(batch,),
        in_specs=[pl.BlockSpec(w1.shape, const2),
                  pl.BlockSpec((1, major, minor, F_WIDTH), lambda b: (b, 0, 0, 0)),
                  pl.BlockSpec(m_tab.shape, lambda b: (0, 0, 0)),
                  pl.BlockSpec(w_c.shape, const2),
                  pl.BlockSpec((1, F_WIDTH), const2)],
        out_specs=pl.BlockSpec((1, minor, major, F_WIDTH), lambda b: (b, 0, 0, 0)),
        out_shape=jax.ShapeDtypeStruct((batch, minor, major, F_WIDTH), BF16),
        scratch_shapes=[pltpu.VMEM((1, major, 2, minor, F_WIDTH), BF16),
                        pltpu.VMEM((major * minor, 2 * F_WIDTH), BF16),
                        pltpu.VMEM((F_WIDTH // LANES, minor * (major + 1), LANES), F32)],
        compiler_params=pltpu.CompilerParams(
            dimension_semantics=("arbitrary",), vmem_limit_bytes=VMEM_LIMIT),
        name="fourier_fused",
    )(w1, u4, m_tab, w_c, b_f)


def _fourier(u, w_c, b_f, batch, seq):
    major = seq // DFT_MINOR
    kron, w1, m_tab = _dft_constants(seq)
    u4 = u.reshape(batch, major, DFT_MINOR, F_WIDTH)
    w1, m_tab = jnp.asarray(w1).astype(BF16), jnp.asarray(m_tab).astype(BF16)
    if kron and major == DFT_CHUNK:
        return _fourier_fused(u4, w1, m_tab, w_c, b_f).reshape(batch * seq, F_WIDTH)
    a = _fourier_1(u4, w1, kron)
    y = _fourier_2(a, m_tab, w_c, b_f)
    return y.reshape(batch * seq, F_WIDTH)


def _out_proj_kernel(ot_ref, zat_ref, yf_ref, zf_ref, x_ref, wa_ref, wf_ref,
                     gna_ref, gf_ref, gp_ref, o_ref):
    blocks = range(ot_ref.shape[0])
    y_t = jnp.concatenate([ot_ref[i] for i in blocks], axis=1).astype(F32)
    z_t = jnp.concatenate([zat_ref[i] for i in blocks], axis=1)
    inv = lax.rsqrt(jnp.mean(y_t * y_t, axis=0, keepdims=True) + RMS_EPS)
    mixed_at = _gate(y_t * inv * gna_ref[...], z_t)
    mixed_f = _gate(_rms(yf_ref[...].astype(F32), gf_ref[...]), zf_ref[...])
    out = lax.dot_general(mixed_at, wa_ref[...], (((0,), (0,)), ((), ())),
                          preferred_element_type=F32)
    out = out + jnp.dot(mixed_f, wf_ref[...], preferred_element_type=F32)
    o_ref[...] = x_ref[...] + _rms(out, gp_ref[...])


def _out_proj_specs(t, tm, w_a, w_fo):
    tok = lambda i: (i, 0)
    blk = lambda i: (i, 0, 0)
    const = lambda i: (0, 0)
    in_specs = [pl.BlockSpec((tm // Q_TOK, NA_WIDTH, Q_TOK), blk),
                pl.BlockSpec((tm // Q_TOK, NA_WIDTH, Q_TOK), blk),
                pl.BlockSpec((tm, F_WIDTH), tok),
                pl.BlockSpec((tm, F_WIDTH), tok),
                pl.BlockSpec((tm, D_MODEL), tok),
                pl.BlockSpec(w_a.shape, const),
                pl.BlockSpec(w_fo.shape, const),
                pl.BlockSpec((NA_WIDTH, 1), const),
                pl.BlockSpec((1, F_WIDTH), const),
                pl.BlockSpec((1, D_MODEL), const)]
    return in_specs, pl.BlockSpec((tm, D_MODEL), tok), jax.ShapeDtypeStruct((t, D_MODEL), F32)


def _out_proj(o_t, za_t, y_f, z_f, x2d, w_a, w_fo, g_na_col, g_f, g_post, tm=TOKEN_BLOCK):
    t = x2d.shape[0]
    in_specs, out_spec, out_shape = _out_proj_specs(t, tm, w_a, w_fo)
    return pl.pallas_call(
        _out_proj_kernel,
        grid=(t // tm,),
        in_specs=in_specs,
        out_specs=out_spec,
        out_shape=out_shape,
        compiler_params=pltpu.CompilerParams(
            dimension_semantics=("arbitrary",), vmem_limit_bytes=VMEM_LIMIT),
        name="out_proj",
    )(o_t, za_t, y_f, z_f, x2d, w_a, w_fo, g_na_col, g_f, g_post)


def _in_out_proj_kernel(x_ref, g_ref, wn_ref, wt_ref,
                        ot_ref, zat_in_ref, yf_ref, zf_in_ref, xo_ref, wa_ref, wf_ref, gna_ref, gf_ref, gp_ref,
                        k_ref, uf_ref, zf_ref, qt_ref, vt_ref, zat_ref, o_ref):
    h = _rms(x_ref[...], g_ref[...]).astype(BF16)
    blocks = range(ot_ref.shape[0])
    y_t = jnp.concatenate([ot_ref[i] for i in blocks], axis=1).astype(F32)
    z_t = jnp.concatenate([zat_in_ref[i] for i in blocks], axis=1)
    nat = jnp.dot(h, wn_ref[...], preferred_element_type=F32)
    inv = lax.rsqrt(jnp.mean(y_t * y_t, axis=0, keepdims=True) + RMS_EPS)
    mixed_at = _gate(y_t * inv * gna_ref[...], z_t)
    mixed_f = _gate(_rms(yf_ref[...].astype(F32), gf_ref[...]), zf_in_ref[...])
    k_ref[...] = nat[:, 0 * NA_WIDTH:1 * NA_WIDTH].astype(BF16)
    uf_ref[...] = nat[:, 1 * NA_WIDTH:2 * NA_WIDTH].astype(BF16)
    zf_ref[...] = nat[:, 2 * NA_WIDTH:3 * NA_WIDTH].astype(BF16)
    out = lax.dot_general(mixed_at, wa_ref[...], (((0,), (0,)), ((), ())), preferred_element_type=F32)
    out = out + jnp.dot(mixed_f, wf_ref[...], preferred_element_type=F32)
    tr = lax.dot_general(wt_ref[...], h, (((1,), (1,)), ((), ())), preferred_element_type=F32)
    o_ref[...] = xo_ref[...] + _rms(out, gp_ref[...])
    for i in range(qt_ref.shape[0]):
        tok = slice(i * Q_TOK, (i + 1) * Q_TOK)
        qt_ref[i] = tr[0 * NA_WIDTH:1 * NA_WIDTH, tok].astype(BF16)
        vt_ref[i] = tr[1 * NA_WIDTH:2 * NA_WIDTH, tok].astype(BF16)
        zat_ref[i] = tr[2 * NA_WIDTH:3 * NA_WIDTH, tok].astype(BF16)


def _in_out_proj(in_args, out_args, tm=FUSED_TOKEN_BLOCK):
    x_in, _, w_nat, w_tr = in_args
    x_out, w_a, w_fo = out_args[4], out_args[5], out_args[6]
    t = x_in.shape[0]
    assert x_out.shape[0] == t
    in_specs_a, out_specs_a, out_shape_a = _in_proj_specs(t, tm, w_nat, w_tr)
    in_specs_b, out_spec_b, out_shape_b = _out_proj_specs(t, tm, w_a, w_fo)
    res = pl.pallas_call(
        _in_out_proj_kernel,
        grid=(t // tm,),
        in_specs=in_specs_a + in_specs_b,
        out_specs=out_specs_a + [out_spec_b],
        out_shape=out_shape_a + [out_shape_b],
        compiler_params=pltpu.CompilerParams(
            dimension_semantics=("arbitrary",), vmem_limit_bytes=VMEM_LIMIT),
        name="in_out_proj",
    )(*in_args, *out_args)
    return res[:N_IN_PROJ_OUTPUTS], res[-1]


def _mixers(proj, bias, w_c, b_f, batch, seq):
    k, u_f, z_f, q_t, v_t, za_t = proj
    o_t = _attention(q_t, k, v_t, bias, batch, seq)
    y_f = _fourier(u_f, w_c, b_f, batch, seq)
    return o_t, za_t, y_f, z_f


def kernel(x_prompt, x_sample, w_in, rpb, w_fourier, b_fourier, g_pre, g_na, g_f, w_out, g_post):
    depth = w_in.shape[0]
    y_prompt, y_sample = x_prompt, x_sample
    scale = NA_HEAD_DIM ** -0.5
    for l in range(depth):
        w = w_in[l]
        w_nat = jnp.concatenate([w[:, i * NA_WIDTH:(i + 1) * NA_WIDTH] for i in (1, 4, 5)],
                                axis=1).astype(BF16)
        w_tr = _transposed_weights(w, (0, 2, 3), scale)
        bias = _bias_table(rpb[l])
        w_c = _fold_channel_dft(w_fourier[l])
        row = lambda v: v.reshape(1, -1).astype(F32)
        b_f, g_in = row(b_fourier[l]), row(g_pre[l])
        out_w = (w_out[l][:NA_WIDTH].astype(BF16), w_out[l][NA_WIDTH:].astype(BF16),
                 g_na[l].reshape(-1, 1).astype(F32), row(g_f[l]), row(g_post[l]))
        (bp, sp, _), (bs, ss, _) = y_prompt.shape, y_sample.shape
        xp, xs = y_prompt.reshape(bp * sp, D_MODEL), y_sample.reshape(bs * ss, D_MODEL)

        mixed_p = _mixers(_in_proj(xp, g_in, w_nat, w_tr), bias, w_c, b_f, bp, sp)
        if xp.shape[0] == xs.shape[0]:
            proj_s, out_p = _in_out_proj((xs, g_in, w_nat, w_tr), (*mixed_p, xp, *out_w))
        else:
            proj_s = _in_proj(xs, g_in, w_nat, w_tr)
            out_p = _out_proj(*mixed_p, xp, *out_w)
        mixed_s = _mixers(proj_s, bias, w_c, b_f, bs, ss)
        out_s = _out_proj(*mixed_s, xs, *out_w)
        y_prompt, y_sample = out_p.reshape(bp, sp, D_MODEL), out_s.reshape(bs, ss, D_MODEL)
    return (y_prompt, y_sample)
```

```python
import functools

import numpy as np
import jax
import jax.numpy as jnp
from jax import lax
from jax.experimental import pallas as pl
from jax.experimental.pallas import tpu as pltpu

D_MODEL = 1024
GRID_W = 64
WIN_ROWS = 8
WIN_COLS = 16
NA_HEADS = 8
NA_HEAD_DIM = 64
NA_WIDTH = NA_HEADS * NA_HEAD_DIM
F_GROUPS = 8
F_GROUP_DIM = 64
F_WIDTH = F_GROUPS * F_GROUP_DIM
RMS_EPS = 1e-6
NEG_INF = -1e30

Q_ROWS = 4
Q_TOK = Q_ROWS * GRID_W
STEP_SUBS = 4
KV_BLOCKS = STEP_SUBS + 2
KEY_ROWS = 3 * Q_ROWS
ROW_SLOTS = 2 * WIN_ROWS
LOW_SLOT = ROW_SLOTS
HIGH_SLOT = ROW_SLOTS + 1
TABLE_SLOTS = ROW_SLOTS + 2
LOG2E = 1.4426950408889634
DFT_MINOR = 128
TOKEN_BLOCK = 1024
FUSED_TOKEN_BLOCK = 512
VMEM_LIMIT = 56 * 1024 * 1024

BF16 = jnp.bfloat16
F32 = jnp.float32


def _rms(x, g):
    inv = lax.rsqrt(jnp.mean(x * x, axis=-1, keepdims=True) + RMS_EPS)
    return x * inv * g


def _silu(z):
    return z * (1.0 / (1.0 + jnp.exp(-z)))


def _gate(y_normed, z):
    return y_normed.astype(BF16) * _silu(z)


def _in_proj_kernel(x_ref, g_ref, wn_ref, wt_ref,
                    k_ref, uf_ref, zf_ref, qt_ref, vt_ref, zat_ref):
    h = _rms(x_ref[...], g_ref[...]).astype(BF16)
    nat = jnp.dot(h, wn_ref[...], preferred_element_type=F32)
    k_ref[...] = nat[:, 0 * NA_WIDTH:1 * NA_WIDTH].astype(BF16)
    uf_ref[...] = nat[:, 1 * NA_WIDTH:2 * NA_WIDTH].astype(BF16)
    zf_ref[...] = nat[:, 2 * NA_WIDTH:3 * NA_WIDTH].astype(BF16)
    tr = lax.dot_general(wt_ref[...], h, (((1,), (1,)), ((), ())),
                         preferred_element_type=F32)
    for i in range(qt_ref.shape[0]):
        tok = slice(i * Q_TOK, (i + 1) * Q_TOK)
        qt_ref[i] = tr[0 * NA_WIDTH:1 * NA_WIDTH, tok].astype(BF16)
        vt_ref[i] = tr[1 * NA_WIDTH:2 * NA_WIDTH, tok].astype(BF16)
        zat_ref[i] = tr[2 * NA_WIDTH:3 * NA_WIDTH, tok].astype(BF16)


def _transposed_weights_kernel(col_ref, w_ref, o_ref, *, scale):
    del col_ref
    factor = jnp.where(pl.program_id(0) == 0, scale, 1.0).astype(F32)
    o_ref[...] = (w_ref[...] * factor).T.astype(BF16)


def _transposed_weights(w, groups, scale):
    d, width = w.shape[0], NA_WIDTH
    col = jnp.asarray(groups, jnp.int32)
    return pl.pallas_call(
        functools.partial(_transposed_weights_kernel, scale=scale),
        grid_spec=pltpu.PrefetchScalarGridSpec(
            num_scalar_prefetch=1,
            grid=(len(groups),),
            in_specs=[pl.BlockSpec((d, width), lambda i, col: (0, col[i]))],
            out_specs=pl.BlockSpec((width, d), lambda i, col: (i, 0))),
        out_shape=jax.ShapeDtypeStruct((len(groups) * width, d), BF16),
        name="transposed_weights",
    )(col, w)


N_IN_PROJ_INPUTS = 4
N_IN_PROJ_OUTPUTS = 6


def _in_proj_specs(t, tm, w_nat, w_tr):
    tok = lambda i: (i, 0)
    blk = lambda i: (i, 0, 0)
    const = lambda i: (0, 0)
    nat_shape = jax.ShapeDtypeStruct((t, NA_WIDTH), BF16)
    tr_shape = jax.ShapeDtypeStruct((t // Q_TOK, NA_WIDTH, Q_TOK), BF16)
    in_specs = [pl.BlockSpec((tm, D_MODEL), tok),
                pl.BlockSpec((1, D_MODEL), const),
                pl.BlockSpec(w_nat.shape, const),
                pl.BlockSpec(w_tr.shape, const)]
    out_specs = ([pl.BlockSpec((tm, NA_WIDTH), tok)] * 3
                 + [pl.BlockSpec((tm // Q_TOK, NA_WIDTH, Q_TOK), blk)] * 3)
    return in_specs, out_specs, [nat_shape] * 3 + [tr_shape] * 3


def _in_proj(x2d, g_pre, w_nat, w_tr, tm=TOKEN_BLOCK):
    t = x2d.shape[0]
    in_specs, out_specs, out_shape = _in_proj_specs(t, tm, w_nat, w_tr)
    return pl.pallas_call(
        _in_proj_kernel,
        grid=(t // tm,),
        in_specs=in_specs,
        out_specs=out_specs,
        out_shape=out_shape,
        compiler_params=pltpu.CompilerParams(
            dimension_semantics=("arbitrary",), vmem_limit_bytes=VMEM_LIMIT),
        name="in_proj",
    )(x2d, g_pre, w_nat, w_tr)


def _bias_table_kernel(rpb_ref, t2_ref):
    head = pl.program_id(0)
    shape = (GRID_W, 2 * GRID_W)
    kc = lax.broadcasted_iota(jnp.int32, shape, 0)
    lane = lax.broadcasted_iota(jnp.int32, shape, 1)
    second = lane >= GRID_W
    qc = jnp.where(second, lane - GRID_W, lane)
    col_off = kc - qc + (WIN_COLS - 1)
    win_start = jnp.clip(qc - WIN_COLS // 2, 0, GRID_W - WIN_COLS)
    col_valid = (kc >= win_start) & (kc < win_start + WIN_COLS)
    n_off = 2 * WIN_COLS - 1
    n_row = 2 * WIN_ROWS - 1

    def entry(d, o):
        if 0 <= d < n_row:
            return rpb_ref[(head * n_row + d) * n_off + o]
        return jnp.float32(NEG_INF)

    edge_low = WIN_ROWS - 1 - WIN_ROWS // 2
    edge_high = WIN_ROWS - 1 + WIN_ROWS // 2
    for d in range(ROW_SLOTS):
        tile = jnp.full(shape, NEG_INF, F32)
        for o in range(n_off):
            val = jnp.where(second, entry(d - 1, o), entry(d, o))
            tile = jnp.where(col_off == o, val, tile)
        tile = jnp.where(col_valid, tile * LOG2E, NEG_INF)
        t2_ref[0, d] = tile
        if d == edge_low:
            t2_ref[0, LOW_SLOT] = jnp.where(second, NEG_INF, tile)
        if d == edge_high:
            t2_ref[0, HIGH_SLOT] = jnp.where(second, tile, NEG_INF)


def _bias_table(rpb):
    return pl.pallas_call(
        _bias_table_kernel,
        grid=(NA_HEADS,),
        in_specs=[pl.BlockSpec(memory_space=pltpu.SMEM)],
        out_specs=pl.BlockSpec((1, TABLE_SLOTS, GRID_W, 2 * GRID_W), lambda h: (h, 0, 0, 0)),
        out_shape=jax.ShapeDtypeStruct((NA_HEADS, TABLE_SLOTS, GRID_W, 2 * GRID_W), F32),
        compiler_params=pltpu.CompilerParams(dimension_semantics=("arbitrary",)),
        name="bias_table",
    )(rpb.astype(F32).reshape(-1))


def _token_range(refs, lo, hi, other, axis):
    pieces = []
    for idx, ref in enumerate(refs):
        a, b = max(lo, idx * Q_TOK), min(hi, (idx + 1) * Q_TOK)
        if a < b:
            tok = slice(a - idx * Q_TOK, b - idx * Q_TOK)
            pieces.append(ref[tok, other] if axis == 0 else ref[0, other, tok])
    return pieces[0] if len(pieces) == 1 else jnp.concatenate(pieces, axis=axis)


def _window_plan(variant, jp):
    if variant == 0:
        return [(i, i - 2 * jp + WIN_ROWS - 1) for i in range(WIN_ROWS)]
    if variant == 2:
        q_row = KEY_ROWS - Q_ROWS + 2 * jp
        return [(i, i - q_row + WIN_ROWS - 1) for i in range(KEY_ROWS - WIN_ROWS, KEY_ROWS)]
    q_row = Q_ROWS + 2 * jp
    first, last = q_row - WIN_ROWS // 2, q_row + WIN_ROWS // 2
    return [(i, LOW_SLOT if i == first else HIGH_SLOT if i == last else i - q_row + WIN_ROWS - 1)
            for i in range(first, last + 1)]


def _step_plan(position):
    plan = []
    for j in range(STEP_SUBS):
        if position == 0:
            plan.append((0, 0) if j == 0 else (1, (j - 1) * Q_ROWS))
        elif position == 1:
            plan.append((1, j * Q_ROWS))
        else:
            last = j == STEP_SUBS - 1
            plan.append((2 if last else 1, (j + 1 - last) * Q_ROWS))
    return tuple(plan)


def _attention_kernel(qt_ref, *refs, steps):
    k_refs, v_refs = refs[:KV_BLOCKS], refs[KV_BLOCKS:2 * KV_BLOCKS]
    t2_ref, out_ref = refs[2 * KV_BLOCKS:]
    step = pl.program_id(0)
    pair_lanes = 2 * NA_HEAD_DIM
    q_lanes = 2 * GRID_W
    n_pairs = NA_HEADS // 2

    def sub_group(sub, variant, key_off):
        plans = [_window_plan(variant, jp) for jp in range(Q_ROWS // 2)]
        key_lo = min(p[0][0] for p in plans)
        key_top = max(p[-1][0] for p in plans) + 1
        key_hi = key_top + (key_top - key_lo) % 2
        tok_lo, tok_top, tok_hi = ((key_off + r) * GRID_W for r in (key_lo, key_top, key_hi))
        ones_rows = jnp.ones((16, tok_hi - tok_lo), BF16)
        zero_blk = jnp.zeros((GRID_W, q_lanes), BF16)

        def scores(pair):
            cols = slice(pair * pair_lanes, (pair + 1) * pair_lanes)
            k_pair = _token_range(k_refs, tok_lo, tok_top, cols, 0)
            qt_pair = qt_ref[sub, cols, :]
            zero = jnp.zeros((NA_HEAD_DIM, Q_TOK), BF16)
            qt_both = jnp.concatenate(
                [jnp.concatenate([qt_pair[:NA_HEAD_DIM], zero], axis=0),
                 jnp.concatenate([zero, qt_pair[NA_HEAD_DIM:]], axis=0)], axis=1)
            s = jnp.dot(k_pair, qt_both, preferred_element_type=F32)
            tiles = [(head_sub, jp) for head_sub in range(2) for jp in range(len(plans))]
            m_acc = [None] * len(tiles)
            biased = [{} for _ in tiles]
            for i in range(key_lo, key_top):
                rows = slice((i - key_lo) * GRID_W, (i - key_lo + 1) * GRID_W)
                for t, (head_sub, jp) in enumerate(tiles):
                    entry = [e for e in plans[jp] if e[0] == i]
                    if not entry:
                        continue
                    first = head_sub * Q_TOK + jp * q_lanes
                    lanes = slice(first, first + q_lanes)
                    blk = s[rows, lanes] + t2_ref[2 * pair + head_sub, entry[0][1]]
                    biased[t][i] = blk
                    blk_max = jnp.max(blk.reshape(GRID_W // 8, 8, q_lanes), axis=0)
                    m_acc[t] = blk_max if m_acc[t] is None else jnp.maximum(m_acc[t], blk_max)
            maxima = [jnp.max(m, axis=0, keepdims=True) for m in m_acc]
            return maxima, biased

        def softmax_pv(pair, state):
            maxima, biased = state
            for head_sub in range(2):
                head = 2 * pair + head_sub
                cols = []
                for jp in range(len(plans)):
                    t = 2 * head_sub + jp
                    blocks = [jnp.exp2(biased[t][i] - maxima[t]).astype(BF16) if i in biased[t] else zero_blk
                              for i in range(key_lo, key_hi)]
                    cols.append(jnp.concatenate(blocks, axis=0))
                p_head = jnp.concatenate(cols, axis=1)
                rows = slice(head * NA_HEAD_DIM, (head + 1) * NA_HEAD_DIM)
                vt_head = _token_range(v_refs, tok_lo, tok_hi, rows, 1)
                vt_ext = jnp.concatenate([vt_head, ones_rows], axis=0)
                o = jnp.dot(vt_ext, p_head, preferred_element_type=F32)
                out_ref[sub, rows, :] = (o[:NA_HEAD_DIM] * (1.0 / o[NA_HEAD_DIM:NA_HEAD_DIM + 1])).astype(BF16)

        return scores, softmax_pv

    def body(step_variant):
        fns = [sub_group(sub, *plan) for sub, plan in enumerate(_step_plan(step_variant))]
        stages = [(sub, pair) for sub in range(len(fns)) for pair in range(n_pairs)]
        state = fns[0][0](0)
        for idx, (sub, pair) in enumerate(stages):
            nxt = None
            if idx + 1 < len(stages):
                nxt_sub, nxt_pair = stages[idx + 1]
                nxt = fns[nxt_sub][0](nxt_pair)
            fns[sub][1](pair, state)
            state = nxt

    is_top = step == 0
    is_bot = step == steps - 1
    pl.when(is_top)(lambda: body(0))
    pl.when(jnp.logical_not(is_top | is_bot))(lambda: body(1))
    pl.when(is_bot)(lambda: body(2))


def _attention(qt, k, vt, t2, batch, seq):
    t = batch * seq
    blocks = seq // Q_TOK
    steps = blocks // STEP_SUBS
    assert steps >= 2 and blocks >= KV_BLOCKS

    def first_kv(u):
        return jnp.clip(STEP_SUBS * u - 1, 0, blocks - KV_BLOCKS)

    def blk(u, b):
        return (b * steps + u, 0, 0)

    def kv(i):
        return lambda u, b: (b * blocks + first_kv(u) + i, 0)

    def kv_blk(i):
        return lambda u, b: (b * blocks + first_kv(u) + i, 0, 0)

    return pl.pallas_call(
        functools.partial(_attention_kernel, steps=steps),
        grid=(steps, batch),
        in_specs=[pl.BlockSpec((STEP_SUBS, NA_WIDTH, Q_TOK), blk)]
                 + [pl.BlockSpec((Q_TOK, NA_WIDTH), kv(i)) for i in range(KV_BLOCKS)]
                 + [pl.BlockSpec((1, NA_WIDTH, Q_TOK), kv_blk(i)) for i in range(KV_BLOCKS)]
                 + [pl.BlockSpec(t2.shape, lambda u, b: (0, 0, 0, 0))],
        out_specs=pl.BlockSpec((STEP_SUBS, NA_WIDTH, Q_TOK), blk),
        out_shape=jax.ShapeDtypeStruct((t // Q_TOK, NA_WIDTH, Q_TOK), BF16),
        compiler_params=pltpu.CompilerParams(
            dimension_semantics=("arbitrary", "arbitrary"),
            vmem_limit_bytes=VMEM_LIMIT),
        name="attention",
    )(qt, *([k] * KV_BLOCKS), *([vt] * KV_BLOCKS), t2)


DFT_CHUNK = 16
LANES = 128


def _to_lane_tiles(scr, x):
    for c in range(scr.shape[0]):
        scr[c] = x[:, c * LANES:(c + 1) * LANES]


def _from_lane_tiles(scr):
    return jnp.concatenate([scr[c] for c in range(scr.shape[0])], axis=1)


def _strided_rows(scr, start, size, stride):
    return jnp.concatenate([scr[c, pl.ds(start, size, stride=stride), :] for c in range(scr.shape[0])], axis=1)


def _store_strided_rows(scr, start, stride, x):
    for c in range(scr.shape[0]):
        scr[c, pl.ds(start, x.shape[0], stride=stride), :] = x[:, c * LANES:(c + 1) * LANES]


def _fourier_1_kernel(w_ref, u_ref, a_ref, *scratch, major, chunks):
    if not scratch:
        for c in range(chunks):
            rows = slice(c * DFT_CHUNK, (c + 1) * DFT_CHUNK)
            u2d = u_ref[0, :, rows, :].reshape(major * DFT_CHUNK, F_WIDTH)
            res = jnp.dot(w_ref[...], u2d, preferred_element_type=F32)
            a_ref[0, :, :, rows, :] = res.astype(BF16).reshape(major, 2, DFT_CHUNK, F_WIDTH)
        return
    in_w, out_w = scratch
    pairs = DFT_CHUNK // 2
    _to_lane_tiles(in_w, pltpu.bitcast(u_ref[0].reshape(major * DFT_CHUNK, F_WIDTH), jnp.uint32))
    for j in range(pairs):
        rhs = pltpu.bitcast(_strided_rows(in_w, j, major, pairs), BF16)
        res = jnp.dot(w_ref[...], rhs, preferred_element_type=F32)
        _store_strided_rows(out_w, j, pairs, pltpu.bitcast(res.astype(BF16), jnp.uint32))
    a_ref[0] = pltpu.bitcast(_from_lane_tiles(out_w), BF16).reshape(major, 2, DFT_CHUNK, F_WIDTH)


def _fourier_1(u4, w1, kron):
    batch, major, minor, _ = u4.shape
    chunks = minor // DFT_CHUNK if kron else 1
    tb = chunks * DFT_CHUNK
    scratch = [] if kron else [pltpu.VMEM((F_WIDTH // LANES, major * DFT_CHUNK // 2, LANES), jnp.uint32),
                               pltpu.VMEM((F_WIDTH // LANES, major * DFT_CHUNK, LANES), jnp.uint32)]
    return pl.pallas_call(
        functools.partial(_fourier_1_kernel, major=major, chunks=chunks),
        grid=(batch, minor // tb),
        in_specs=[pl.BlockSpec(w1.shape, lambda b, j: (0, 0)),
                  pl.BlockSpec((1, major, tb, F_WIDTH), lambda b, j: (b, 0, j, 0))],
        out_specs=pl.BlockSpec((1, major, 2, tb, F_WIDTH), lambda b, j: (b, 0, 0, j, 0)),
        out_shape=jax.ShapeDtypeStruct((batch, major, 2, minor, F_WIDTH), BF16),
        scratch_shapes=scratch,
        compiler_params=pltpu.CompilerParams(
            dimension_semantics=("arbitrary", "arbitrary"),
            vmem_limit_bytes=VMEM_LIMIT),
        name="fourier_1",
    )(w1, u4)


def _fourier_2_kernel(a_ref, m_ref, wc_ref, bf_ref, y_ref, x_scr, y_scr):
    for j in range(DFT_CHUNK):
        rhs = a_ref[0, j].reshape(2 * DFT_MINOR, F_WIDTH)
        x = jnp.dot(m_ref[j], rhs, preferred_element_type=F32)
        rows = slice(j * DFT_MINOR, (j + 1) * DFT_MINOR)
        x_scr[rows, :F_WIDTH] = x[:DFT_MINOR].astype(BF16)
        x_scr[rows, F_WIDTH:] = x[DFT_MINOR:].astype(BF16)
    y = jnp.dot(x_scr[...], wc_ref[...], preferred_element_type=F32) + bf_ref[...]
    pitch = DFT_CHUNK + 1
    for j in range(DFT_CHUNK):
        _store_strided_rows(y_scr, j, pitch, y[j * DFT_MINOR:(j + 1) * DFT_MINOR])
    packed = jnp.concatenate(
        [jnp.concatenate([y_scr[c, p * pitch:p * pitch + DFT_CHUNK, :] for p in range(DFT_MINOR)], axis=0)
         for c in range(y_scr.shape[0])], axis=1)
    y_ref[0] = packed.astype(BF16).reshape(DFT_MINOR, DFT_CHUNK, F_WIDTH)


def _fold_channel_dft_kernel(cs_ref, wf_ref, wc_ref):
    wc_ref[...] = jnp.dot(cs_ref[...], wf_ref[...], preferred_element_type=F32,
                          precision=lax.Precision.HIGHEST).astype(BF16)


def _fold_channel_dft(w_f):
    c = np.arange(F_GROUP_DIM)
    ang = 2.0 * np.pi * np.outer(c, c) / F_GROUP_DIM
    eye = np.eye(F_GROUPS)
    cs = np.concatenate([np.kron(eye, np.cos(ang)), np.kron(eye, np.sin(ang))], axis=0)
    return pl.pallas_call(
        _fold_channel_dft_kernel,
        out_shape=jax.ShapeDtypeStruct((2 * F_WIDTH, F_WIDTH), BF16),
        name="fold_channel_dft",
    )(jnp.asarray(cs, F32), w_f.astype(F32))


def _fourier_2(a5, m_tab, w_c, b_f):
    batch, major, _, minor, _ = a5.shape
    tr = DFT_CHUNK
    return pl.pallas_call(
        _fourier_2_kernel,
        grid=(batch, major // tr),
        in_specs=[pl.BlockSpec((1, tr, 2, minor, F_WIDTH), lambda b, r: (b, r, 0, 0, 0)),
                  pl.BlockSpec((tr, 2 * minor, 2 * minor), lambda b, r: (r, 0, 0)),
                  pl.BlockSpec(w_c.shape, lambda b, r: (0, 0)),
                  pl.BlockSpec((1, F_WIDTH), lambda b, r: (0, 0))],
        out_specs=pl.BlockSpec((1, minor, tr, F_WIDTH), lambda b, r: (b, 0, r, 0)),
        out_shape=jax.ShapeDtypeStruct((batch, minor, major, F_WIDTH), BF16),
        scratch_shapes=[pltpu.VMEM((tr * minor, 2 * F_WIDTH), BF16),
                        pltpu.VMEM((F_WIDTH // LANES, minor * (tr + 1), LANES), F32)],
        compiler_params=pltpu.CompilerParams(
            dimension_semantics=("arbitrary", "arbitrary"),
            vmem_limit_bytes=VMEM_LIMIT),
        name="fourier_2",
    )(a5, m_tab, w_c, b_f)


@functools.lru_cache(maxsize=None)
def _dft_constants(seq):
    major = seq // DFT_MINOR
    kron = major * DFT_CHUNK <= 256
    a = np.arange(major)
    ang1 = 2.0 * np.pi * np.outer(a, a) / major
    w1 = np.stack([np.cos(ang1), -np.sin(ang1)], axis=1).reshape(2 * major, major)
    w1 = np.kron(w1, np.eye(DFT_CHUNK if kron else 2))
    r = np.arange(major)[:, None, None]
    p = np.arange(DFT_MINOR)[None, :, None]
    b = np.arange(DFT_MINOR)[None, None, :]
    ang2 = 2.0 * np.pi * ((b * (r + major * p)) % seq) / seq
    scale = 1.0 / np.sqrt(seq * F_GROUP_DIM)
    e_re, e_im = np.cos(ang2) * scale, -np.sin(ang2) * scale
    m_tab = np.concatenate([np.concatenate([e_re, -e_im], axis=2),
                            np.concatenate([e_im, e_re], axis=2)], axis=1)
    return kron, np.asarray(w1, np.float32), np.asarray(m_tab, np.float32)


def _fourier_fused_kernel(w1_ref, u_ref, m_ref, wc_ref, bf_ref, y_ref, a_scr, *scratch, major, chunks):
    _fourier_1_kernel(w1_ref, u_ref, a_scr, major=major, chunks=chunks)
    _fourier_2_kernel(a_scr, m_ref, wc_ref, bf_ref, y_ref, *scratch)


def _fourier_fused(u4, w1, m_tab, w_c, b_f):
    batch, major, minor, _ = u4.shape
    const2 = lambda b: (0, 0)
    return pl.pallas_call(
        functools.partial(_fourier_fused_kernel, major=major, chunks=minor // DFT_CHUNK),
        grid=(batch,),
        in_specs=[pl.BlockSpec(w1.shape, const2),
                  pl.BlockSpec((1, major, minor, F_WIDTH), lambda b: (b, 0, 0, 0)),
                  pl.BlockSpec(m_tab.shape, lambda b: (0, 0, 0)),
                  pl.BlockSpec(w_c.shape, const2),
                  pl.BlockSpec((1, F_WIDTH), const2)],
        out_specs=pl.BlockSpec((1, minor, major, F_WIDTH), lambda b: (b, 0, 0, 0)),
        out_shape=jax.ShapeDtypeStruct((batch, minor, major, F_WIDTH), BF16),
        scratch_shapes=[pltpu.VMEM((1, major, 2, minor, F_WIDTH), BF16),
                        pltpu.VMEM((major * minor, 2 * F_WIDTH), BF16),
                        pltpu.VMEM((F_WIDTH // LANES, minor * (major + 1), LANES), F32)],
        compiler_params=pltpu.CompilerParams(
            dimension_semantics=("arbitrary",), vmem_limit_bytes=VMEM_LIMIT),
        name="fourier_fused",
    )(w1, u4, m_tab, w_c, b_f)


def _fourier(u, w_c, b_f, batch, seq):
    major = seq // DFT_MINOR
    kron, w1, m_tab = _dft_constants(seq)
    u4 = u.reshape(batch, major, DFT_MINOR, F_WIDTH)
    w1, m_tab = jnp.asarray(w1).astype(BF16), jnp.asarray(m_tab).astype(BF16)
    if kron and major == DFT_CHUNK:
        return _fourier_fused(u4, w1, m_tab, w_c, b_f).reshape(batch * seq, F_WIDTH)
    a = _fourier_1(u4, w1, kron)
    y = _fourier_2(a, m_tab, w_c, b_f)
    return y.reshape(batch * seq, F_WIDTH)


def _out_proj_kernel(ot_ref, zat_ref, yf_ref, zf_ref, x_ref, wa_ref, wf_ref,
                     gna_ref, gf_ref, gp_ref, o_ref):
    blocks = range(ot_ref.shape[0])
    y_t = jnp.concatenate([ot_ref[i] for i in blocks], axis=1).astype(F32)
    z_t = jnp.concatenate([zat_ref[i] for i in blocks], axis=1)
    inv = lax.rsqrt(jnp.mean(y_t * y_t, axis=0, keepdims=True) + RMS_EPS)
    mixed_at = _gate(y_t * inv * gna_ref[...], z_t)
    mixed_f = _gate(_rms(yf_ref[...].astype(F32), gf_ref[...]), zf_ref[...])
    out = lax.dot_general(mixed_at, wa_ref[...], (((0,), (0,)), ((), ())),
                          preferred_element_type=F32)
    out = out + jnp.dot(mixed_f, wf_ref[...], preferred_element_type=F32)
    o_ref[...] = x_ref[...] + _rms(out, gp_ref[...])


def _out_proj_specs(t, tm, w_a, w_fo):
    tok = lambda i: (i, 0)
    blk = lambda i: (i, 0, 0)
    const = lambda i: (0, 0)
    in_specs = [pl.BlockSpec((tm // Q_TOK, NA_WIDTH, Q_TOK), blk),
                pl.BlockSpec((tm // Q_TOK, NA_WIDTH, Q_TOK), blk),
                pl.BlockSpec((tm, F_WIDTH), tok),
                pl.BlockSpec((tm, F_WIDTH), tok),
                pl.BlockSpec((tm, D_MODEL), tok),
                pl.BlockSpec(w_a.shape, const),
                pl.BlockSpec(w_fo.shape, const),
                pl.BlockSpec((NA_WIDTH, 1), const),
                pl.BlockSpec((1, F_WIDTH), const),
                pl.BlockSpec((1, D_MODEL), const)]
    return in_specs, pl.BlockSpec((tm, D_MODEL), tok), jax.ShapeDtypeStruct((t, D_MODEL), F32)


def _out_proj(o_t, za_t, y_f, z_f, x2d, w_a, w_fo, g_na_col, g_f, g_post, tm=TOKEN_BLOCK):
    t = x2d.shape[0]
    in_specs, out_spec, out_shape = _out_proj_specs(t, tm, w_a, w_fo)
    return pl.pallas_call(
        _out_proj_kernel,
        grid=(t // tm,),
        in_specs=in_specs,
        out_specs=out_spec,
        out_shape=out_shape,
        compiler_params=pltpu.CompilerParams(
            dimension_semantics=("arbitrary",), vmem_limit_bytes=VMEM_LIMIT),
        name="out_proj",
    )(o_t, za_t, y_f, z_f, x2d, w_a, w_fo, g_na_col, g_f, g_post)


def _in_out_proj_kernel(x_ref, g_ref, wn_ref, wt_ref,
                        ot_ref, zat_in_ref, yf_ref, zf_in_ref, xo_ref, wa_ref, wf_ref, gna_ref, gf_ref, gp_ref,
                        k_ref, uf_ref, zf_ref, qt_ref, vt_ref, zat_ref, o_ref):
    h = _rms(x_ref[...], g_ref[...]).astype(BF16)
    blocks = range(ot_ref.shape[0])
    y_t = jnp.concatenate([ot_ref[i] for i in blocks], axis=1).astype(F32)
    z_t = jnp.concatenate([zat_in_ref[i] for i in blocks], axis=1)
    nat = jnp.dot(h, wn_ref[...], preferred_element_type=F32)
    inv = lax.rsqrt(jnp.mean(y_t * y_t, axis=0, keepdims=True) + RMS_EPS)
    mixed_at = _gate(y_t * inv * gna_ref[...], z_t)
    mixed_f = _gate(_rms(yf_ref[...].astype(F32), gf_ref[...]), zf_in_ref[...])
    k_ref[...] = nat[:, 0 * NA_WIDTH:1 * NA_WIDTH].astype(BF16)
    uf_ref[...] = nat[:, 1 * NA_WIDTH:2 * NA_WIDTH].astype(BF16)
    zf_ref[...] = nat[:, 2 * NA_WIDTH:3 * NA_WIDTH].astype(BF16)
    out = lax.dot_general(mixed_at, wa_ref[...], (((0,), (0,)), ((), ())), preferred_element_type=F32)
    out = out + jnp.dot(mixed_f, wf_ref[...], preferred_element_type=F32)
    tr = lax.dot_general(wt_ref[...], h, (((1,), (1,)), ((), ())), preferred_element_type=F32)
    o_ref[...] = xo_ref[...] + _rms(out, gp_ref[...])
    for i in range(qt_ref.shape[0]):
        tok = slice(i * Q_TOK, (i + 1) * Q_TOK)
        qt_ref[i] = tr[0 * NA_WIDTH:1 * NA_WIDTH, tok].astype(BF16)
        vt_ref[i] = tr[1 * NA_WIDTH:2 * NA_WIDTH, tok].astype(BF16)
        zat_ref[i] = tr[2 * NA_WIDTH:3 * NA_WIDTH, tok].astype(BF16)


def _in_out_proj(in_args, out_args, tm=FUSED_TOKEN_BLOCK):
    x_in, _, w_nat, w_tr = in_args
    x_out, w_a, w_fo = out_args[4], out_args[5], out_args[6]
    t = x_in.shape[0]
    assert x_out.shape[0] == t
    in_specs_a, out_specs_a, out_shape_a = _in_proj_specs(t, tm, w_nat, w_tr)
    in_specs_b, out_spec_b, out_shape_b = _out_proj_specs(t, tm, w_a, w_fo)
    res = pl.pallas_call(
        _in_out_proj_kernel,
        grid=(t // tm,),
        in_specs=in_specs_a + in_specs_b,
        out_specs=out_specs_a + [out_spec_b],
        out_shape=out_shape_a + [out_shape_b],
        compiler_params=pltpu.CompilerParams(
            dimension_semantics=("arbitrary",), vmem_limit_bytes=VMEM_LIMIT),
        name="in_out_proj",
    )(*in_args, *out_args)
    return res[:N_IN_PROJ_OUTPUTS], res[-1]


def _mixers(proj, bias, w_c, b_f, batch, seq):
    k, u_f, z_f, q_t, v_t, za_t = proj
    o_t = _attention(q_t, k, v_t, bias, batch, seq)
    y_f = _fourier(u_f, w_c, b_f, batch, seq)
    return o_t, za_t, y_f, z_f


def kernel(x_prompt, x_sample, w_in, rpb, w_fourier, b_fourier, g_pre, g_na, g_f, w_out, g_post):
    depth = w_in.shape[0]
    y_prompt, y_sample = x_prompt, x_sample
    scale = NA_HEAD_DIM ** -0.5 * LOG2E
    for l in range(depth):
        w = w_in[l]
        w_nat = jnp.concatenate([w[:, i * NA_WIDTH:(i + 1) * NA_WIDTH] for i in (1, 4, 5)],
                                axis=1).astype(BF16)
        w_tr = _transposed_weights(w, (0, 2, 3), scale)
        bias = _bias_table(rpb[l])
        w_c = _fold_channel_dft(w_fourier[l])
        row = lambda v: v.reshape(1, -1).astype(F32)
        b_f, g_in = row(b_fourier[l]), row(g_pre[l])
        out_w = (w_out[l][:NA_WIDTH].astype(BF16), w_out[l][NA_WIDTH:].astype(BF16),
                 g_na[l].reshape(-1, 1).astype(F32), row(g_f[l]), row(g_post[l]))
        (bp, sp, _), (bs, ss, _) = y_prompt.shape, y_sample.shape
        xp, xs = y_prompt.reshape(bp * sp, D_MODEL), y_sample.reshape(bs * ss, D_MODEL)

        mixed_p = _mixers(_in_proj(xp, g_in, w_nat, w_tr), bias, w_c, b_f, bp, sp)
        if xp.shape[0] == xs.shape[0]:
            proj_s, out_p = _in_out_proj((xs, g_in, w_nat, w_tr), (*mixed_p, xp, *out_w))
        else:
            proj_s = _in_proj(xs, g_in, w_nat, w_tr)
            out_p = _out_proj(*mixed_p, xp, *out_w)
        mixed_s = _mixers(proj_s, bias, w_c, b_f, bs, ss)
        out_s = _out_proj(*mixed_s, xs, *out_w)
        y_prompt, y_sample = out_p.reshape(bp, sp, D_MODEL), out_s.reshape(bs, ss, D_MODEL)
    return (y_prompt, y_sample)
```

```python
import functools

import numpy as np
import jax
import jax.numpy as jnp
from jax import lax
from jax.experimental import pallas as pl
from jax.experimental.pallas import tpu as pltpu

D_MODEL = 1024
GRID_W = 64
WIN_ROWS = 8
WIN_COLS = 16
NA_HEADS = 8
NA_HEAD_DIM = 64
NA_WIDTH = NA_HEADS * NA_HEAD_DIM
F_GROUPS = 8
F_GROUP_DIM = 64
F_WIDTH = F_GROUPS * F_GROUP_DIM
RMS_EPS = 1e-6
NEG_INF = -1e30

Q_ROWS = 4
Q_TOK = Q_ROWS * GRID_W
STEP_SUBS = 4
KV_BLOCKS = STEP_SUBS + 2
KEY_ROWS = 3 * Q_ROWS
ROW_SLOTS = 2 * WIN_ROWS
DFT_MINOR = 128
TOKEN_BLOCK = 1024
FUSED_TOKEN_BLOCK = 512
VMEM_LIMIT = 56 * 1024 * 1024

BF16 = jnp.bfloat16
F32 = jnp.float32


def _rms(x, g):
    inv = lax.rsqrt(jnp.mean(x * x, axis=-1, keepdims=True) + RMS_EPS)
    return x * inv * g


def _silu(z):
    return z * (1.0 / (1.0 + jnp.exp(-z)))


def _gate(y_normed, z):
    return y_normed.astype(BF16) * _silu(z)


def _in_proj_kernel(x_ref, g_ref, wn_ref, wt_ref,
                    k_ref, uf_ref, zf_ref, qt_ref, vt_ref, zat_ref):
    h = _rms(x_ref[...], g_ref[...]).astype(BF16)
    nat = jnp.dot(h, wn_ref[...], preferred_element_type=F32)
    k_ref[...] = nat[:, 0 * NA_WIDTH:1 * NA_WIDTH].astype(BF16)
    uf_ref[...] = nat[:, 1 * NA_WIDTH:2 * NA_WIDTH].astype(BF16)
    zf_ref[...] = nat[:, 2 * NA_WIDTH:3 * NA_WIDTH].astype(BF16)
    tr = lax.dot_general(wt_ref[...], h, (((1,), (1,)), ((), ())),
                         preferred_element_type=F32)
    for i in range(qt_ref.shape[0]):
        tok = slice(i * Q_TOK, (i + 1) * Q_TOK)
        qt_ref[i] = tr[0 * NA_WIDTH:1 * NA_WIDTH, tok].astype(BF16)
        vt_ref[i] = tr[1 * NA_WIDTH:2 * NA_WIDTH, tok].astype(BF16)
        zat_ref[i] = tr[2 * NA_WIDTH:3 * NA_WIDTH, tok].astype(BF16)


def _transposed_weights_kernel(col_ref, w_ref, o_ref, *, scale):
    del col_ref
    factor = jnp.where(pl.program_id(0) == 0, scale, 1.0).astype(F32)
    o_ref[...] = (w_ref[...] * factor).T.astype(BF16)


def _transposed_weights(w, groups, scale):
    d, width = w.shape[0], NA_WIDTH
    col = jnp.asarray(groups, jnp.int32)
    return pl.pallas_call(
        functools.partial(_transposed_weights_kernel, scale=scale),
        grid_spec=pltpu.PrefetchScalarGridSpec(
            num_scalar_prefetch=1,
            grid=(len(groups),),
            in_specs=[pl.BlockSpec((d, width), lambda i, col: (0, col[i]))],
            out_specs=pl.BlockSpec((width, d), lambda i, col: (i, 0))),
        out_shape=jax.ShapeDtypeStruct((len(groups) * width, d), BF16),
        name="transposed_weights",
    )(col, w)


N_IN_PROJ_OUTPUTS = 6


def _in_proj_specs(t, tm, w_nat, w_tr):
    tok = lambda i: (i, 0)
    blk = lambda i: (i, 0, 0)
    const = lambda i: (0, 0)
    nat_shape = jax.ShapeDtypeStruct((t, NA_WIDTH), BF16)
    tr_shape = jax.ShapeDtypeStruct((t // Q_TOK, NA_WIDTH, Q_TOK), BF16)
    in_specs = [pl.BlockSpec((tm, D_MODEL), tok),
                pl.BlockSpec((1, D_MODEL), const),
                pl.BlockSpec(w_nat.shape, const),
                pl.BlockSpec(w_tr.shape, const)]
    out_specs = ([pl.BlockSpec((tm, NA_WIDTH), tok)] * 3
                 + [pl.BlockSpec((tm // Q_TOK, NA_WIDTH, Q_TOK), blk)] * 3)
    return in_specs, out_specs, [nat_shape] * 3 + [tr_shape] * 3


def _in_proj(x2d, g_pre, w_nat, w_tr, tm=TOKEN_BLOCK):
    t = x2d.shape[0]
    in_specs, out_specs, out_shape = _in_proj_specs(t, tm, w_nat, w_tr)
    return pl.pallas_call(
        _in_proj_kernel,
        grid=(t // tm,),
        in_specs=in_specs,
        out_specs=out_specs,
        out_shape=out_shape,
        compiler_params=pltpu.CompilerParams(
            dimension_semantics=("arbitrary",), vmem_limit_bytes=VMEM_LIMIT),
        name="in_proj",
    )(x2d, g_pre, w_nat, w_tr)


def _bias_table_kernel(rpb_ref, t2_ref):
    head = pl.program_id(0)
    shape = (GRID_W, 2 * GRID_W)
    kc = lax.broadcasted_iota(jnp.int32, shape, 0)
    lane = lax.broadcasted_iota(jnp.int32, shape, 1)
    second = lane >= GRID_W
    qc = jnp.where(second, lane - GRID_W, lane)
    col_off = kc - qc + (WIN_COLS - 1)
    win_start = jnp.clip(qc - WIN_COLS // 2, 0, GRID_W - WIN_COLS)
    col_valid = (kc >= win_start) & (kc < win_start + WIN_COLS)
    n_off = 2 * WIN_COLS - 1
    n_row = 2 * WIN_ROWS - 1

    def entry(d, o):
        if 0 <= d < n_row:
            return rpb_ref[(head * n_row + d) * n_off + o]
        return jnp.float32(NEG_INF)

    second_row = second[:1]
    for d in range(ROW_SLOTS):
        tile = jnp.full(shape, NEG_INF, F32)
        for o in range(n_off):
            val = jnp.where(second_row, entry(d - 1, o), entry(d, o))
            tile = jnp.where(col_off == o, val, tile)
        t2_ref[0, d] = jnp.where(col_valid, tile, NEG_INF)


def _bias_table(rpb):
    return pl.pallas_call(
        _bias_table_kernel,
        grid=(NA_HEADS,),
        in_specs=[pl.BlockSpec(memory_space=pltpu.SMEM)],
        out_specs=pl.BlockSpec((1, ROW_SLOTS, GRID_W, 2 * GRID_W), lambda h: (h, 0, 0, 0)),
        out_shape=jax.ShapeDtypeStruct((NA_HEADS, ROW_SLOTS, GRID_W, 2 * GRID_W), F32),
        compiler_params=pltpu.CompilerParams(dimension_semantics=("arbitrary",)),
        name="bias_table",
    )(rpb.astype(F32).reshape(-1))


def _token_range(refs, lo, hi, other, axis):
    pieces = []
    for idx, ref in enumerate(refs):
        a, b = max(lo, idx * Q_TOK), min(hi, (idx + 1) * Q_TOK)
        if a < b:
            tok = slice(a - idx * Q_TOK, b - idx * Q_TOK)
            pieces.append(ref[tok, other] if axis == 0 else ref[0, other, tok])
    return pieces[0] if len(pieces) == 1 else jnp.concatenate(pieces, axis=axis)


def _window_plan(variant, jp):
    if variant == 0:
        return [(i, i - 2 * jp + WIN_ROWS - 1, None) for i in range(WIN_ROWS)]
    if variant == 2:
        q_row = KEY_ROWS - Q_ROWS + 2 * jp
        return [(i, i - q_row + WIN_ROWS - 1, None) for i in range(KEY_ROWS - WIN_ROWS, KEY_ROWS)]
    q_row = Q_ROWS + 2 * jp
    plan = []
    for i in range(q_row - WIN_ROWS // 2, q_row + WIN_ROWS // 2 + 1):
        half = "low" if i == q_row - WIN_ROWS // 2 else "high" if i == q_row + WIN_ROWS // 2 else None
        plan.append((i, i - q_row + WIN_ROWS - 1, half))
    return plan


def _step_plan(position):
    plan = []
    for j in range(STEP_SUBS):
        if position == 0:
            plan.append((0, 0) if j == 0 else (1, (j - 1) * Q_ROWS))
        elif position == 1:
            plan.append((1, j * Q_ROWS))
        else:
            last = j == STEP_SUBS - 1
            plan.append((2 if last else 1, (j + 1 - last) * Q_ROWS))
    return tuple(plan)


def _attention_kernel(qt_ref, *refs, steps):
    k_refs, v_refs = refs[:KV_BLOCKS], refs[KV_BLOCKS:2 * KV_BLOCKS]
    t2_ref, out_ref = refs[2 * KV_BLOCKS:]
    step = pl.program_id(0)
    pair_lanes = 2 * NA_HEAD_DIM
    q_lanes = 2 * GRID_W
    n_pairs = NA_HEADS // 2

    def sub_group(sub, variant, key_off):
        plans = [_window_plan(variant, jp) for jp in range(Q_ROWS // 2)]
        key_lo = min(p[0][0] for p in plans)
        key_top = max(p[-1][0] for p in plans) + 1
        key_hi = key_top + (key_top - key_lo) % 2
        tok_lo, tok_top, tok_hi = ((key_off + r) * GRID_W for r in (key_lo, key_top, key_hi))
        lane = lax.broadcasted_iota(jnp.int32, (1, q_lanes), 1)
        half_mask = {"low": jnp.where(lane < GRID_W, 0.0, NEG_INF).astype(F32),
                     "high": jnp.where(lane >= GRID_W, 0.0, NEG_INF).astype(F32)}
        ones_rows = jnp.ones((16, tok_hi - tok_lo), BF16)
        zero_blk = jnp.zeros((GRID_W, q_lanes), BF16)

        def scores(pair):
            cols = slice(pair * pair_lanes, (pair + 1) * pair_lanes)
            k_pair = _token_range(k_refs, tok_lo, tok_top, cols, 0)
            qt_pair = qt_ref[sub, cols, :]
            zero = jnp.zeros((NA_HEAD_DIM, Q_TOK), BF16)
            qt_both = jnp.concatenate(
                [jnp.concatenate([qt_pair[:NA_HEAD_DIM], zero], axis=0),
                 jnp.concatenate([zero, qt_pair[NA_HEAD_DIM:]], axis=0)], axis=1)
            s = jnp.dot(k_pair, qt_both, preferred_element_type=F32)
            tiles = [(head_sub, jp) for head_sub in range(2) for jp in range(len(plans))]
            m_acc = [None] * len(tiles)
            biased = [{} for _ in tiles]
            for i in range(key_lo, key_top):
                rows = slice((i - key_lo) * GRID_W, (i - key_lo + 1) * GRID_W)
                for t, (head_sub, jp) in enumerate(tiles):
                    entry = [e for e in plans[jp] if e[0] == i]
                    if not entry:
                        continue
                    _, slot_d, half = entry[0]
                    first = head_sub * Q_TOK + jp * q_lanes
                    lanes = slice(first, first + q_lanes)
                    blk = s[rows, lanes] + t2_ref[2 * pair + head_sub, slot_d]
                    if half is not None:
                        blk = blk + half_mask[half]
                    biased[t][i] = blk
                    blk_max = jnp.max(blk.reshape(GRID_W // 8, 8, q_lanes), axis=0)
                    m_acc[t] = blk_max if m_acc[t] is None else jnp.maximum(m_acc[t], blk_max)
            maxima = [jnp.max(m, axis=0, keepdims=True) for m in m_acc]
            return maxima, biased

        def softmax_pv(pair, state):
            maxima, biased = state
            for head_sub in range(2):
                head = 2 * pair + head_sub
                cols = []
                for jp in range(len(plans)):
                    t = 2 * head_sub + jp
                    blocks = [jnp.exp(biased[t][i] - maxima[t]).astype(BF16) if i in biased[t] else zero_blk
                              for i in range(key_lo, key_hi)]
                    cols.append(jnp.concatenate(blocks, axis=0))
                p_head = jnp.concatenate(cols, axis=1)
                rows = slice(head * NA_HEAD_DIM, (head + 1) * NA_HEAD_DIM)
                vt_head = _token_range(v_refs, tok_lo, tok_hi, rows, 1)
                vt_ext = jnp.concatenate([vt_head, ones_rows], axis=0)
                o = jnp.dot(vt_ext, p_head, preferred_element_type=F32)
                out_ref[sub, rows, :] = (o[:NA_HEAD_DIM] * (1.0 / o[NA_HEAD_DIM:NA_HEAD_DIM + 1])).astype(BF16)

        return scores, softmax_pv

    def body(step_variant):
        fns = [sub_group(sub, *plan) for sub, plan in enumerate(_step_plan(step_variant))]
        stages = [(sub, pair) for sub in range(len(fns)) for pair in range(n_pairs)]
        state = fns[0][0](0)
        for idx, (sub, pair) in enumerate(stages):
            nxt = None
            if idx + 1 < len(stages):
                nxt_sub, nxt_pair = stages[idx + 1]
                nxt = fns[nxt_sub][0](nxt_pair)
            fns[sub][1](pair, state)
            state = nxt

    is_top = step == 0
    is_bot = step == steps - 1
    pl.when(is_top)(lambda: body(0))
    pl.when(jnp.logical_not(is_top | is_bot))(lambda: body(1))
    pl.when(is_bot)(lambda: body(2))


def _attention(qt, k, vt, t2, batch, seq):
    t = batch * seq
    blocks = seq // Q_TOK
    steps = blocks // STEP_SUBS
    assert steps >= 2 and blocks >= KV_BLOCKS

    def first_kv(u):
        return jnp.clip(STEP_SUBS * u - 1, 0, blocks - KV_BLOCKS)

    def blk(u, b):
        return (b * steps + u, 0, 0)

    def kv(i):
        return lambda u, b: (b * blocks + first_kv(u) + i, 0)

    def kv_blk(i):
        return lambda u, b: (b * blocks + first_kv(u) + i, 0, 0)

    return pl.pallas_call(
        functools.partial(_attention_kernel, steps=steps),
        grid=(steps, batch),
        in_specs=[pl.BlockSpec((STEP_SUBS, NA_WIDTH, Q_TOK), blk)]
                 + [pl.BlockSpec((Q_TOK, NA_WIDTH), kv(i)) for i in range(KV_BLOCKS)]
                 + [pl.BlockSpec((1, NA_WIDTH, Q_TOK), kv_blk(i)) for i in range(KV_BLOCKS)]
                 + [pl.BlockSpec(t2.shape, lambda u, b: (0, 0, 0, 0))],
        out_specs=pl.BlockSpec((STEP_SUBS, NA_WIDTH, Q_TOK), blk),
        out_shape=jax.ShapeDtypeStruct((t // Q_TOK, NA_WIDTH, Q_TOK), BF16),
        compiler_params=pltpu.CompilerParams(
            dimension_semantics=("arbitrary", "arbitrary"),
            vmem_limit_bytes=VMEM_LIMIT),
        name="attention",
    )(qt, *([k] * KV_BLOCKS), *([vt] * KV_BLOCKS), t2)


DFT_CHUNK = 16
LANES = 128


def _to_lane_tiles(scr, x):
    for c in range(scr.shape[0]):
        scr[c] = x[:, c * LANES:(c + 1) * LANES]


def _from_lane_tiles(scr):
    return jnp.concatenate([scr[c] for c in range(scr.shape[0])], axis=1)


def _strided_rows(scr, start, size, stride):
    return jnp.concatenate([scr[c, pl.ds(start, size, stride=stride), :] for c in range(scr.shape[0])], axis=1)


def _store_strided_rows(scr, start, stride, x):
    for c in range(scr.shape[0]):
        scr[c, pl.ds(start, x.shape[0], stride=stride), :] = x[:, c * LANES:(c + 1) * LANES]


def _fourier_1_kernel(w_ref, u_ref, a_ref, *scratch, major, chunks):
    if not scratch:
        for c in range(chunks):
            rows = slice(c * DFT_CHUNK, (c + 1) * DFT_CHUNK)
            u2d = u_ref[0, :, rows, :].reshape(major * DFT_CHUNK, F_WIDTH)
            res = jnp.dot(w_ref[...], u2d, preferred_element_type=F32)
            a_ref[0, :, :, rows, :] = res.astype(BF16).reshape(major, 2, DFT_CHUNK, F_WIDTH)
        return
    in_w, out_w = scratch
    pairs = DFT_CHUNK // 2
    _to_lane_tiles(in_w, pltpu.bitcast(u_ref[0].reshape(major * DFT_CHUNK, F_WIDTH), jnp.uint32))
    for j in range(pairs):
        rhs = pltpu.bitcast(_strided_rows(in_w, j, major, pairs), BF16)
        res = jnp.dot(w_ref[...], rhs, preferred_element_type=F32)
        _store_strided_rows(out_w, j, pairs, pltpu.bitcast(res.astype(BF16), jnp.uint32))
    a_ref[0] = pltpu.bitcast(_from_lane_tiles(out_w), BF16).reshape(major, 2, DFT_CHUNK, F_WIDTH)


def _fourier_1(u4, w1, kron):
    batch, major, minor, _ = u4.shape
    chunks = minor // DFT_CHUNK if kron else 1
    tb = chunks * DFT_CHUNK
    scratch = [] if kron else [pltpu.VMEM((F_WIDTH // LANES, major * DFT_CHUNK // 2, LANES), jnp.uint32),
                               pltpu.VMEM((F_WIDTH // LANES, major * DFT_CHUNK, LANES), jnp.uint32)]
    return pl.pallas_call(
        functools.partial(_fourier_1_kernel, major=major, chunks=chunks),
        grid=(batch, minor // tb),
        in_specs=[pl.BlockSpec(w1.shape, lambda b, j: (0, 0)),
                  pl.BlockSpec((1, major, tb, F_WIDTH), lambda b, j: (b, 0, j, 0))],
        out_specs=pl.BlockSpec((1, major, 2, tb, F_WIDTH), lambda b, j: (b, 0, 0, j, 0)),
        out_shape=jax.ShapeDtypeStruct((batch, major, 2, minor, F_WIDTH), BF16),
        scratch_shapes=scratch,
        compiler_params=pltpu.CompilerParams(
            dimension_semantics=("arbitrary", "arbitrary"),
            vmem_limit_bytes=VMEM_LIMIT),
        name="fourier_1",
    )(w1, u4)


def _fourier_2_kernel(a_ref, m_ref, wc_ref, bf_ref, y_ref, x_scr, y_scr):
    for j in range(DFT_CHUNK):
        rhs = a_ref[0, j].reshape(2 * DFT_MINOR, F_WIDTH)
        x = jnp.dot(m_ref[j], rhs, preferred_element_type=F32)
        rows = slice(j * DFT_MINOR, (j + 1) * DFT_MINOR)
        x_scr[rows, :F_WIDTH] = x[:DFT_MINOR].astype(BF16)
        x_scr[rows, F_WIDTH:] = x[DFT_MINOR:].astype(BF16)
    y = jnp.dot(x_scr[...], wc_ref[...], preferred_element_type=F32) + bf_ref[...]
    pitch = DFT_CHUNK + 1
    for j in range(DFT_CHUNK):
        _store_strided_rows(y_scr, j, pitch, y[j * DFT_MINOR:(j + 1) * DFT_MINOR])
    packed = jnp.concatenate(
        [jnp.concatenate([y_scr[c, p * pitch:p * pitch + DFT_CHUNK, :] for p in range(DFT_MINOR)], axis=0)
         for c in range(y_scr.shape[0])], axis=1)
    y_ref[0] = packed.astype(BF16).reshape(DFT_MINOR, DFT_CHUNK, F_WIDTH)


def _fold_channel_dft_kernel(cs_ref, wf_ref, wc_ref):
    wc_ref[...] = jnp.dot(cs_ref[...], wf_ref[...], preferred_element_type=F32,
                          precision=lax.Precision.HIGHEST).astype(BF16)


def _fold_channel_dft(w_f):
    c = np.arange(F_GROUP_DIM)
    ang = 2.0 * np.pi * np.outer(c, c) / F_GROUP_DIM
    eye = np.eye(F_GROUPS)
    cs = np.concatenate([np.kron(eye, np.cos(ang)), np.kron(eye, np.sin(ang))], axis=0)
    return pl.pallas_call(
        _fold_channel_dft_kernel,
        out_shape=jax.ShapeDtypeStruct((2 * F_WIDTH, F_WIDTH), BF16),
        name="fold_channel_dft",
    )(jnp.asarray(cs, F32), w_f.astype(F32))


def _fourier_2(a5, m_tab, w_c, b_f):
    batch, major, _, minor, _ = a5.shape
    tr = DFT_CHUNK
    return pl.pallas_call(
        _fourier_2_kernel,
        grid=(batch, major // tr),
        in_specs=[pl.BlockSpec((1, tr, 2, minor, F_WIDTH), lambda b, r: (b, r, 0, 0, 0)),
                  pl.BlockSpec((tr, 2 * minor, 2 * minor), lambda b, r: (r, 0, 0)),
                  pl.BlockSpec(w_c.shape, lambda b, r: (0, 0)),
                  pl.BlockSpec((1, F_WIDTH), lambda b, r: (0, 0))],
        out_specs=pl.BlockSpec((1, minor, tr, F_WIDTH), lambda b, r: (b, 0, r, 0)),
        out_shape=jax.ShapeDtypeStruct((batch, minor, major, F_WIDTH), BF16),
        scratch_shapes=[pltpu.VMEM((tr * minor, 2 * F_WIDTH), BF16),
                        pltpu.VMEM((F_WIDTH // LANES, minor * (tr + 1), LANES), F32)],
        compiler_params=pltpu.CompilerParams(
            dimension_semantics=("arbitrary", "arbitrary"),
            vmem_limit_bytes=VMEM_LIMIT),
        name="fourier_2",
    )(a5, m_tab, w_c, b_f)


@functools.lru_cache(maxsize=None)
def _dft_constants(seq):
    major = seq // DFT_MINOR
    kron = major * DFT_CHUNK <= 256
    a = np.arange(major)
    ang1 = 2.0 * np.pi * np.outer(a, a) / major
    w1 = np.stack([np.cos(ang1), -np.sin(ang1)], axis=1).reshape(2 * major, major)
    w1 = np.kron(w1, np.eye(DFT_CHUNK if kron else 2))
    r = np.arange(major)[:, None, None]
    p = np.arange(DFT_MINOR)[None, :, None]
    b = np.arange(DFT_MINOR)[None, None, :]
    ang2 = 2.0 * np.pi * ((b * (r + major * p)) % seq) / seq
    scale = 1.0 / np.sqrt(seq * F_GROUP_DIM)
    e_re, e_im = np.cos(ang2) * scale, -np.sin(ang2) * scale
    m_tab = np.concatenate([np.concatenate([e_re, -e_im], axis=2),
                            np.concatenate([e_im, e_re], axis=2)], axis=1)
    return kron, np.asarray(w1, np.float32), np.asarray(m_tab, np.float32)


def _fourier_fused_kernel(w1_ref, u_ref, m_ref, wc_ref, bf_ref, y_ref, a_scr, *scratch, major, chunks):
    _fourier_1_kernel(w1_ref, u_ref, a_scr, major=major, chunks=chunks)
    _fourier_2_kernel(a_scr, m_ref, wc_ref, bf_ref, y_ref, *scratch)


def _fourier_fused(u4, w1, m_tab, w_c, b_f):
    batch, major, minor, _ = u4.shape
    const2 = lambda b: (0, 0)
    return pl.pallas_call(
        functools.partial(_fourier_fused_kernel, major=major, chunks=minor // DFT_CHUNK),
        grid=(batch,),
        in_specs=[pl.BlockSpec(w1.shape, const2),
                  pl.BlockSpec((1, major, minor, F_WIDTH), lambda b: (b, 0, 0, 0)),
                  pl.BlockSpec(m_tab.shape, lambda b: (0, 0, 0)),
                  pl.BlockSpec(w_c.shape, const2),
                  pl.BlockSpec((1, F_WIDTH), const2)],
        out_specs=pl.BlockSpec((1, minor, major, F_WIDTH), lambda b: (b, 0, 0, 0)),
        out_shape=jax.ShapeDtypeStruct((batch, minor, major, F_WIDTH), BF16),
        scratch_shapes=[pltpu.VMEM((1, major, 2, minor, F_WIDTH), BF16),
                        pltpu.VMEM((major * minor, 2 * F_WIDTH), BF16),
                        pltpu.VMEM((F_WIDTH // LANES, minor * (major + 1), LANES), F32)],
        compiler_params=pltpu.CompilerParams(
            dimension_semantics=("arbitrary",), vmem_limit_bytes=VMEM_LIMIT),
        name="fourier_fused",
    )(w1, u4, m_tab, w_c, b_f)


def _fourier(u, w_c, b_f, batch, seq):
    major = seq // DFT_MINOR
    kron, w1, m_tab = _dft_constants(seq)
    u4 = u.reshape(batch, major, DFT_MINOR, F_WIDTH)
    w1, m_tab = jnp.asarray(w1).astype(BF16), jnp.asarray(m_tab).astype(BF16)
    if kron and major == DFT_CHUNK:
        return _fourier_fused(u4, w1, m_tab, w_c, b_f).reshape(batch * seq, F_WIDTH)
    a = _fourier_1(u4, w1, kron)
    y = _fourier_2(a, m_tab, w_c, b_f)
    return y.reshape(batch * seq, F_WIDTH)


def _out_proj_kernel(ot_ref, zat_ref, yf_ref, zf_ref, x_ref, wa_ref, wf_ref,
                     gna_ref, gf_ref, gp_ref, o_ref):
    blocks = range(ot_ref.shape[0])
    y_t = jnp.concatenate([ot_ref[i] for i in blocks], axis=1).astype(F32)
    z_t = jnp.concatenate([zat_ref[i] for i in blocks], axis=1)
    inv = lax.rsqrt(jnp.mean(y_t * y_t, axis=0, keepdims=True) + RMS_EPS)
    mixed_at = _gate(y_t * inv * gna_ref[...], z_t)
    mixed_f = _gate(_rms(yf_ref[...].astype(F32), gf_ref[...]), zf_ref[...])
    out = lax.dot_general(mixed_at, wa_ref[...], (((0,), (0,)), ((), ())),
                          preferred_element_type=F32)
    out = out + jnp.dot(mixed_f, wf_ref[...], preferred_element_type=F32)
    o_ref[...] = x_ref[...] + _rms(out, gp_ref[...])


def _out_proj_specs(t, tm, w_a, w_fo):
    tok = lambda i: (i, 0)
    blk = lambda i: (i, 0, 0)
    const = lambda i: (0, 0)
    in_specs = [pl.BlockSpec((tm // Q_TOK, NA_WIDTH, Q_TOK), blk),
                pl.BlockSpec((tm // Q_TOK, NA_WIDTH, Q_TOK), blk),
                pl.BlockSpec((tm, F_WIDTH), tok),
                pl.BlockSpec((tm, F_WIDTH), tok),
                pl.BlockSpec((tm, D_MODEL), tok),
                pl.BlockSpec(w_a.shape, const),
                pl.BlockSpec(w_fo.shape, const),
                pl.BlockSpec((NA_WIDTH, 1), const),
                pl.BlockSpec((1, F_WIDTH), const),
                pl.BlockSpec((1, D_MODEL), const)]
    return in_specs, pl.BlockSpec((tm, D_MODEL), tok), jax.ShapeDtypeStruct((t, D_MODEL), F32)


def _out_proj(o_t, za_t, y_f, z_f, x2d, w_a, w_fo, g_na_col, g_f, g_post, tm=TOKEN_BLOCK):
    t = x2d.shape[0]
    in_specs, out_spec, out_shape = _out_proj_specs(t, tm, w_a, w_fo)
    return pl.pallas_call(
        _out_proj_kernel,
        grid=(t // tm,),
        in_specs=in_specs,
        out_specs=out_spec,
        out_shape=out_shape,
        compiler_params=pltpu.CompilerParams(
            dimension_semantics=("arbitrary",), vmem_limit_bytes=VMEM_LIMIT),
        name="out_proj",
    )(o_t, za_t, y_f, z_f, x2d, w_a, w_fo, g_na_col, g_f, g_post)


def _in_out_proj_kernel(x_ref, g_ref, wn_ref, wt_ref,
                        ot_ref, zat_in_ref, yf_ref, zf_in_ref, xo_ref, wa_ref, wf_ref, gna_ref, gf_ref, gp_ref,
                        k_ref, uf_ref, zf_ref, qt_ref, vt_ref, zat_ref, o_ref):
    h = _rms(x_ref[...], g_ref[...]).astype(BF16)
    blocks = range(ot_ref.shape[0])
    y_t = jnp.concatenate([ot_ref[i] for i in blocks], axis=1).astype(F32)
    z_t = jnp.concatenate([zat_in_ref[i] for i in blocks], axis=1)
    nat = jnp.dot(h, wn_ref[...], preferred_element_type=F32)
    inv = lax.rsqrt(jnp.mean(y_t * y_t, axis=0, keepdims=True) + RMS_EPS)
    mixed_at = _gate(y_t * inv * gna_ref[...], z_t)
    mixed_f = _gate(_rms(yf_ref[...].astype(F32), gf_ref[...]), zf_in_ref[...])
    k_ref[...] = nat[:, 0 * NA_WIDTH:1 * NA_WIDTH].astype(BF16)
    uf_ref[...] = nat[:, 1 * NA_WIDTH:2 * NA_WIDTH].astype(BF16)
    zf_ref[...] = nat[:, 2 * NA_WIDTH:3 * NA_WIDTH].astype(BF16)
    out = lax.dot_general(mixed_at, wa_ref[...], (((0,), (0,)), ((), ())), preferred_element_type=F32)
    out = out + jnp.dot(mixed_f, wf_ref[...], preferred_element_type=F32)
    tr = lax.dot_general(wt_ref[...], h, (((1,), (1,)), ((), ())), preferred_element_type=F32)
    o_ref[...] = xo_ref[...] + _rms(out, gp_ref[...])
    for i in range(qt_ref.shape[0]):
        tok = slice(i * Q_TOK, (i + 1) * Q_TOK)
        qt_ref[i] = tr[0 * NA_WIDTH:1 * NA_WIDTH, tok].astype(BF16)
        vt_ref[i] = tr[1 * NA_WIDTH:2 * NA_WIDTH, tok].astype(BF16)
        zat_ref[i] = tr[2 * NA_WIDTH:3 * NA_WIDTH, tok].astype(BF16)


def _in_out_proj(in_args, out_args, tm=FUSED_TOKEN_BLOCK):
    x_in, _, w_nat, w_tr = in_args
    x_out, w_a, w_fo = out_args[4], out_args[5], out_args[6]
    t = x_in.shape[0]
    assert x_out.shape[0] == t
    in_specs_a, out_specs_a, out_shape_a = _in_proj_specs(t, tm, w_nat, w_tr)
    in_specs_b, out_spec_b, out_shape_b = _out_proj_specs(t, tm, w_a, w_fo)
    res = pl.pallas_call(
        _in_out_proj_kernel,
        grid=(t // tm,),
        in_specs=in_specs_a + in_specs_b,
        out_specs=out_specs_a + [out_spec_b],
        out_shape=out_shape_a + [out_shape_b],
        compiler_params=pltpu.CompilerParams(
            dimension_semantics=("arbitrary",), vmem_limit_bytes=VMEM_LIMIT),
        name="in_out_proj",
    )(*in_args, *out_args)
    return res[:N_IN_PROJ_OUTPUTS], res[-1]


def _mixers(proj, bias, w_c, b_f, batch, seq):
    k, u_f, z_f, q_t, v_t, za_t = proj
    o_t = _attention(q_t, k, v_t, bias, batch, seq)
    y_f = _fourier(u_f, w_c, b_f, batch, seq)
    return o_t, za_t, y_f, z_f


def kernel(x_prompt, x_sample, w_in, rpb, w_fourier, b_fourier, g_pre, g_na, g_f, w_out, g_post):
    depth = w_in.shape[0]
    y_prompt, y_sample = x_prompt, x_sample
    scale = NA_HEAD_DIM ** -0.5
    for l in range(depth):
        w = w_in[l]
        w_nat = jnp.concatenate([w[:, i * NA_WIDTH:(i + 1) * NA_WIDTH] for i in (1, 4, 5)],
                                axis=1).astype(BF16)
        w_tr = _transposed_weights(w, (0, 2, 3), scale)
        bias = _bias_table(rpb[l])
        w_c = _fold_channel_dft(w_fourier[l])
        row = lambda v: v.reshape(1, -1).astype(F32)
        b_f, g_in = row(b_fourier[l]), row(g_pre[l])
        out_w = (w_out[l][:NA_WIDTH].astype(BF16), w_out[l][NA_WIDTH:].astype(BF16),
                 g_na[l].reshape(-1, 1).astype(F32), row(g_f[l]), row(g_post[l]))
        (bp, sp, _), (bs, ss, _) = y_prompt.shape, y_sample.shape
        xp, xs = y_prompt.reshape(bp * sp, D_MODEL), y_sample.reshape(bs * ss, D_MODEL)

        mixed_p = _mixers(_in_proj(xp, g_in, w_nat, w_tr), bias, w_c, b_f, bp, sp)
        if xp.shape[0] == xs.shape[0]:
            proj_s, out_p = _in_out_proj((xs, g_in, w_nat, w_tr), (*mixed_p, xp, *out_w))
        else:
            proj_s = _in_proj(xs, g_in, w_nat, w_tr)
            out_p = _out_proj(*mixed_p, xp, *out_w)
        mixed_s = _mixers(proj_s, bias, w_c, b_f, bs, ss)
        out_s = _out_proj(*mixed_s, xs, *out_w)
        y_prompt, y_sample = out_p.reshape(bp, sp, D_MODEL), out_s.reshape(bs, ss, D_MODEL)
    return (y_prompt, y_sample)
```

```python
import functools

import numpy as np
import jax
import jax.numpy as jnp
from jax import lax
from jax.experimental import pallas as pl
from jax.experimental.pallas import tpu as pltpu

D_MODEL = 1024
GRID_W = 64
WIN_ROWS = 8
WIN_COLS = 16
NA_HEADS = 8
NA_HEAD_DIM = 64
NA_WIDTH = NA_HEADS * NA_HEAD_DIM
F_GROUPS = 8
F_GROUP_DIM = 64
F_WIDTH = F_GROUPS * F_GROUP_DIM
RMS_EPS = 1e-6
NEG_INF = -1e30

Q_ROWS = 4
Q_TOK = Q_ROWS * GRID_W
STEP_SUBS = 4
KV_BLOCKS = STEP_SUBS + 2
KEY_ROWS = 3 * Q_ROWS
ROW_SLOTS = 2 * WIN_ROWS
DFT_MINOR = 128
TOKEN_BLOCK = 1024
FUSED_TOKEN_BLOCK = 512
VMEM_LIMIT = 56 * 1024 * 1024

BF16 = jnp.bfloat16
F32 = jnp.float32


def _rms(x, g):
    inv = lax.rsqrt(jnp.mean(x * x, axis=-1, keepdims=True) + RMS_EPS)
    return x * inv * g


def _silu(z):
    return z * (1.0 / (1.0 + jnp.exp(-z)))


def _gate(y_normed, z):
    return y_normed.astype(BF16) * _silu(z)


def _in_proj_kernel(x_ref, g_ref, wn_ref, wt_ref,
                    k_ref, uf_ref, zf_ref, qt_ref, vt_ref, zat_ref):
    h = _rms(x_ref[...], g_ref[...]).astype(BF16)
    nat = jnp.dot(h, wn_ref[...], preferred_element_type=F32)
    k_ref[...] = nat[:, 0 * NA_WIDTH:1 * NA_WIDTH].astype(BF16)
    uf_ref[...] = nat[:, 1 * NA_WIDTH:2 * NA_WIDTH].astype(BF16)
    zf_ref[...] = nat[:, 2 * NA_WIDTH:3 * NA_WIDTH].astype(BF16)
    tr = lax.dot_general(wt_ref[...], h, (((1,), (1,)), ((), ())),
                         preferred_element_type=F32)
    for i in range(qt_ref.shape[0]):
        tok = slice(i * Q_TOK, (i + 1) * Q_TOK)
        qt_ref[i] = tr[0 * NA_WIDTH:1 * NA_WIDTH, tok].astype(BF16)
        vt_ref[i] = tr[1 * NA_WIDTH:2 * NA_WIDTH, tok].astype(BF16)
        zat_ref[i] = tr[2 * NA_WIDTH:3 * NA_WIDTH, tok].astype(BF16)


def _transposed_weights_kernel(col_ref, w_ref, o_ref, *, scale):
    del col_ref
    factor = jnp.where(pl.program_id(0) == 0, scale, 1.0).astype(F32)
    o_ref[...] = (w_ref[...] * factor).T.astype(BF16)


def _transposed_weights(w, groups, scale):
    d, width = w.shape[0], NA_WIDTH
    col = jnp.asarray(groups, jnp.int32)
    return pl.pallas_call(
        functools.partial(_transposed_weights_kernel, scale=scale),
        grid_spec=pltpu.PrefetchScalarGridSpec(
            num_scalar_prefetch=1,
            grid=(len(groups),),
            in_specs=[pl.BlockSpec((d, width), lambda i, col: (0, col[i]))],
            out_specs=pl.BlockSpec((width, d), lambda i, col: (i, 0))),
        out_shape=jax.ShapeDtypeStruct((len(groups) * width, d), BF16),
        name="transposed_weights",
    )(col, w)


N_IN_PROJ_INPUTS = 4
N_IN_PROJ_OUTPUTS = 6


def _in_proj_specs(t, tm, w_nat, w_tr):
    tok = lambda i: (i, 0)
    blk = lambda i: (i, 0, 0)
    const = lambda i: (0, 0)
    nat_shape = jax.ShapeDtypeStruct((t, NA_WIDTH), BF16)
    tr_shape = jax.ShapeDtypeStruct((t // Q_TOK, NA_WIDTH, Q_TOK), BF16)
    in_specs = [pl.BlockSpec((tm, D_MODEL), tok),
                pl.BlockSpec((1, D_MODEL), const),
                pl.BlockSpec(w_nat.shape, const),
                pl.BlockSpec(w_tr.shape, const)]
    out_specs = ([pl.BlockSpec((tm, NA_WIDTH), tok)] * 3
                 + [pl.BlockSpec((tm // Q_TOK, NA_WIDTH, Q_TOK), blk)] * 3)
    return in_specs, out_specs, [nat_shape] * 3 + [tr_shape] * 3


def _in_proj(x2d, g_pre, w_nat, w_tr, tm=TOKEN_BLOCK):
    t = x2d.shape[0]
    in_specs, out_specs, out_shape = _in_proj_specs(t, tm, w_nat, w_tr)
    return pl.pallas_call(
        _in_proj_kernel,
        grid=(t // tm,),
        in_specs=in_specs,
        out_specs=out_specs,
        out_shape=out_shape,
        compiler_params=pltpu.CompilerParams(
            dimension_semantics=("arbitrary",), vmem_limit_bytes=VMEM_LIMIT),
        name="in_proj",
    )(x2d, g_pre, w_nat, w_tr)


def _bias_table_kernel(rpb_ref, t2_ref):
    head = pl.program_id(0)
    shape = (GRID_W, 2 * GRID_W)
    kc = lax.broadcasted_iota(jnp.int32, shape, 0)
    lane = lax.broadcasted_iota(jnp.int32, shape, 1)
    second = lane >= GRID_W
    qc = jnp.where(second, lane - GRID_W, lane)
    col_off = kc - qc + (WIN_COLS - 1)
    win_start = jnp.clip(qc - WIN_COLS // 2, 0, GRID_W - WIN_COLS)
    col_valid = (kc >= win_start) & (kc < win_start + WIN_COLS)
    n_off = 2 * WIN_COLS - 1
    n_row = 2 * WIN_ROWS - 1

    def entry(d, o):
        if 0 <= d < n_row:
            return rpb_ref[(head * n_row + d) * n_off + o]
        return jnp.float32(NEG_INF)

    for d in range(ROW_SLOTS):
        tile = jnp.full(shape, NEG_INF, F32)
        for o in range(n_off):
            val = jnp.where(second, entry(d - 1, o), entry(d, o))
            tile = jnp.where(col_off == o, val, tile)
        t2_ref[0, d] = jnp.where(col_valid, tile, NEG_INF)


def _bias_table(rpb):
    return pl.pallas_call(
        _bias_table_kernel,
        grid=(NA_HEADS,),
        in_specs=[pl.BlockSpec(memory_space=pltpu.SMEM)],
        out_specs=pl.BlockSpec((1, ROW_SLOTS, GRID_W, 2 * GRID_W), lambda h: (h, 0, 0, 0)),
        out_shape=jax.ShapeDtypeStruct((NA_HEADS, ROW_SLOTS, GRID_W, 2 * GRID_W), F32),
        compiler_params=pltpu.CompilerParams(dimension_semantics=("arbitrary",)),
        name="bias_table",
    )(rpb.astype(F32).reshape(-1))


def _token_range(refs, lo, hi, other, axis):
    pieces = []
    for idx, ref in enumerate(refs):
        a, b = max(lo, idx * Q_TOK), min(hi, (idx + 1) * Q_TOK)
        if a < b:
            tok = slice(a - idx * Q_TOK, b - idx * Q_TOK)
            pieces.append(ref[tok, other] if axis == 0 else ref[0, other, tok])
    return pieces[0] if len(pieces) == 1 else jnp.concatenate(pieces, axis=axis)


def _window_plan(variant, jp):
    if variant == 0:
        return [(i, i - 2 * jp + WIN_ROWS - 1, None) for i in range(WIN_ROWS)]
    if variant == 2:
        q_row = KEY_ROWS - Q_ROWS + 2 * jp
        return [(i, i - q_row + WIN_ROWS - 1, None) for i in range(KEY_ROWS - WIN_ROWS, KEY_ROWS)]
    q_row = Q_ROWS + 2 * jp
    plan = []
    for i in range(q_row - WIN_ROWS // 2, q_row + WIN_ROWS // 2 + 1):
        half = "low" if i == q_row - WIN_ROWS // 2 else "high" if i == q_row + WIN_ROWS // 2 else None
        plan.append((i, i - q_row + WIN_ROWS - 1, half))
    return plan


def _step_plan(position):
    plan = []
    for j in range(STEP_SUBS):
        if position == 0:
            plan.append((0, 0) if j == 0 else (1, (j - 1) * Q_ROWS))
        elif position == 1:
            plan.append((1, j * Q_ROWS))
        else:
            last = j == STEP_SUBS - 1
            plan.append((2 if last else 1, (j + 1 - last) * Q_ROWS))
    return tuple(plan)


def _attention_kernel(qt_ref, *refs, steps):
    k_refs, v_refs = refs[:KV_BLOCKS], refs[KV_BLOCKS:2 * KV_BLOCKS]
    t2_ref, out_ref = refs[2 * KV_BLOCKS:]
    step = pl.program_id(0)
    pair_lanes = 2 * NA_HEAD_DIM
    q_lanes = 2 * GRID_W
    n_pairs = NA_HEADS // 2

    def sub_group(sub, variant, key_off):
        plans = [_window_plan(variant, jp) for jp in range(Q_ROWS // 2)]
        key_lo = min(p[0][0] for p in plans)
        key_top = max(p[-1][0] for p in plans) + 1
        key_hi = key_top + (key_top - key_lo) % 2
        tok_lo, tok_top, tok_hi = ((key_off + r) * GRID_W for r in (key_lo, key_top, key_hi))
        lane = lax.broadcasted_iota(jnp.int32, (1, q_lanes), 1)
        half_mask = {"low": jnp.where(lane < GRID_W, 0.0, NEG_INF).astype(F32),
                     "high": jnp.where(lane >= GRID_W, 0.0, NEG_INF).astype(F32)}
        ones_rows = jnp.ones((16, tok_hi - tok_lo), BF16)
        zero_blk = jnp.zeros((GRID_W, q_lanes), BF16)

        def scores(pair):
            cols = slice(pair * pair_lanes, (pair + 1) * pair_lanes)
            qt_pair = qt_ref[sub, cols, :]
            zero = jnp.zeros((NA_HEAD_DIM, q_lanes), BF16)
            m_acc = [None] * (2 * len(plans))
            biased = [{} for _ in m_acc]
            for jp, plan in enumerate(plans):
                q_cols = slice(jp * q_lanes, (jp + 1) * q_lanes)
                qt_both = jnp.concatenate(
                    [jnp.concatenate([qt_pair[:NA_HEAD_DIM, q_cols], zero], axis=0),
                     jnp.concatenate([zero, qt_pair[NA_HEAD_DIM:, q_cols]], axis=0)], axis=1)
                row_lo, row_hi = plan[0][0], plan[-1][0] + 1
                k_rows = _token_range(k_refs, (key_off + row_lo) * GRID_W, (key_off + row_hi) * GRID_W, cols, 0)
                s = jnp.dot(k_rows, qt_both, preferred_element_type=F32)
                for i, slot_d, half in plan:
                    rows = slice((i - row_lo) * GRID_W, (i - row_lo + 1) * GRID_W)
                    for head_sub in range(2):
                        t = 2 * head_sub + jp
                        blk = s[rows, head_sub * q_lanes:(head_sub + 1) * q_lanes] + t2_ref[2 * pair + head_sub, slot_d]
                        if half is not None:
                            blk = blk + half_mask[half]
                        biased[t][i] = blk
                        blk_max = jnp.max(blk.reshape(GRID_W // 8, 8, q_lanes), axis=0)
                        m_acc[t] = blk_max if m_acc[t] is None else jnp.maximum(m_acc[t], blk_max)
            maxima = [jnp.max(m, axis=0, keepdims=True) for m in m_acc]
            return maxima, biased

        def softmax_pv(pair, state):
            maxima, biased = state
            for head_sub in range(2):
                head = 2 * pair + head_sub
                cols = []
                for jp in range(len(plans)):
                    t = 2 * head_sub + jp
                    blocks = [jnp.exp(biased[t][i] - maxima[t]).astype(BF16) if i in biased[t] else zero_blk
                              for i in range(key_lo, key_hi)]
                    cols.append(jnp.concatenate(blocks, axis=0))
                p_head = jnp.concatenate(cols, axis=1)
                rows = slice(head * NA_HEAD_DIM, (head + 1) * NA_HEAD_DIM)
                vt_head = _token_range(v_refs, tok_lo, tok_hi, rows, 1)
                vt_ext = jnp.concatenate([vt_head, ones_rows], axis=0)
                o = jnp.dot(vt_ext, p_head, preferred_element_type=F32)
                out_ref[sub, rows, :] = (o[:NA_HEAD_DIM] * (1.0 / o[NA_HEAD_DIM:NA_HEAD_DIM + 1])).astype(BF16)

        return scores, softmax_pv

    def body(step_variant):
        fns = [sub_group(sub, *plan) for sub, plan in enumerate(_step_plan(step_variant))]
        stages = [(sub, pair) for sub in range(len(fns)) for pair in range(n_pairs)]
        state = fns[0][0](0)
        for idx, (sub, pair) in enumerate(stages):
            nxt = None
            if idx + 1 < len(stages):
                nxt_sub, nxt_pair = stages[idx + 1]
                nxt = fns[nxt_sub][0](nxt_pair)
            fns[sub][1](pair, state)
            state = nxt

    is_top = step == 0
    is_bot = step == steps - 1
    pl.when(is_top)(lambda: body(0))
    pl.when(jnp.logical_not(is_top | is_bot))(lambda: body(1))
    pl.when(is_bot)(lambda: body(2))


def _attention(qt, k, vt, t2, batch, seq):
    t = batch * seq
    blocks = seq // Q_TOK
    steps = blocks // STEP_SUBS
    assert steps >= 2 and blocks >= KV_BLOCKS

    def first_kv(u):
        return jnp.clip(STEP_SUBS * u - 1, 0, blocks - KV_BLOCKS)

    def blk(u, b):
        return (b * steps + u, 0, 0)

    def kv(i):
        return lambda u, b: (b * blocks + first_kv(u) + i, 0)

    def kv_blk(i):
        return lambda u, b: (b * blocks + first_kv(u) + i, 0, 0)

    return pl.pallas_call(
        functools.partial(_attention_kernel, steps=steps),
        grid=(steps, batch),
        in_specs=[pl.BlockSpec((STEP_SUBS, NA_WIDTH, Q_TOK), blk)]
                 + [pl.BlockSpec((Q_TOK, NA_WIDTH), kv(i)) for i in range(KV_BLOCKS)]
                 + [pl.BlockSpec((1, NA_WIDTH, Q_TOK), kv_blk(i)) for i in range(KV_BLOCKS)]
                 + [pl.BlockSpec(t2.shape, lambda u, b: (0, 0, 0, 0))],
        out_specs=pl.BlockSpec((STEP_SUBS, NA_WIDTH, Q_TOK), blk),
        out_shape=jax.ShapeDtypeStruct((t // Q_TOK, NA_WIDTH, Q_TOK), BF16),
        compiler_params=pltpu.CompilerParams(
            dimension_semantics=("arbitrary", "arbitrary"),
            vmem_limit_bytes=VMEM_LIMIT),
        name="attention",
    )(qt, *([k] * KV_BLOCKS), *([vt] * KV_BLOCKS), t2)


DFT_CHUNK = 16
LANES = 128


def _to_lane_tiles(scr, x):
    for c in range(scr.shape[0]):
        scr[c] = x[:, c * LANES:(c + 1) * LANES]


def _from_lane_tiles(scr):
    return jnp.concatenate([scr[c] for c in range(scr.shape[0])], axis=1)


def _strided_rows(scr, start, size, stride):
    return jnp.concatenate([scr[c, pl.ds(start, size, stride=stride), :] for c in range(scr.shape[0])], axis=1)


def _store_strided_rows(scr, start, stride, x):
    for c in range(scr.shape[0]):
        scr[c, pl.ds(start, x.shape[0], stride=stride), :] = x[:, c * LANES:(c + 1) * LANES]


def _fourier_1_kernel(w_ref, u_ref, a_ref, *scratch, major, chunks):
    if not scratch:
        for c in range(chunks):
            rows = slice(c * DFT_CHUNK, (c + 1) * DFT_CHUNK)
            u2d = u_ref[0, :, rows, :].reshape(major * DFT_CHUNK, F_WIDTH)
            res = jnp.dot(w_ref[...], u2d, preferred_element_type=F32)
            a_ref[0, :, :, rows, :] = res.astype(BF16).reshape(major, 2, DFT_CHUNK, F_WIDTH)
        return
    in_w, out_w = scratch
    pairs = DFT_CHUNK // 2
    _to_lane_tiles(in_w, pltpu.bitcast(u_ref[0].reshape(major * DFT_CHUNK, F_WIDTH), jnp.uint32))
    for j in range(pairs):
        rhs = pltpu.bitcast(_strided_rows(in_w, j, major, pairs), BF16)
        res = jnp.dot(w_ref[...], rhs, preferred_element_type=F32)
        _store_strided_rows(out_w, j, pairs, pltpu.bitcast(res.astype(BF16), jnp.uint32))
    a_ref[0] = pltpu.bitcast(_from_lane_tiles(out_w), BF16).reshape(major, 2, DFT_CHUNK, F_WIDTH)


def _fourier_1(u4, w1, kron):
    batch, major, minor, _ = u4.shape
    chunks = minor // DFT_CHUNK if kron else 1
    tb = chunks * DFT_CHUNK
    scratch = [] if kron else [pltpu.VMEM((F_WIDTH // LANES, major * DFT_CHUNK // 2, LANES), jnp.uint32),
                               pltpu.VMEM((F_WIDTH // LANES, major * DFT_CHUNK, LANES), jnp.uint32)]
    return pl.pallas_call(
        functools.partial(_fourier_1_kernel, major=major, chunks=chunks),
        grid=(batch, minor // tb),
        in_specs=[pl.BlockSpec(w1.shape, lambda b, j: (0, 0)),
                  pl.BlockSpec((1, major, tb, F_WIDTH), lambda b, j: (b, 0, j, 0))],
        out_specs=pl.BlockSpec((1, major, 2, tb, F_WIDTH), lambda b, j: (b, 0, 0, j, 0)),
        out_shape=jax.ShapeDtypeStruct((batch, major, 2, minor, F_WIDTH), BF16),
        scratch_shapes=scratch,
        compiler_params=pltpu.CompilerParams(
            dimension_semantics=("arbitrary", "arbitrary"),
            vmem_limit_bytes=VMEM_LIMIT),
        name="fourier_1",
    )(w1, u4)


def _fourier_2_kernel(a_ref, m_ref, wc_ref, bf_ref, y_ref, x_scr, y_scr):
    for j in range(DFT_CHUNK):
        rhs = a_ref[0, j].reshape(2 * DFT_MINOR, F_WIDTH)
        x = jnp.dot(m_ref[j], rhs, preferred_element_type=F32)
        rows = slice(j * DFT_MINOR, (j + 1) * DFT_MINOR)
        x_scr[rows, :F_WIDTH] = x[:DFT_MINOR].astype(BF16)
        x_scr[rows, F_WIDTH:] = x[DFT_MINOR:].astype(BF16)
    y = jnp.dot(x_scr[...], wc_ref[...], preferred_element_type=F32) + bf_ref[...]
    pitch = DFT_CHUNK + 1
    for j in range(DFT_CHUNK):
        _store_strided_rows(y_scr, j, pitch, y[j * DFT_MINOR:(j + 1) * DFT_MINOR])
    packed = jnp.concatenate(
        [jnp.concatenate([y_scr[c, p * pitch:p * pitch + DFT_CHUNK, :] for p in range(DFT_MINOR)], axis=0)
         for c in range(y_scr.shape[0])], axis=1)
    y_ref[0] = packed.astype(BF16).reshape(DFT_MINOR, DFT_CHUNK, F_WIDTH)


def _fold_channel_dft_kernel(cs_ref, wf_ref, wc_ref):
    wc_ref[...] = jnp.dot(cs_ref[...], wf_ref[...], preferred_element_type=F32,
                          precision=lax.Precision.HIGHEST).astype(BF16)


def _fold_channel_dft(w_f):
    c = np.arange(F_GROUP_DIM)
    ang = 2.0 * np.pi * np.outer(c, c) / F_GROUP_DIM
    eye = np.eye(F_GROUPS)
    cs = np.concatenate([np.kron(eye, np.cos(ang)), np.kron(eye, np.sin(ang))], axis=0)
    return pl.pallas_call(
        _fold_channel_dft_kernel,
        out_shape=jax.ShapeDtypeStruct((2 * F_WIDTH, F_WIDTH), BF16),
        name="fold_channel_dft",
    )(jnp.asarray(cs, F32), w_f.astype(F32))


def _fourier_2(a5, m_tab, w_c, b_f):
    batch, major, _, minor, _ = a5.shape
    tr = DFT_CHUNK
    return pl.pallas_call(
        _fourier_2_kernel,
        grid=(batch, major // tr),
        in_specs=[pl.BlockSpec((1, tr, 2, minor, F_WIDTH), lambda b, r: (b, r, 0, 0, 0)),
                  pl.BlockSpec((tr, 2 * minor, 2 * minor), lambda b, r: (r, 0, 0)),
                  pl.BlockSpec(w_c.shape, lambda b, r: (0, 0)),
                  pl.BlockSpec((1, F_WIDTH), lambda b, r: (0, 0))],
        out_specs=pl.BlockSpec((1, minor, tr, F_WIDTH), lambda b, r: (b, 0, r, 0)),
        out_shape=jax.ShapeDtypeStruct((batch, minor, major, F_WIDTH), BF16),
        scratch_shapes=[pltpu.VMEM((tr * minor, 2 * F_WIDTH), BF16),
                        pltpu.VMEM((F_WIDTH // LANES, minor * (tr + 1), LANES), F32)],
        compiler_params=pltpu.CompilerParams(
            dimension_semantics=("arbitrary", "arbitrary"),
            vmem_limit_bytes=VMEM_LIMIT),
        name="fourier_2",
    )(a5, m_tab, w_c, b_f)


@functools.lru_cache(maxsize=None)
def _dft_constants(seq):
    major = seq // DFT_MINOR
    kron = major * DFT_CHUNK <= 256
    a = np.arange(major)
    ang1 = 2.0 * np.pi * np.outer(a, a) / major
    w1 = np.stack([np.cos(ang1), -np.sin(ang1)], axis=1).reshape(2 * major, major)
    w1 = np.kron(w1, np.eye(DFT_CHUNK if kron else 2))
    r = np.arange(major)[:, None, None]
    p = np.arange(DFT_MINOR)[None, :, None]
    b = np.arange(DFT_MINOR)[None, None, :]
    ang2 = 2.0 * np.pi * ((b * (r + major * p)) % seq) / seq
    scale = 1.0 / np.sqrt(seq * F_GROUP_DIM)
    e_re, e_im = np.cos(ang2) * scale, -np.sin(ang2) * scale
    m_tab = np.concatenate([np.concatenate([e_re, -e_im], axis=2),
                            np.concatenate([e_im, e_re], axis=2)], axis=1)
    return kron, np.asarray(w1, np.float32), np.asarray(m_tab, np.float32)


def _fourier_fused_kernel(w1_ref, u_ref, m_ref, wc_ref, bf_ref, y_ref, a_scr, *scratch, major, chunks):
    _fourier_1_kernel(w1_ref, u_ref, a_scr, major=major, chunks=chunks)
    _fourier_2_kernel(a_scr, m_ref, wc_ref, bf_ref, y_ref, *scratch)


def _fourier_fused(u4, w1, m_tab, w_c, b_f):
    batch, major, minor, _ = u4.shape
    const2 = lambda b: (0, 0)
    return pl.pallas_call(
        functools.partial(_fourier_fused_kernel, major=major, chunks=minor // DFT_CHUNK),
        grid=(batch,),
        in_specs=[pl.BlockSpec(w1.shape, const2),
                  pl.BlockSpec((1, major, minor, F_WIDTH), lambda b: (b, 0, 0, 0)),
                  pl.BlockSpec(m_tab.shape, lambda b: (0, 0, 0)),
                  pl.BlockSpec(w_c.shape, const2),
                  pl.BlockSpec((1, F_WIDTH), const2)],
        out_specs=pl.BlockSpec((1, minor, major, F_WIDTH), lambda b: (b, 0, 0, 0)),
        out_shape=jax.ShapeDtypeStruct((batch, minor, major, F_WIDTH), BF16),
        scratch_shapes=[pltpu.VMEM((1, major, 2, minor, F_WIDTH), BF16),
                        pltpu.VMEM((major * minor, 2 * F_WIDTH), BF16),
                        pltpu.VMEM((F_WIDTH // LANES, minor * (major + 1), LANES), F32)],
        compiler_params=pltpu.CompilerParams(
            dimension_semantics=("arbitrary",), vmem_limit_bytes=VMEM_LIMIT),
        name="fourier_fused",
    )(w1, u4, m_tab, w_c, b_f)


def _fourier(u, w_c, b_f, batch, seq):
    major = seq // DFT_MINOR
    kron, w1, m_tab = _dft_constants(seq)
    u4 = u.reshape(batch, major, DFT_MINOR, F_WIDTH)
    w1, m_tab = jnp.asarray(w1).astype(BF16), jnp.asarray(m_tab).astype(BF16)
    if kron and major == DFT_CHUNK:
        return _fourier_fused(u4, w1, m_tab, w_c, b_f).reshape(batch * seq, F_WIDTH)
    a = _fourier_1(u4, w1, kron)
    y = _fourier_2(a, m_tab, w_c, b_f)
    return y.reshape(batch * seq, F_WIDTH)


def _out_proj_kernel(ot_ref, zat_ref, yf_ref, zf_ref, x_ref, wa_ref, wf_ref,
                     gna_ref, gf_ref, gp_ref, o_ref):
    blocks = range(ot_ref.shape[0])
    y_t = jnp.concatenate([ot_ref[i] for i in blocks], axis=1).astype(F32)
    z_t = jnp.concatenate([zat_ref[i] for i in blocks], axis=1)
    inv = lax.rsqrt(jnp.mean(y_t * y_t, axis=0, keepdims=True) + RMS_EPS)
    mixed_at = _gate(y_t * inv * gna_ref[...], z_t)
    mixed_f = _gate(_rms(yf_ref[...].astype(F32), gf_ref[...]), zf_ref[...])
    out = lax.dot_general(mixed_at, wa_ref[...], (((0,), (0,)), ((), ())),
                          preferred_element_type=F32)
    out = out + jnp.dot(mixed_f, wf_ref[...], preferred_element_type=F32)
    o_ref[...] = x_ref[...] + _rms(out, gp_ref[...])


def _out_proj_specs(t, tm, w_a, w_fo):
    tok = lambda i: (i, 0)
    blk = lambda i: (i, 0, 0)
    const = lambda i: (0, 0)
    in_specs = [pl.BlockSpec((tm // Q_TOK, NA_WIDTH, Q_TOK), blk),
                pl.BlockSpec((tm // Q_TOK, NA_WIDTH, Q_TOK), blk),
                pl.BlockSpec((tm, F_WIDTH), tok),
                pl.BlockSpec((tm, F_WIDTH), tok),
                pl.BlockSpec((tm, D_MODEL), tok),
                pl.BlockSpec(w_a.shape, const),
                pl.BlockSpec(w_fo.shape, const),
                pl.BlockSpec((NA_WIDTH, 1), const),
                pl.BlockSpec((1, F_WIDTH), const),
                pl.BlockSpec((1, D_MODEL), const)]
    return in_specs, pl.BlockSpec((tm, D_MODEL), tok), jax.ShapeDtypeStruct((t, D_MODEL), F32)


def _out_proj(o_t, za_t, y_f, z_f, x2d, w_a, w_fo, g_na_col, g_f, g_post, tm=TOKEN_BLOCK):
    t = x2d.shape[0]
    in_specs, out_spec, out_shape = _out_proj_specs(t, tm, w_a, w_fo)
    return pl.pallas_call(
        _out_proj_kernel,
        grid=(t // tm,),
        in_specs=in_specs,
        out_specs=out_spec,
        out_shape=out_shape,
        compiler_params=pltpu.CompilerParams(
            dimension_semantics=("arbitrary",), vmem_limit_bytes=VMEM_LIMIT),
        name="out_proj",
    )(o_t, za_t, y_f, z_f, x2d, w_a, w_fo, g_na_col, g_f, g_post)


def _in_out_proj_kernel(x_ref, g_ref, wn_ref, wt_ref,
                        ot_ref, zat_in_ref, yf_ref, zf_in_ref, xo_ref, wa_ref, wf_ref, gna_ref, gf_ref, gp_ref,
                        k_ref, uf_ref, zf_ref, qt_ref, vt_ref, zat_ref, o_ref):
    h = _rms(x_ref[...], g_ref[...]).astype(BF16)
    blocks = range(ot_ref.shape[0])
    y_t = jnp.concatenate([ot_ref[i] for i in blocks], axis=1).astype(F32)
    z_t = jnp.concatenate([zat_in_ref[i] for i in blocks], axis=1)
    nat = jnp.dot(h, wn_ref[...], preferred_element_type=F32)
    inv = lax.rsqrt(jnp.mean(y_t * y_t, axis=0, keepdims=True) + RMS_EPS)
    mixed_at = _gate(y_t * inv * gna_ref[...], z_t)
    mixed_f = _gate(_rms(yf_ref[...].astype(F32), gf_ref[...]), zf_in_ref[...])
    k_ref[...] = nat[:, 0 * NA_WIDTH:1 * NA_WIDTH].astype(BF16)
    uf_ref[...] = nat[:, 1 * NA_WIDTH:2 * NA_WIDTH].astype(BF16)
    zf_ref[...] = nat[:, 2 * NA_WIDTH:3 * NA_WIDTH].astype(BF16)
    out = lax.dot_general(mixed_at, wa_ref[...], (((0,), (0,)), ((), ())), preferred_element_type=F32)
    out = out + jnp.dot(mixed_f, wf_ref[...], preferred_element_type=F32)
    tr = lax.dot_general(wt_ref[...], h, (((1,), (1,)), ((), ())), preferred_element_type=F32)
    o_ref[...] = xo_ref[...] + _rms(out, gp_ref[...])
    for i in range(qt_ref.shape[0]):
        tok = slice(i * Q_TOK, (i + 1) * Q_TOK)
        qt_ref[i] = tr[0 * NA_WIDTH:1 * NA_WIDTH, tok].astype(BF16)
        vt_ref[i] = tr[1 * NA_WIDTH:2 * NA_WIDTH, tok].astype(BF16)
        zat_ref[i] = tr[2 * NA_WIDTH:3 * NA_WIDTH, tok].astype(BF16)


def _in_out_proj(in_args, out_args, tm=FUSED_TOKEN_BLOCK):
    x_in, _, w_nat, w_tr = in_args
    x_out, w_a, w_fo = out_args[4], out_args[5], out_args[6]
    t = x_in.shape[0]
    assert x_out.shape[0] == t
    in_specs_a, out_specs_a, out_shape_a = _in_proj_specs(t, tm, w_nat, w_tr)
    in_specs_b, out_spec_b, out_shape_b = _out_proj_specs(t, tm, w_a, w_fo)
    res = pl.pallas_call(
        _in_out_proj_kernel,
        grid=(t // tm,),
        in_specs=in_specs_a + in_specs_b,
        out_specs=out_specs_a + [out_spec_b],
        out_shape=out_shape_a + [out_shape_b],
        compiler_params=pltpu.CompilerParams(
            dimension_semantics=("arbitrary",), vmem_limit_bytes=VMEM_LIMIT),
        name="in_out_proj",
    )(*in_args, *out_args)
    return res[:N_IN_PROJ_OUTPUTS], res[-1]


def _mixers(proj, bias, w_c, b_f, batch, seq):
    k, u_f, z_f, q_t, v_t, za_t = proj
    o_t = _attention(q_t, k, v_t, bias, batch, seq)
    y_f = _fourier(u_f, w_c, b_f, batch, seq)
    return o_t, za_t, y_f, z_f


def kernel(x_prompt, x_sample, w_in, rpb, w_fourier, b_fourier, g_pre, g_na, g_f, w_out, g_post):
    depth = w_in.shape[0]
    y_prompt, y_sample = x_prompt, x_sample
    scale = NA_HEAD_DIM ** -0.5
    for l in range(depth):
        w = w_in[l]
        w_nat = jnp.concatenate([w[:, i * NA_WIDTH:(i + 1) * NA_WIDTH] for i in (1, 4, 5)],
                                axis=1).astype(BF16)
        w_tr = _transposed_weights(w, (0, 2, 3), scale)
        bias = _bias_table(rpb[l])
        w_c = _fold_channel_dft(w_fourier[l])
        row = lambda v: v.reshape(1, -1).astype(F32)
        b_f, g_in = row(b_fourier[l]), row(g_pre[l])
        out_w = (w_out[l][:NA_WIDTH].astype(BF16), w_out[l][NA_WIDTH:].astype(BF16),
                 g_na[l].reshape(-1, 1).astype(F32), row(g_f[l]), row(g_post[l]))
        (bp, sp, _), (bs, ss, _) = y_prompt.shape, y_sample.shape
        xp, xs = y_prompt.reshape(bp * sp, D_MODEL), y_sample.reshape(bs * ss, D_MODEL)

        mixed_p = _mixers(_in_proj(xp, g_in, w_nat, w_tr), bias, w_c, b_f, bp, sp)
        if xp.shape[0] == xs.shape[0]:
            proj_s, out_p = _in_out_proj((xs, g_in, w_nat, w_tr), (*mixed_p, xp, *out_w))
        else:
            proj_s = _in_proj(xs, g_in, w_nat, w_tr)
            out_p = _out_proj(*mixed_p, xp, *out_w)
        mixed_s = _mixers(proj_s, bias, w_c, b_f, bs, ss)
        out_s = _out_proj(*mixed_s, xs, *out_w)
        y_prompt, y_sample = out_p.reshape(bp, sp, D_MODEL), out_s.reshape(bs, ss, D_MODEL)
    return (y_prompt, y_sample)
```

```python
import functools

import numpy as np
import jax
import jax.numpy as jnp
from jax import lax
from jax.experimental import pallas as pl
from jax.experimental.pallas import tpu as pltpu

D_MODEL = 1024
GRID_W = 64
WIN_ROWS = 8
WIN_COLS = 16
NA_HEADS = 8
NA_HEAD_DIM = 64
NA_WIDTH = NA_HEADS * NA_HEAD_DIM
F_GROUPS = 8
F_GROUP_DIM = 64
F_WIDTH = F_GROUPS * F_GROUP_DIM
RMS_EPS = 1e-6
NEG_INF = -1e30

Q_ROWS = 4
Q_TOK = Q_ROWS * GRID_W
STEP_SUBS = 4
KV_BLOCKS = STEP_SUBS + 2
KEY_ROWS = 3 * Q_ROWS
ROW_SLOTS = 2 * WIN_ROWS
DFT_MINOR = 128
TOKEN_BLOCK = 1024
FUSED_TOKEN_BLOCK = 512
VMEM_LIMIT = 56 * 1024 * 1024

BF16 = jnp.bfloat16
F32 = jnp.float32


def _rms(x, g):
    inv = lax.rsqrt(jnp.mean(x * x, axis=-1, keepdims=True) + RMS_EPS)
    return x * inv * g


def _silu(z):
    return z * (1.0 / (1.0 + jnp.exp(-z)))


def _gate(y_normed, z):
    return y_normed.astype(BF16) * _silu(z)


def _in_proj_kernel(x_ref, g_ref, wn_ref, wt_ref,
                    k_ref, uf_ref, zf_ref, qt_ref, vt_ref, zat_ref):
    h = _rms(x_ref[...], g_ref[...]).astype(BF16)
    nat = jnp.dot(h, wn_ref[...], preferred_element_type=F32)
    k_ref[...] = nat[:, 0 * NA_WIDTH:1 * NA_WIDTH].astype(BF16)
    uf_ref[...] = nat[:, 1 * NA_WIDTH:2 * NA_WIDTH].astype(BF16)
    zf_ref[...] = nat[:, 2 * NA_WIDTH:3 * NA_WIDTH].astype(BF16)
    tr = lax.dot_general(wt_ref[...], h, (((1,), (1,)), ((), ())),
                         preferred_element_type=F32)
    for i in range(qt_ref.shape[0]):
        tok = slice(i * Q_TOK, (i + 1) * Q_TOK)
        qt_ref[i] = tr[0 * NA_WIDTH:1 * NA_WIDTH, tok].astype(BF16)
        vt_ref[i] = tr[1 * NA_WIDTH:2 * NA_WIDTH, tok].astype(BF16)
        zat_ref[i] = tr[2 * NA_WIDTH:3 * NA_WIDTH, tok].astype(BF16)


def _transposed_weights_kernel(col_ref, w_ref, o_ref, *, scale):
    del col_ref
    factor = jnp.where(pl.program_id(0) == 0, scale, 1.0).astype(F32)
    o_ref[...] = (w_ref[...] * factor).T.astype(BF16)


def _transposed_weights(w, groups, scale):
    d, width = w.shape[0], NA_WIDTH
    col = jnp.asarray(groups, jnp.int32)
    return pl.pallas_call(
        functools.partial(_transposed_weights_kernel, scale=scale),
        grid_spec=pltpu.PrefetchScalarGridSpec(
            num_scalar_prefetch=1,
            grid=(len(groups),),
            in_specs=[pl.BlockSpec((d, width), lambda i, col: (0, col[i]))],
            out_specs=pl.BlockSpec((width, d), lambda i, col: (i, 0))),
        out_shape=jax.ShapeDtypeStruct((len(groups) * width, d), BF16),
        name="transposed_weights",
    )(col, w)


N_IN_PROJ_INPUTS = 4
N_IN_PROJ_OUTPUTS = 6


def _in_proj_specs(t, tm, w_nat, w_tr):
    tok = lambda i: (i, 0)
    blk = lambda i: (i, 0, 0)
    const = lambda i: (0, 0)
    nat_shape = jax.ShapeDtypeStruct((t, NA_WIDTH), BF16)
    tr_shape = jax.ShapeDtypeStruct((t // Q_TOK, NA_WIDTH, Q_TOK), BF16)
    in_specs = [pl.BlockSpec((tm, D_MODEL), tok),
                pl.BlockSpec((1, D_MODEL), const),
                pl.BlockSpec(w_nat.shape, const),
                pl.BlockSpec(w_tr.shape, const)]
    out_specs = ([pl.BlockSpec((tm, NA_WIDTH), tok)] * 3
                 + [pl.BlockSpec((tm // Q_TOK, NA_WIDTH, Q_TOK), blk)] * 3)
    return in_specs, out_specs, [nat_shape] * 3 + [tr_shape] * 3


def _in_proj(x2d, g_pre, w_nat, w_tr, tm=TOKEN_BLOCK):
    t = x2d.shape[0]
    in_specs, out_specs, out_shape = _in_proj_specs(t, tm, w_nat, w_tr)
    return pl.pallas_call(
        _in_proj_kernel,
        grid=(t // tm,),
        in_specs=in_specs,
        out_specs=out_specs,
        out_shape=out_shape,
        compiler_params=pltpu.CompilerParams(
            dimension_semantics=("arbitrary",), vmem_limit_bytes=VMEM_LIMIT),
        name="in_proj",
    )(x2d, g_pre, w_nat, w_tr)


def _bias_table_kernel(rpb_ref, t2_ref):
    head = pl.program_id(0)
    shape = (GRID_W, 2 * GRID_W)
    kc = lax.broadcasted_iota(jnp.int32, shape, 0)
    lane = lax.broadcasted_iota(jnp.int32, shape, 1)
    second = lane >= GRID_W
    qc = jnp.where(second, lane - GRID_W, lane)
    col_off = kc - qc + (WIN_COLS - 1)
    win_start = jnp.clip(qc - WIN_COLS // 2, 0, GRID_W - WIN_COLS)
    col_valid = (kc >= win_start) & (kc < win_start + WIN_COLS)
    n_off = 2 * WIN_COLS - 1
    n_row = 2 * WIN_ROWS - 1

    def entry(d, o):
        if 0 <= d < n_row:
            return rpb_ref[(head * n_row + d) * n_off + o]
        return jnp.float32(NEG_INF)

    for d in range(ROW_SLOTS):
        tile = jnp.full(shape, NEG_INF, F32)
        for o in range(n_off):
            val = jnp.where(second, entry(d - 1, o), entry(d, o))
            tile = jnp.where(col_off == o, val, tile)
        t2_ref[0, d] = jnp.where(col_valid, tile, NEG_INF)


def _bias_table(rpb):
    return pl.pallas_call(
        _bias_table_kernel,
        grid=(NA_HEADS,),
        in_specs=[pl.BlockSpec(memory_space=pltpu.SMEM)],
        out_specs=pl.BlockSpec((1, ROW_SLOTS, GRID_W, 2 * GRID_W), lambda h: (h, 0, 0, 0)),
        out_shape=jax.ShapeDtypeStruct((NA_HEADS, ROW_SLOTS, GRID_W, 2 * GRID_W), F32),
        compiler_params=pltpu.CompilerParams(dimension_semantics=("arbitrary",)),
        name="bias_table",
    )(rpb.astype(F32).reshape(-1))


def _token_range(refs, lo, hi, other, axis):
    pieces = []
    for idx, ref in enumerate(refs):
        a, b = max(lo, idx * Q_TOK), min(hi, (idx + 1) * Q_TOK)
        if a < b:
            tok = slice(a - idx * Q_TOK, b - idx * Q_TOK)
            pieces.append(ref[tok, other] if axis == 0 else ref[0, other, tok])
    return pieces[0] if len(pieces) == 1 else jnp.concatenate(pieces, axis=axis)


def _window_plan(variant, jp):
    if variant == 0:
        return [(i, i - 2 * jp + WIN_ROWS - 1, None) for i in range(WIN_ROWS)]
    if variant == 2:
        q_row = KEY_ROWS - Q_ROWS + 2 * jp
        return [(i, i - q_row + WIN_ROWS - 1, None) for i in range(KEY_ROWS - WIN_ROWS, KEY_ROWS)]
    q_row = Q_ROWS + 2 * jp
    plan = []
    for i in range(q_row - WIN_ROWS // 2, q_row + WIN_ROWS // 2 + 1):
        half = "low" if i == q_row - WIN_ROWS // 2 else "high" if i == q_row + WIN_ROWS // 2 else None
        plan.append((i, i - q_row + WIN_ROWS - 1, half))
    return plan


def _step_plan(position):
    plan = []
    for j in range(STEP_SUBS):
        if position == 0:
            plan.append((0, 0) if j == 0 else (1, (j - 1) * Q_ROWS))
        elif position == 1:
            plan.append((1, j * Q_ROWS))
        else:
            last = j == STEP_SUBS - 1
            plan.append((2 if last else 1, (j + 1 - last) * Q_ROWS))
    return tuple(plan)


def _attention_kernel(qt_ref, *refs, steps):
    k_refs, v_refs = refs[:KV_BLOCKS], refs[KV_BLOCKS:2 * KV_BLOCKS]
    t2_ref, out_ref = refs[2 * KV_BLOCKS:]
    step = pl.program_id(0)
    pair_lanes = 2 * NA_HEAD_DIM
    q_lanes = 2 * GRID_W
    n_pairs = NA_HEADS // 2

    def sub_group(sub, variant, key_off):
        plans = [_window_plan(variant, jp) for jp in range(Q_ROWS // 2)]
        key_lo = min(p[0][0] for p in plans)
        key_top = max(p[-1][0] for p in plans) + 1
        key_hi = key_top + (key_top - key_lo) % 2
        tok_lo, tok_top, tok_hi = ((key_off + r) * GRID_W for r in (key_lo, key_top, key_hi))
        lane = lax.broadcasted_iota(jnp.int32, (1, q_lanes), 1)
        half_mask = {"low": jnp.where(lane < GRID_W, 0.0, NEG_INF).astype(F32),
                     "high": jnp.where(lane >= GRID_W, 0.0, NEG_INF).astype(F32)}
        ones_rows = jnp.ones((16, tok_hi - tok_lo), BF16)
        zero_blk = jnp.zeros((GRID_W, q_lanes), BF16)

        def scores(pair):
            cols = slice(pair * pair_lanes, (pair + 1) * pair_lanes)
            qt_pair = qt_ref[sub, cols, :]
            zero = jnp.zeros((NA_HEAD_DIM, q_lanes), BF16)
            m_acc = [None] * (2 * len(plans))
            biased = [{} for _ in m_acc]
            for jp, plan in enumerate(plans):
                q_cols = slice(jp * q_lanes, (jp + 1) * q_lanes)
                qt_both = jnp.concatenate(
                    [jnp.concatenate([qt_pair[:NA_HEAD_DIM, q_cols], zero], axis=0),
                     jnp.concatenate([zero, qt_pair[NA_HEAD_DIM:, q_cols]], axis=0)], axis=1)
                row_lo, row_hi = plan[0][0], plan[-1][0] + 1
                k_rows = _token_range(k_refs, (key_off + row_lo) * GRID_W, (key_off + row_hi) * GRID_W, cols, 0)
                s = jnp.dot(k_rows, qt_both, preferred_element_type=F32)
                for i, slot_d, half in plan:
                    rows = slice((i - row_lo) * GRID_W, (i - row_lo + 1) * GRID_W)
                    for head_sub in range(2):
                        t = 2 * head_sub + jp
                        blk = s[rows, head_sub * q_lanes:(head_sub + 1) * q_lanes] + t2_ref[2 * pair + head_sub, slot_d]
                        if half is not None:
                            blk = blk + half_mask[half]
                        biased[t][i] = blk
                        blk_max = jnp.max(blk.reshape(GRID_W // 8, 8, q_lanes), axis=0)
                        m_acc[t] = blk_max if m_acc[t] is None else jnp.maximum(m_acc[t], blk_max)
            maxima = [jnp.max(m, axis=0, keepdims=True) for m in m_acc]
            return maxima, biased

        def softmax_pv(pair, state):
            maxima, biased = state
            for head_sub in range(2):
                head = 2 * pair + head_sub
                cols = []
                for jp in range(len(plans)):
                    t = 2 * head_sub + jp
                    blocks = [jnp.exp(biased[t][i] - maxima[t]).astype(BF16) if i in biased[t] else zero_blk
                              for i in range(key_lo, key_hi)]
                    cols.append(jnp.concatenate(blocks, axis=0))
                p_head = jnp.concatenate(cols, axis=1)
                rows = slice(head * NA_HEAD_DIM, (head + 1) * NA_HEAD_DIM)
                vt_head = _token_range(v_refs, tok_lo, tok_hi, rows, 1)
                vt_ext = jnp.concatenate([vt_head, ones_rows], axis=0)
                o = jnp.dot(vt_ext, p_head, preferred_element_type=F32)
                out_ref[sub, rows, :] = (o[:NA_HEAD_DIM] * (1.0 / o[NA_HEAD_DIM:NA_HEAD_DIM + 1])).astype(BF16)

        return scores, softmax_pv

    def body(step_variant):
        fns = [sub_group(sub, *plan) for sub, plan in enumerate(_step_plan(step_variant))]
        stages = [(sub, pair) for sub in range(len(fns)) for pair in range(n_pairs)]
        state = fns[0][0](0)
        for idx, (sub, pair) in enumerate(stages):
            nxt = None
            if idx + 1 < len(stages):
                nxt_sub, nxt_pair = stages[idx + 1]
                nxt = fns[nxt_sub][0](nxt_pair)
            fns[sub][1](pair, state)
            state = nxt

    is_top = step == 0
    is_bot = step == steps - 1
    pl.when(is_top)(lambda: body(0))
    pl.when(jnp.logical_not(is_top | is_bot))(lambda: body(1))
    pl.when(is_bot)(lambda: body(2))


def _attention(qt, k, vt, t2, batch, seq):
    t = batch * seq
    blocks = seq // Q_TOK
    steps = blocks // STEP_SUBS
    assert steps >= 2 and blocks >= KV_BLOCKS

    def first_kv(u):
        return jnp.clip(STEP_SUBS * u - 1, 0, blocks - KV_BLOCKS)

    def blk(u, b):
        return (b * steps + u, 0, 0)

    def kv(i):
        return lambda u, b: (b * blocks + first_kv(u) + i, 0)

    def kv_blk(i):
        return lambda u, b: (b * blocks + first_kv(u) + i, 0, 0)

    return pl.pallas_call(
        functools.partial(_attention_kernel, steps=steps),
        grid=(steps, batch),
        in_specs=[pl.BlockSpec((STEP_SUBS, NA_WIDTH, Q_TOK), blk)]
                 + [pl.BlockSpec((Q_TOK, NA_WIDTH), kv(i)) for i in range(KV_BLOCKS)]
                 + [pl.BlockSpec((1, NA_WIDTH, Q_TOK), kv_blk(i)) for i in range(KV_BLOCKS)]
                 + [pl.BlockSpec(t2.shape, lambda u, b: (0, 0, 0, 0))],
        out_specs=pl.BlockSpec((STEP_SUBS, NA_WIDTH, Q_TOK), blk),
        out_shape=jax.ShapeDtypeStruct((t // Q_TOK, NA_WIDTH, Q_TOK), BF16),
        compiler_params=pltpu.CompilerParams(
            dimension_semantics=("arbitrary", "arbitrary"),
            vmem_limit_bytes=VMEM_LIMIT),
        name="attention",
    )(qt, *([k] * KV_BLOCKS), *([vt] * KV_BLOCKS), t2)


DFT_CHUNK = 16
LANES = 128
STRIDED_CHUNKS = 2


def _to_lane_tiles(scr, x):
    for c in range(scr.shape[0]):
        scr[c] = x[:, c * LANES:(c + 1) * LANES]


def _from_lane_tiles(scr):
    return jnp.concatenate([scr[c] for c in range(scr.shape[0])], axis=1)


def _strided_rows(scr, start, size, stride):
    return jnp.concatenate([scr[c, pl.ds(start, size, stride=stride), :] for c in range(scr.shape[0])], axis=1)


def _store_strided_rows(scr, start, stride, x):
    for c in range(scr.shape[0]):
        scr[c, pl.ds(start, x.shape[0], stride=stride), :] = x[:, c * LANES:(c + 1) * LANES]


def _fourier_1_kernel(w_ref, u_ref, a_ref, *scratch, major, chunks):
    if not scratch:
        for c in range(chunks):
            rows = slice(c * DFT_CHUNK, (c + 1) * DFT_CHUNK)
            u2d = u_ref[0, :, rows, :].reshape(major * DFT_CHUNK, F_WIDTH)
            res = jnp.dot(w_ref[...], u2d, preferred_element_type=F32)
            a_ref[0, :, :, rows, :] = res.astype(BF16).reshape(major, 2, DFT_CHUNK, F_WIDTH)
        return
    in_w, out_w = scratch
    pairs = DFT_CHUNK // 2
    for c in range(chunks):
        rows = slice(c * DFT_CHUNK, (c + 1) * DFT_CHUNK)
        _to_lane_tiles(in_w, pltpu.bitcast(u_ref[0, :, rows, :].reshape(major * DFT_CHUNK, F_WIDTH), jnp.uint32))
        for j in range(pairs):
            rhs = pltpu.bitcast(_strided_rows(in_w, j, major, pairs), BF16)
            res = jnp.dot(w_ref[...], rhs, preferred_element_type=F32)
            _store_strided_rows(out_w, j, pairs, pltpu.bitcast(res.astype(BF16), jnp.uint32))
        a_ref[0, :, :, rows, :] = pltpu.bitcast(_from_lane_tiles(out_w), BF16).reshape(
            major, 2, DFT_CHUNK, F_WIDTH)


def _fourier_1(u4, w1, kron):
    batch, major, minor, _ = u4.shape
    chunks = minor // DFT_CHUNK if kron else STRIDED_CHUNKS
    tb = chunks * DFT_CHUNK
    scratch = [] if kron else [pltpu.VMEM((F_WIDTH // LANES, major * DFT_CHUNK // 2, LANES), jnp.uint32),
                               pltpu.VMEM((F_WIDTH // LANES, major * DFT_CHUNK, LANES), jnp.uint32)]
    return pl.pallas_call(
        functools.partial(_fourier_1_kernel, major=major, chunks=chunks),
        grid=(batch, minor // tb),
        in_specs=[pl.BlockSpec(w1.shape, lambda b, j: (0, 0)),
                  pl.BlockSpec((1, major, tb, F_WIDTH), lambda b, j: (b, 0, j, 0))],
        out_specs=pl.BlockSpec((1, major, 2, tb, F_WIDTH), lambda b, j: (b, 0, 0, j, 0)),
        out_shape=jax.ShapeDtypeStruct((batch, major, 2, minor, F_WIDTH), BF16),
        scratch_shapes=scratch,
        compiler_params=pltpu.CompilerParams(
            dimension_semantics=("arbitrary", "arbitrary"),
            vmem_limit_bytes=VMEM_LIMIT),
        name="fourier_1",
    )(w1, u4)


def _fourier_2_kernel(a_ref, m_ref, wc_ref, bf_ref, y_ref, x_scr, y_scr):
    for j in range(DFT_CHUNK):
        rhs = a_ref[0, j].reshape(2 * DFT_MINOR, F_WIDTH)
        x = jnp.dot(m_ref[j], rhs, preferred_element_type=F32)
        rows = slice(j * DFT_MINOR, (j + 1) * DFT_MINOR)
        x_scr[rows, :F_WIDTH] = x[:DFT_MINOR].astype(BF16)
        x_scr[rows, F_WIDTH:] = x[DFT_MINOR:].astype(BF16)
    y = jnp.dot(x_scr[...], wc_ref[...], preferred_element_type=F32) + bf_ref[...]
    pitch = DFT_CHUNK + 1
    for j in range(DFT_CHUNK):
        _store_strided_rows(y_scr, j, pitch, y[j * DFT_MINOR:(j + 1) * DFT_MINOR])
    packed = jnp.concatenate(
        [jnp.concatenate([y_scr[c, p * pitch:p * pitch + DFT_CHUNK, :] for p in range(DFT_MINOR)], axis=0)
         for c in range(y_scr.shape[0])], axis=1)
    y_ref[0] = packed.astype(BF16).reshape(DFT_MINOR, DFT_CHUNK, F_WIDTH)


def _fold_channel_dft_kernel(cs_ref, wf_ref, wc_ref):
    wc_ref[...] = jnp.dot(cs_ref[...], wf_ref[...], preferred_element_type=F32,
                          precision=lax.Precision.HIGHEST).astype(BF16)


def _fold_channel_dft(w_f):
    c = np.arange(F_GROUP_DIM)
    ang = 2.0 * np.pi * np.outer(c, c) / F_GROUP_DIM
    eye = np.eye(F_GROUPS)
    cs = np.concatenate([np.kron(eye, np.cos(ang)), np.kron(eye, np.sin(ang))], axis=0)
    return pl.pallas_call(
        _fold_channel_dft_kernel,
        out_shape=jax.ShapeDtypeStruct((2 * F_WIDTH, F_WIDTH), BF16),
        name="fold_channel_dft",
    )(jnp.asarray(cs, F32), w_f.astype(F32))


def _fourier_2(a5, m_tab, w_c, b_f):
    batch, major, _, minor, _ = a5.shape
    tr = DFT_CHUNK
    return pl.pallas_call(
        _fourier_2_kernel,
        grid=(batch, major // tr),
        in_specs=[pl.BlockSpec((1, tr, 2, minor, F_WIDTH), lambda b, r: (b, r, 0, 0, 0)),
                  pl.BlockSpec((tr, 2 * minor, 2 * minor), lambda b, r: (r, 0, 0)),
                  pl.BlockSpec(w_c.shape, lambda b, r: (0, 0)),
                  pl.BlockSpec((1, F_WIDTH), lambda b, r: (0, 0))],
        out_specs=pl.BlockSpec((1, minor, tr, F_WIDTH), lambda b, r: (b, 0, r, 0)),
        out_shape=jax.ShapeDtypeStruct((batch, minor, major, F_WIDTH), BF16),
        scratch_shapes=[pltpu.VMEM((tr * minor, 2 * F_WIDTH), BF16),
                        pltpu.VMEM((F_WIDTH // LANES, minor * (tr + 1), LANES), F32)],
        compiler_params=pltpu.CompilerParams(
            dimension_semantics=("arbitrary", "arbitrary"),
            vmem_limit_bytes=VMEM_LIMIT),
        name="fourier_2",
    )(a5, m_tab, w_c, b_f)


@functools.lru_cache(maxsize=None)
def _dft_constants(seq):
    major = seq // DFT_MINOR
    kron = major * DFT_CHUNK <= 256
    a = np.arange(major)
    ang1 = 2.0 * np.pi * np.outer(a, a) / major
    w1 = np.stack([np.cos(ang1), -np.sin(ang1)], axis=1).reshape(2 * major, major)
    w1 = np.kron(w1, np.eye(DFT_CHUNK if kron else 2))
    r = np.arange(major)[:, None, None]
    p = np.arange(DFT_MINOR)[None, :, None]
    b = np.arange(DFT_MINOR)[None, None, :]
    ang2 = 2.0 * np.pi * ((b * (r + major * p)) % seq) / seq
    scale = 1.0 / np.sqrt(seq * F_GROUP_DIM)
    e_re, e_im = np.cos(ang2) * scale, -np.sin(ang2) * scale
    m_tab = np.concatenate([np.concatenate([e_re, -e_im], axis=2),
                            np.concatenate([e_im, e_re], axis=2)], axis=1)
    return kron, np.asarray(w1, np.float32), np.asarray(m_tab, np.float32)


def _fourier_fused_kernel(w1_ref, u_ref, m_ref, wc_ref, bf_ref, y_ref, a_scr, *scratch, major, chunks):
    _fourier_1_kernel(w1_ref, u_ref, a_scr, major=major, chunks=chunks)
    _fourier_2_kernel(a_scr, m_ref, wc_ref, bf_ref, y_ref, *scratch)


def _fourier_fused(u4, w1, m_tab, w_c, b_f):
    batch, major, minor, _ = u4.shape
    const2 = lambda b: (0, 0)
    return pl.pallas_call(
        functools.partial(_fourier_fused_kernel, major=major, chunks=minor // DFT_CHUNK),
        grid=(batch,),
        in_specs=[pl.BlockSpec(w1.shape, const2),
                  pl.BlockSpec((1, major, minor, F_WIDTH), lambda b: (b, 0, 0, 0)),
                  pl.BlockSpec(m_tab.shape, lambda b: (0, 0, 0)),
                  pl.BlockSpec(w_c.shape, const2),
                  pl.BlockSpec((1, F_WIDTH), const2)],
        out_specs=pl.BlockSpec((1, minor, major, F_WIDTH), lambda b: (b, 0, 0, 0)),
        out_shape=jax.ShapeDtypeStruct((batch, minor, major, F_WIDTH), BF16),
        scratch_shapes=[pltpu.VMEM((1, major, 2, minor, F_WIDTH), BF16),
                        pltpu.VMEM((major * minor, 2 * F_WIDTH), BF16),
                        pltpu.VMEM((F_WIDTH // LANES, minor * (major + 1), LANES), F32)],
        compiler_params=pltpu.CompilerParams(
            dimension_semantics=("arbitrary",), vmem_limit_bytes=VMEM_LIMIT),
        name="fourier_fused",
    )(w1, u4, m_tab, w_c, b_f)


def _fourier(u, w_c, b_f, batch, seq):
    major = seq // DFT_MINOR
    kron, w1, m_tab = _dft_constants(seq)
    u4 = u.reshape(batch, major, DFT_MINOR, F_WIDTH)
    w1, m_tab = jnp.asarray(w1).astype(BF16), jnp.asarray(m_tab).astype(BF16)
    if kron and major == DFT_CHUNK:
        return _fourier_fused(u4, w1, m_tab, w_c, b_f).reshape(batch * seq, F_WIDTH)
    a = _fourier_1(u4, w1, kron)
    y = _fourier_2(a, m_tab, w_c, b_f)
    return y.reshape(batch * seq, F_WIDTH)


def _out_proj_kernel(ot_ref, zat_ref, yf_ref, zf_ref, x_ref, wa_ref, wf_ref,
                     gna_ref, gf_ref, gp_ref, o_ref):
    blocks = range(ot_ref.shape[0])
    y_t = jnp.concatenate([ot_ref[i] for i in blocks], axis=1).astype(F32)
    z_t = jnp.concatenate([zat_ref[i] for i in blocks], axis=1)
    inv = lax.rsqrt(jnp.mean(y_t * y_t, axis=0, keepdims=True) + RMS_EPS)
    mixed_at = _gate(y_t * inv * gna_ref[...], z_t)
    mixed_f = _gate(_rms(yf_ref[...].astype(F32), gf_ref[...]), zf_ref[...])
    out = lax.dot_general(mixed_at, wa_ref[...], (((0,), (0,)), ((), ())),
                          preferred_element_type=F32)
    out = out + jnp.dot(mixed_f, wf_ref[...], preferred_element_type=F32)
    o_ref[...] = x_ref[...] + _rms(out, gp_ref[...])


def _out_proj_specs(t, tm, w_a, w_fo):
    tok = lambda i: (i, 0)
    blk = lambda i: (i, 0, 0)
    const = lambda i: (0, 0)
    in_specs = [pl.BlockSpec((tm // Q_TOK, NA_WIDTH, Q_TOK), blk),
                pl.BlockSpec((tm // Q_TOK, NA_WIDTH, Q_TOK), blk),
                pl.BlockSpec((tm, F_WIDTH), tok),
                pl.BlockSpec((tm, F_WIDTH), tok),
                pl.BlockSpec((tm, D_MODEL), tok),
                pl.BlockSpec(w_a.shape, const),
                pl.BlockSpec(w_fo.shape, const),
                pl.BlockSpec((NA_WIDTH, 1), const),
                pl.BlockSpec((1, F_WIDTH), const),
                pl.BlockSpec((1, D_MODEL), const)]
    return in_specs, pl.BlockSpec((tm, D_MODEL), tok), jax.ShapeDtypeStruct((t, D_MODEL), F32)


def _out_proj(o_t, za_t, y_f, z_f, x2d, w_a, w_fo, g_na_col, g_f, g_post, tm=TOKEN_BLOCK):
    t = x2d.shape[0]
    in_specs, out_spec, out_shape = _out_proj_specs(t, tm, w_a, w_fo)
    return pl.pallas_call(
        _out_proj_kernel,
        grid=(t // tm,),
        in_specs=in_specs,
        out_specs=out_spec,
        out_shape=out_shape,
        compiler_params=pltpu.CompilerParams(
            dimension_semantics=("arbitrary",), vmem_limit_bytes=VMEM_LIMIT),
        name="out_proj",
    )(o_t, za_t, y_f, z_f, x2d, w_a, w_fo, g_na_col, g_f, g_post)


def _in_out_proj_kernel(x_ref, g_ref, wn_ref, wt_ref,
                        ot_ref, zat_in_ref, yf_ref, zf_in_ref, xo_ref, wa_ref, wf_ref, gna_ref, gf_ref, gp_ref,
                        k_ref, uf_ref, zf_ref, qt_ref, vt_ref, zat_ref, o_ref):
    h = _rms(x_ref[...], g_ref[...]).astype(BF16)
    blocks = range(ot_ref.shape[0])
    y_t = jnp.concatenate([ot_ref[i] for i in blocks], axis=1).astype(F32)
    z_t = jnp.concatenate([zat_in_ref[i] for i in blocks], axis=1)
    nat = jnp.dot(h, wn_ref[...], preferred_element_type=F32)
    inv = lax.rsqrt(jnp.mean(y_t * y_t, axis=0, keepdims=True) + RMS_EPS)
    mixed_at = _gate(y_t * inv * gna_ref[...], z_t)
    mixed_f = _gate(_rms(yf_ref[...].astype(F32), gf_ref[...]), zf_in_ref[...])
    k_ref[...] = nat[:, 0 * NA_WIDTH:1 * NA_WIDTH].astype(BF16)
    uf_ref[...] = nat[:, 1 * NA_WIDTH:2 * NA_WIDTH].astype(BF16)
    zf_ref[...] = nat[:, 2 * NA_WIDTH:3 * NA_WIDTH].astype(BF16)
    out = lax.dot_general(mixed_at, wa_ref[...], (((0,), (0,)), ((), ())), preferred_element_type=F32)
    out = out + jnp.dot(mixed_f, wf_ref[...], preferred_element_type=F32)
    tr = lax.dot_general(wt_ref[...], h, (((1,), (1,)), ((), ())), preferred_element_type=F32)
    o_ref[...] = xo_ref[...] + _rms(out, gp_ref[...])
    for i in range(qt_ref.shape[0]):
        tok = slice(i * Q_TOK, (i + 1) * Q_TOK)
        qt_ref[i] = tr[0 * NA_WIDTH:1 * NA_WIDTH, tok].astype(BF16)
        vt_ref[i] = tr[1 * NA_WIDTH:2 * NA_WIDTH, tok].astype(BF16)
        zat_ref[i] = tr[2 * NA_WIDTH:3 * NA_WIDTH, tok].astype(BF16)


def _in_out_proj(in_args, out_args, tm=FUSED_TOKEN_BLOCK):
    x_in, _, w_nat, w_tr = in_args
    x_out, w_a, w_fo = out_args[4], out_args[5], out_args[6]
    t = x_in.shape[0]
    assert x_out.shape[0] == t
    in_specs_a, out_specs_a, out_shape_a = _in_proj_specs(t, tm, w_nat, w_tr)
    in_specs_b, out_spec_b, out_shape_b = _out_proj_specs(t, tm, w_a, w_fo)
    res = pl.pallas_call(
        _in_out_proj_kernel,
        grid=(t // tm,),
        in_specs=in_specs_a + in_specs_b,
        out_specs=out_specs_a + [out_spec_b],
        out_shape=out_shape_a + [out_shape_b],
        compiler_params=pltpu.CompilerParams(
            dimension_semantics=("arbitrary",), vmem_limit_bytes=VMEM_LIMIT),
        name="in_out_proj",
    )(*in_args, *out_args)
    return res[:N_IN_PROJ_OUTPUTS], res[-1]


def _mixers(proj, bias, w_c, b_f, batch, seq):
    k, u_f, z_f, q_t, v_t, za_t = proj
    o_t = _attention(q_t, k, v_t, bias, batch, seq)
    y_f = _fourier(u_f, w_c, b_f, batch, seq)
    return o_t, za_t, y_f, z_f


def kernel(x_prompt, x_sample, w_in, rpb, w_fourier, b_fourier, g_pre, g_na, g_f, w_out, g_post):
    depth = w_in.shape[0]
    y_prompt, y_sample = x_prompt, x_sample
    scale = NA_HEAD_DIM ** -0.5
    for l in range(depth):
        w = w_in[l]
        w_nat = jnp.concatenate([w[:, i * NA_WIDTH:(i + 1) * NA_WIDTH] for i in (1, 4, 5)],
                                axis=1).astype(BF16)
        w_tr = _transposed_weights(w, (0, 2, 3), scale)
        bias = _bias_table(rpb[l])
        w_c = _fold_channel_dft(w_fourier[l])
        row = lambda v: v.reshape(1, -1).astype(F32)
        b_f, g_in = row(b_fourier[l]), row(g_pre[l])
        out_w = (w_out[l][:NA_WIDTH].astype(BF16), w_out[l][NA_WIDTH:].astype(BF16),
                 g_na[l].reshape(-1, 1).astype(F32), row(g_f[l]), row(g_post[l]))
        (bp, sp, _), (bs, ss, _) = y_prompt.shape, y_sample.shape
        xp, xs = y_prompt.reshape(bp * sp, D_MODEL), y_sample.reshape(bs * ss, D_MODEL)

        mixed_p = _mixers(_in_proj(xp, g_in, w_nat, w_tr), bias, w_c, b_f, bp, sp)
        if xp.shape[0] == xs.shape[0]:
            proj_s, out_p = _in_out_proj((xs, g_in, w_nat, w_tr), (*mixed_p, xp, *out_w))
        else:
            proj_s = _in_proj(xs, g_in, w_nat, w_tr)
            out_p = _out_proj(*mixed_p, xp, *out_w)
        mixed_s = _mixers(proj_s, bias, w_c, b_f, bs, ss)
        out_s = _out_proj(*mixed_s, xs, *out_w)
        y_prompt, y_sample = out_p.reshape(bp, sp, D_MODEL), out_s.reshape(bs, ss, D_MODEL)
    return (y_prompt, y_sample)
```

```python
import functools

import numpy as np
import jax
import jax.numpy as jnp
from jax import lax
from jax.experimental import pallas as pl
from jax.experimental.pallas import tpu as pltpu

D_MODEL = 1024
GRID_W = 64
WIN_ROWS = 8
WIN_COLS = 16
NA_HEADS = 8
NA_HEAD_DIM = 64
NA_WIDTH = NA_HEADS * NA_HEAD_DIM
F_GROUPS = 8
F_GROUP_DIM = 64
F_WIDTH = F_GROUPS * F_GROUP_DIM
RMS_EPS = 1e-6
NEG_INF = -1e30

Q_ROWS = 4
Q_TOK = Q_ROWS * GRID_W
STEP_SUBS = 4
KV_BLOCKS = STEP_SUBS + 2
KEY_ROWS = 3 * Q_ROWS
ROW_SLOTS = 2 * WIN_ROWS
DFT_MINOR = 128
TOKEN_BLOCK = 1024
FUSED_TOKEN_BLOCK = 1024
VMEM_LIMIT = 62 * 1024 * 1024

BF16 = jnp.bfloat16
F32 = jnp.float32


def _rms(x, g):
    inv = lax.rsqrt(jnp.mean(x * x, axis=-1, keepdims=True) + RMS_EPS)
    return x * inv * g


def _silu(z):
    return z * (1.0 / (1.0 + jnp.exp(-z)))


def _gate(y_normed, z):
    return y_normed.astype(BF16) * _silu(z)


def _in_proj_kernel(x_ref, g_ref, wn_ref, wt_ref,
                    k_ref, uf_ref, zf_ref, qt_ref, vt_ref, zat_ref):
    h = _rms(x_ref[...], g_ref[...]).astype(BF16)
    nat = jnp.dot(h, wn_ref[...], preferred_element_type=F32)
    k_ref[...] = nat[:, 0 * NA_WIDTH:1 * NA_WIDTH].astype(BF16)
    uf_ref[...] = nat[:, 1 * NA_WIDTH:2 * NA_WIDTH].astype(BF16)
    zf_ref[...] = nat[:, 2 * NA_WIDTH:3 * NA_WIDTH].astype(BF16)
    tr = lax.dot_general(wt_ref[...], h, (((1,), (1,)), ((), ())),
                         preferred_element_type=F32)
    for i in range(qt_ref.shape[0]):
        tok = slice(i * Q_TOK, (i + 1) * Q_TOK)
        qt_ref[i] = tr[0 * NA_WIDTH:1 * NA_WIDTH, tok].astype(BF16)
        vt_ref[i] = tr[1 * NA_WIDTH:2 * NA_WIDTH, tok].astype(BF16)
        zat_ref[i] = tr[2 * NA_WIDTH:3 * NA_WIDTH, tok].astype(BF16)


def _transposed_weights_kernel(col_ref, w_ref, o_ref, *, scale):
    del col_ref
    factor = jnp.where(pl.program_id(0) == 0, scale, 1.0).astype(F32)
    o_ref[...] = (w_ref[...] * factor).T.astype(BF16)


def _transposed_weights(w, groups, scale):
    d, width = w.shape[0], NA_WIDTH
    col = jnp.asarray(groups, jnp.int32)
    return pl.pallas_call(
        functools.partial(_transposed_weights_kernel, scale=scale),
        grid_spec=pltpu.PrefetchScalarGridSpec(
            num_scalar_prefetch=1,
            grid=(len(groups),),
            in_specs=[pl.BlockSpec((d, width), lambda i, col: (0, col[i]))],
            out_specs=pl.BlockSpec((width, d), lambda i, col: (i, 0))),
        out_shape=jax.ShapeDtypeStruct((len(groups) * width, d), BF16),
        name="transposed_weights",
    )(col, w)


N_IN_PROJ_INPUTS = 4
N_IN_PROJ_OUTPUTS = 6


def _in_proj_specs(t, tm, w_nat, w_tr):
    tok = lambda i: (i, 0)
    blk = lambda i: (i, 0, 0)
    const = lambda i: (0, 0)
    nat_shape = jax.ShapeDtypeStruct((t, NA_WIDTH), BF16)
    tr_shape = jax.ShapeDtypeStruct((t // Q_TOK, NA_WIDTH, Q_TOK), BF16)
    in_specs = [pl.BlockSpec((tm, D_MODEL), tok),
                pl.BlockSpec((1, D_MODEL), const),
                pl.BlockSpec(w_nat.shape, const),
                pl.BlockSpec(w_tr.shape, const)]
    out_specs = ([pl.BlockSpec((tm, NA_WIDTH), tok)] * 3
                 + [pl.BlockSpec((tm // Q_TOK, NA_WIDTH, Q_TOK), blk)] * 3)
    return in_specs, out_specs, [nat_shape] * 3 + [tr_shape] * 3


def _in_proj(x2d, g_pre, w_nat, w_tr, tm=TOKEN_BLOCK):
    t = x2d.shape[0]
    in_specs, out_specs, out_shape = _in_proj_specs(t, tm, w_nat, w_tr)
    return pl.pallas_call(
        _in_proj_kernel,
        grid=(t // tm,),
        in_specs=in_specs,
        out_specs=out_specs,
        out_shape=out_shape,
        compiler_params=pltpu.CompilerParams(
            dimension_semantics=("arbitrary",), vmem_limit_bytes=VMEM_LIMIT),
        name="in_proj",
    )(x2d, g_pre, w_nat, w_tr)


def _bias_table_kernel(rpb_ref, t2_ref):
    head = pl.program_id(0)
    shape = (GRID_W, 2 * GRID_W)
    kc = lax.broadcasted_iota(jnp.int32, shape, 0)
    lane = lax.broadcasted_iota(jnp.int32, shape, 1)
    second = lane >= GRID_W
    qc = jnp.where(second, lane - GRID_W, lane)
    col_off = kc - qc + (WIN_COLS - 1)
    win_start = jnp.clip(qc - WIN_COLS // 2, 0, GRID_W - WIN_COLS)
    col_valid = (kc >= win_start) & (kc < win_start + WIN_COLS)
    n_off = 2 * WIN_COLS - 1
    n_row = 2 * WIN_ROWS - 1

    def entry(d, o):
        if 0 <= d < n_row:
            return rpb_ref[(head * n_row + d) * n_off + o]
        return jnp.float32(NEG_INF)

    for d in range(ROW_SLOTS):
        tile = jnp.full(shape, NEG_INF, F32)
        for o in range(n_off):
            val = jnp.where(second, entry(d - 1, o), entry(d, o))
            tile = jnp.where(col_off == o, val, tile)
        t2_ref[0, d] = jnp.where(col_valid, tile, NEG_INF)


def _bias_table(rpb):
    return pl.pallas_call(
        _bias_table_kernel,
        grid=(NA_HEADS,),
        in_specs=[pl.BlockSpec(memory_space=pltpu.SMEM)],
        out_specs=pl.BlockSpec((1, ROW_SLOTS, GRID_W, 2 * GRID_W), lambda h: (h, 0, 0, 0)),
        out_shape=jax.ShapeDtypeStruct((NA_HEADS, ROW_SLOTS, GRID_W, 2 * GRID_W), F32),
        compiler_params=pltpu.CompilerParams(dimension_semantics=("arbitrary",)),
        name="bias_table",
    )(rpb.astype(F32).reshape(-1))


def _token_range(refs, lo, hi, other, axis):
    pieces = []
    for idx, ref in enumerate(refs):
        a, b = max(lo, idx * Q_TOK), min(hi, (idx + 1) * Q_TOK)
        if a < b:
            tok = slice(a - idx * Q_TOK, b - idx * Q_TOK)
            pieces.append(ref[tok, other] if axis == 0 else ref[0, other, tok])
    return pieces[0] if len(pieces) == 1 else jnp.concatenate(pieces, axis=axis)


def _window_plan(variant, jp):
    if variant == 0:
        return [(i, i - 2 * jp + WIN_ROWS - 1, None) for i in range(WIN_ROWS)]
    if variant == 2:
        q_row = KEY_ROWS - Q_ROWS + 2 * jp
        return [(i, i - q_row + WIN_ROWS - 1, None) for i in range(KEY_ROWS - WIN_ROWS, KEY_ROWS)]
    q_row = Q_ROWS + 2 * jp
    plan = []
    for i in range(q_row - WIN_ROWS // 2, q_row + WIN_ROWS // 2 + 1):
        half = "low" if i == q_row - WIN_ROWS // 2 else "high" if i == q_row + WIN_ROWS // 2 else None
        plan.append((i, i - q_row + WIN_ROWS - 1, half))
    return plan


def _step_plan(position):
    plan = []
    for j in range(STEP_SUBS):
        if position == 0:
            plan.append((0, 0) if j == 0 else (1, (j - 1) * Q_ROWS))
        elif position == 1:
            plan.append((1, j * Q_ROWS))
        else:
            last = j == STEP_SUBS - 1
            plan.append((2 if last else 1, (j + 1 - last) * Q_ROWS))
    return tuple(plan)


def _attention_kernel(qt_ref, *refs, steps):
    k_refs, v_refs = refs[:KV_BLOCKS], refs[KV_BLOCKS:2 * KV_BLOCKS]
    t2_ref, out_ref = refs[2 * KV_BLOCKS:]
    step = pl.program_id(0)
    pair_lanes = 2 * NA_HEAD_DIM
    q_lanes = 2 * GRID_W
    n_pairs = NA_HEADS // 2

    def sub_group(sub, variant, key_off):
        plans = [_window_plan(variant, jp) for jp in range(Q_ROWS // 2)]
        key_lo = min(p[0][0] for p in plans)
        key_top = max(p[-1][0] for p in plans) + 1
        key_hi = key_top + (key_top - key_lo) % 2
        tok_lo, tok_top, tok_hi = ((key_off + r) * GRID_W for r in (key_lo, key_top, key_hi))
        lane = lax.broadcasted_iota(jnp.int32, (1, q_lanes), 1)
        half_mask = {"low": jnp.where(lane < GRID_W, 0.0, NEG_INF).astype(F32),
                     "high": jnp.where(lane >= GRID_W, 0.0, NEG_INF).astype(F32)}
        ones_rows = jnp.ones((16, tok_hi - tok_lo), BF16)
        zero_blk = jnp.zeros((GRID_W, q_lanes), BF16)

        def scores(pair):
            cols = slice(pair * pair_lanes, (pair + 1) * pair_lanes)
            qt_pair = qt_ref[sub, cols, :]
            zero = jnp.zeros((NA_HEAD_DIM, q_lanes), BF16)
            m_acc = [None] * (2 * len(plans))
            biased = [{} for _ in m_acc]
            for jp, plan in enumerate(plans):
                q_cols = slice(jp * q_lanes, (jp + 1) * q_lanes)
                qt_both = jnp.concatenate(
                    [jnp.concatenate([qt_pair[:NA_HEAD_DIM, q_cols], zero], axis=0),
                     jnp.concatenate([zero, qt_pair[NA_HEAD_DIM:, q_cols]], axis=0)], axis=1)
                row_lo, row_hi = plan[0][0], plan[-1][0] + 1
                k_rows = _token_range(k_refs, (key_off + row_lo) * GRID_W, (key_off + row_hi) * GRID_W, cols, 0)
                s = jnp.dot(k_rows, qt_both, preferred_element_type=F32)
                for i, slot_d, half in plan:
                    rows = slice((i - row_lo) * GRID_W, (i - row_lo + 1) * GRID_W)
                    for head_sub in range(2):
                        t = 2 * head_sub + jp
                        blk = s[rows, head_sub * q_lanes:(head_sub + 1) * q_lanes] + t2_ref[2 * pair + head_sub, slot_d]
                        if half is not None:
                            blk = blk + half_mask[half]
                        biased[t][i] = blk
                        blk_max = jnp.max(blk.reshape(GRID_W // 8, 8, q_lanes), axis=0)
                        m_acc[t] = blk_max if m_acc[t] is None else jnp.maximum(m_acc[t], blk_max)
            maxima = [jnp.max(m, axis=0, keepdims=True) for m in m_acc]
            return maxima, biased

        def softmax_pv(pair, state):
            maxima, biased = state
            for head_sub in range(2):
                head = 2 * pair + head_sub
                cols = []
                for jp in range(len(plans)):
                    t = 2 * head_sub + jp
                    blocks = [jnp.exp(biased[t][i] - maxima[t]).astype(BF16) if i in biased[t] else zero_blk
                              for i in range(key_lo, key_hi)]
                    cols.append(jnp.concatenate(blocks, axis=0))
                p_head = jnp.concatenate(cols, axis=1)
                rows = slice(head * NA_HEAD_DIM, (head + 1) * NA_HEAD_DIM)
                vt_head = _token_range(v_refs, tok_lo, tok_hi, rows, 1)
                vt_ext = jnp.concatenate([vt_head, ones_rows], axis=0)
                o = jnp.dot(vt_ext, p_head, preferred_element_type=F32)
                out_ref[sub, rows, :] = (o[:NA_HEAD_DIM] * (1.0 / o[NA_HEAD_DIM:NA_HEAD_DIM + 1])).astype(BF16)

        return scores, softmax_pv

    def body(step_variant):
        fns = [sub_group(sub, *plan) for sub, plan in enumerate(_step_plan(step_variant))]
        stages = [(sub, pair) for sub in range(len(fns)) for pair in range(n_pairs)]
        state = fns[0][0](0)
        for idx, (sub, pair) in enumerate(stages):
            nxt = None
            if idx + 1 < len(stages):
                nxt_sub, nxt_pair = stages[idx + 1]
                nxt = fns[nxt_sub][0](nxt_pair)
            fns[sub][1](pair, state)
            state = nxt

    is_top = step == 0
    is_bot = step == steps - 1
    pl.when(is_top)(lambda: body(0))
    pl.when(jnp.logical_not(is_top | is_bot))(lambda: body(1))
    pl.when(is_bot)(lambda: body(2))


def _attention(qt, k, vt, t2, batch, seq):
    t = batch * seq
    blocks = seq // Q_TOK
    steps = blocks // STEP_SUBS
    assert steps >= 2 and blocks >= KV_BLOCKS

    def first_kv(u):
        return jnp.clip(STEP_SUBS * u - 1, 0, blocks - KV_BLOCKS)

    def blk(u, b):
        return (b * steps + u, 0, 0)

    def kv(i):
        return lambda u, b: (b * blocks + first_kv(u) + i, 0)

    def kv_blk(i):
        return lambda u, b: (b * blocks + first_kv(u) + i, 0, 0)

    return pl.pallas_call(
        functools.partial(_attention_kernel, steps=steps),
        grid=(steps, batch),
        in_specs=[pl.BlockSpec((STEP_SUBS, NA_WIDTH, Q_TOK), blk)]
                 + [pl.BlockSpec((Q_TOK, NA_WIDTH), kv(i)) for i in range(KV_BLOCKS)]
                 + [pl.BlockSpec((1, NA_WIDTH, Q_TOK), kv_blk(i)) for i in range(KV_BLOCKS)]
                 + [pl.BlockSpec(t2.shape, lambda u, b: (0, 0, 0, 0))],
        out_specs=pl.BlockSpec((STEP_SUBS, NA_WIDTH, Q_TOK), blk),
        out_shape=jax.ShapeDtypeStruct((t // Q_TOK, NA_WIDTH, Q_TOK), BF16),
        compiler_params=pltpu.CompilerParams(
            dimension_semantics=("arbitrary", "arbitrary"),
            vmem_limit_bytes=VMEM_LIMIT),
        name="attention",
    )(qt, *([k] * KV_BLOCKS), *([vt] * KV_BLOCKS), t2)


DFT_CHUNK = 16
LANES = 128
STRIDED_CHUNKS = 2


def _to_lane_tiles(scr, x):
    for c in range(scr.shape[0]):
        scr[c] = x[:, c * LANES:(c + 1) * LANES]


def _from_lane_tiles(scr):
    return jnp.concatenate([scr[c] for c in range(scr.shape[0])], axis=1)


def _strided_rows(scr, start, size, stride):
    return jnp.concatenate([scr[c, pl.ds(start, size, stride=stride), :] for c in range(scr.shape[0])], axis=1)


def _store_strided_rows(scr, start, stride, x):
    for c in range(scr.shape[0]):
        scr[c, pl.ds(start, x.shape[0], stride=stride), :] = x[:, c * LANES:(c + 1) * LANES]


def _fourier_1_kernel(w_ref, u_ref, a_ref, *scratch, major, chunks):
    if not scratch:
        for c in range(chunks):
            rows = slice(c * DFT_CHUNK, (c + 1) * DFT_CHUNK)
            u2d = u_ref[0, :, rows, :].reshape(major * DFT_CHUNK, F_WIDTH)
            res = jnp.dot(w_ref[...], u2d, preferred_element_type=F32)
            a_ref[0, :, :, rows, :] = res.astype(BF16).reshape(major, 2, DFT_CHUNK, F_WIDTH)
        return
    in_w, out_w = scratch
    pairs = DFT_CHUNK // 2
    for c in range(chunks):
        rows = slice(c * DFT_CHUNK, (c + 1) * DFT_CHUNK)
        _to_lane_tiles(in_w, pltpu.bitcast(u_ref[0, :, rows, :].reshape(major * DFT_CHUNK, F_WIDTH), jnp.uint32))
        for j in range(pairs):
            rhs = pltpu.bitcast(_strided_rows(in_w, j, major, pairs), BF16)
            res = jnp.dot(w_ref[...], rhs, preferred_element_type=F32)
            _store_strided_rows(out_w, j, pairs, pltpu.bitcast(res.astype(BF16), jnp.uint32))
        a_ref[0, :, :, rows, :] = pltpu.bitcast(_from_lane_tiles(out_w), BF16).reshape(
            major, 2, DFT_CHUNK, F_WIDTH)


def _fourier_1(u4, w1, kron):
    batch, major, minor, _ = u4.shape
    chunks = minor // DFT_CHUNK if kron else STRIDED_CHUNKS
    tb = chunks * DFT_CHUNK
    scratch = [] if kron else [pltpu.VMEM((F_WIDTH // LANES, major * DFT_CHUNK // 2, LANES), jnp.uint32),
                               pltpu.VMEM((F_WIDTH // LANES, major * DFT_CHUNK, LANES), jnp.uint32)]
    return pl.pallas_call(
        functools.partial(_fourier_1_kernel, major=major, chunks=chunks),
        grid=(batch, minor // tb),
        in_specs=[pl.BlockSpec(w1.shape, lambda b, j: (0, 0)),
                  pl.BlockSpec((1, major, tb, F_WIDTH), lambda b, j: (b, 0, j, 0))],
        out_specs=pl.BlockSpec((1, major, 2, tb, F_WIDTH), lambda b, j: (b, 0, 0, j, 0)),
        out_shape=jax.ShapeDtypeStruct((batch, major, 2, minor, F_WIDTH), BF16),
        scratch_shapes=scratch,
        compiler_params=pltpu.CompilerParams(
            dimension_semantics=("arbitrary", "arbitrary"),
            vmem_limit_bytes=VMEM_LIMIT),
        name="fourier_1",
    )(w1, u4)


def _fourier_2_kernel(a_ref, m_ref, wc_ref, bf_ref, y_ref, x_scr, y_scr):
    for j in range(DFT_CHUNK):
        rhs = a_ref[0, j].reshape(2 * DFT_MINOR, F_WIDTH)
        x = jnp.dot(m_ref[j], rhs, preferred_element_type=F32)
        rows = slice(j * DFT_MINOR, (j + 1) * DFT_MINOR)
        x_scr[rows, :F_WIDTH] = x[:DFT_MINOR].astype(BF16)
        x_scr[rows, F_WIDTH:] = x[DFT_MINOR:].astype(BF16)
    y = jnp.dot(x_scr[...], wc_ref[...], preferred_element_type=F32) + bf_ref[...]
    pitch = DFT_CHUNK + 1
    for j in range(DFT_CHUNK):
        _store_strided_rows(y_scr, j, pitch, y[j * DFT_MINOR:(j + 1) * DFT_MINOR])
    packed = jnp.concatenate(
        [jnp.concatenate([y_scr[c, p * pitch:p * pitch + DFT_CHUNK, :] for p in range(DFT_MINOR)], axis=0)
         for c in range(y_scr.shape[0])], axis=1)
    y_ref[0] = packed.astype(BF16).reshape(DFT_MINOR, DFT_CHUNK, F_WIDTH)


def _fold_channel_dft_kernel(cs_ref, wf_ref, wc_ref):
    wc_ref[...] = jnp.dot(cs_ref[...], wf_ref[...], preferred_element_type=F32,
                          precision=lax.Precision.HIGHEST).astype(BF16)


def _fold_channel_dft(w_f):
    c = np.arange(F_GROUP_DIM)
    ang = 2.0 * np.pi * np.outer(c, c) / F_GROUP_DIM
    eye = np.eye(F_GROUPS)
    cs = np.concatenate([np.kron(eye, np.cos(ang)), np.kron(eye, np.sin(ang))], axis=0)
    return pl.pallas_call(
        _fold_channel_dft_kernel,
        out_shape=jax.ShapeDtypeStruct((2 * F_WIDTH, F_WIDTH), BF16),
        name="fold_channel_dft",
    )(jnp.asarray(cs, F32), w_f.astype(F32))


def _fourier_2(a5, m_tab, w_c, b_f):
    batch, major, _, minor, _ = a5.shape
    tr = DFT_CHUNK
    return pl.pallas_call(
        _fourier_2_kernel,
        grid=(batch, major // tr),
        in_specs=[pl.BlockSpec((1, tr, 2, minor, F_WIDTH), lambda b, r: (b, r, 0, 0, 0)),
                  pl.BlockSpec((tr, 2 * minor, 2 * minor), lambda b, r: (r, 0, 0)),
                  pl.BlockSpec(w_c.shape, lambda b, r: (0, 0)),
                  pl.BlockSpec((1, F_WIDTH), lambda b, r: (0, 0))],
        out_specs=pl.BlockSpec((1, minor, tr, F_WIDTH), lambda b, r: (b, 0, r, 0)),
        out_shape=jax.ShapeDtypeStruct((batch, minor, major, F_WIDTH), BF16),
        scratch_shapes=[pltpu.VMEM((tr * minor, 2 * F_WIDTH), BF16),
                        pltpu.VMEM((F_WIDTH // LANES, minor * (tr + 1), LANES), F32)],
        compiler_params=pltpu.CompilerParams(
            dimension_semantics=("arbitrary", "arbitrary"),
            vmem_limit_bytes=VMEM_LIMIT),
        name="fourier_2",
    )(a5, m_tab, w_c, b_f)


@functools.lru_cache(maxsize=None)
def _dft_constants(seq):
    major = seq // DFT_MINOR
    kron = major * DFT_CHUNK <= 256
    a = np.arange(major)
    ang1 = 2.0 * np.pi * np.outer(a, a) / major
    w1 = np.stack([np.cos(ang1), -np.sin(ang1)], axis=1).reshape(2 * major, major)
    w1 = np.kron(w1, np.eye(DFT_CHUNK if kron else 2))
    r = np.arange(major)[:, None, None]
    p = np.arange(DFT_MINOR)[None, :, None]
    b = np.arange(DFT_MINOR)[None, None, :]
    ang2 = 2.0 * np.pi * ((b * (r + major * p)) % seq) / seq
    scale = 1.0 / np.sqrt(seq * F_GROUP_DIM)
    e_re, e_im = np.cos(ang2) * scale, -np.sin(ang2) * scale
    m_tab = np.concatenate([np.concatenate([e_re, -e_im], axis=2),
                            np.concatenate([e_im, e_re], axis=2)], axis=1)
    return kron, np.asarray(w1, np.float32), np.asarray(m_tab, np.float32)


def _fourier_fused_kernel(w1_ref, u_ref, m_ref, wc_ref, bf_ref, y_ref, a_scr, *scratch, major, chunks):
    _fourier_1_kernel(w1_ref, u_ref, a_scr, major=major, chunks=chunks)
    _fourier_2_kernel(a_scr, m_ref, wc_ref, bf_ref, y_ref, *scratch)


def _fourier_fused(u4, w1, m_tab, w_c, b_f):
    batch, major, minor, _ = u4.shape
    const2 = lambda b: (0, 0)
    return pl.pallas_call(
        functools.partial(_fourier_fused_kernel, major=major, chunks=minor // DFT_CHUNK),
        grid=(batch,),
        in_specs=[pl.BlockSpec(w1.shape, const2),
                  pl.BlockSpec((1, major, minor, F_WIDTH), lambda b: (b, 0, 0, 0)),
                  pl.BlockSpec(m_tab.shape, lambda b: (0, 0, 0)),
                  pl.BlockSpec(w_c.shape, const2),
                  pl.BlockSpec((1, F_WIDTH), const2)],
        out_specs=pl.BlockSpec((1, minor, major, F_WIDTH), lambda b: (b, 0, 0, 0)),
        out_shape=jax.ShapeDtypeStruct((batch, minor, major, F_WIDTH), BF16),
        scratch_shapes=[pltpu.VMEM((1, major, 2, minor, F_WIDTH), BF16),
                        pltpu.VMEM((major * minor, 2 * F_WIDTH), BF16),
                        pltpu.VMEM((F_WIDTH // LANES, minor * (major + 1), LANES), F32)],
        compiler_params=pltpu.CompilerParams(
            dimension_semantics=("arbitrary",), vmem_limit_bytes=VMEM_LIMIT),
        name="fourier_fused",
    )(w1, u4, m_tab, w_c, b_f)


def _fourier(u, w_c, b_f, batch, seq):
    major = seq // DFT_MINOR
    kron, w1, m_tab = _dft_constants(seq)
    u4 = u.reshape(batch, major, DFT_MINOR, F_WIDTH)
    w1, m_tab = jnp.asarray(w1).astype(BF16), jnp.asarray(m_tab).astype(BF16)
    if kron and major == DFT_CHUNK:
        return _fourier_fused(u4, w1, m_tab, w_c, b_f).reshape(batch * seq, F_WIDTH)
    a = _fourier_1(u4, w1, kron)
    y = _fourier_2(a, m_tab, w_c, b_f)
    return y.reshape(batch * seq, F_WIDTH)


def _out_proj_kernel(ot_ref, zat_ref, yf_ref, zf_ref, x_ref, wa_ref, wf_ref,
                     gna_ref, gf_ref, gp_ref, o_ref):
    blocks = range(ot_ref.shape[0])
    y_t = jnp.concatenate([ot_ref[i] for i in blocks], axis=1).astype(F32)
    z_t = jnp.concatenate([zat_ref[i] for i in blocks], axis=1)
    inv = lax.rsqrt(jnp.mean(y_t * y_t, axis=0, keepdims=True) + RMS_EPS)
    mixed_at = _gate(y_t * inv * gna_ref[...], z_t)
    mixed_f = _gate(_rms(yf_ref[...].astype(F32), gf_ref[...]), zf_ref[...])
    out = lax.dot_general(mixed_at, wa_ref[...], (((0,), (0,)), ((), ())),
                          preferred_element_type=F32)
    out = out + jnp.dot(mixed_f, wf_ref[...], preferred_element_type=F32)
    o_ref[...] = x_ref[...] + _rms(out, gp_ref[...])


def _out_proj_specs(t, tm, w_a, w_fo):
    tok = lambda i: (i, 0)
    blk = lambda i: (i, 0, 0)
    const = lambda i: (0, 0)
    in_specs = [pl.BlockSpec((tm // Q_TOK, NA_WIDTH, Q_TOK), blk),
                pl.BlockSpec((tm // Q_TOK, NA_WIDTH, Q_TOK), blk),
                pl.BlockSpec((tm, F_WIDTH), tok),
                pl.BlockSpec((tm, F_WIDTH), tok),
                pl.BlockSpec((tm, D_MODEL), tok),
                pl.BlockSpec(w_a.shape, const),
                pl.BlockSpec(w_fo.shape, const),
                pl.BlockSpec((NA_WIDTH, 1), const),
                pl.BlockSpec((1, F_WIDTH), const),
                pl.BlockSpec((1, D_MODEL), const)]
    return in_specs, pl.BlockSpec((tm, D_MODEL), tok), jax.ShapeDtypeStruct((t, D_MODEL), F32)


def _out_proj(o_t, za_t, y_f, z_f, x2d, w_a, w_fo, g_na_col, g_f, g_post, tm=TOKEN_BLOCK):
    t = x2d.shape[0]
    in_specs, out_spec, out_shape = _out_proj_specs(t, tm, w_a, w_fo)
    return pl.pallas_call(
        _out_proj_kernel,
        grid=(t // tm,),
        in_specs=in_specs,
        out_specs=out_spec,
        out_shape=out_shape,
        compiler_params=pltpu.CompilerParams(
            dimension_semantics=("arbitrary",), vmem_limit_bytes=VMEM_LIMIT),
        name="out_proj",
    )(o_t, za_t, y_f, z_f, x2d, w_a, w_fo, g_na_col, g_f, g_post)


def _in_out_proj_kernel(x_ref, g_ref, wn_ref, wt_ref,
                        ot_ref, zat_in_ref, yf_ref, zf_in_ref, xo_ref, wa_ref, wf_ref, gna_ref, gf_ref, gp_ref,
                        k_ref, uf_ref, zf_ref, qt_ref, vt_ref, zat_ref, o_ref):
    h = _rms(x_ref[...], g_ref[...]).astype(BF16)
    blocks = range(ot_ref.shape[0])
    y_t = jnp.concatenate([ot_ref[i] for i in blocks], axis=1).astype(F32)
    z_t = jnp.concatenate([zat_in_ref[i] for i in blocks], axis=1)
    nat = jnp.dot(h, wn_ref[...], preferred_element_type=F32)
    inv = lax.rsqrt(jnp.mean(y_t * y_t, axis=0, keepdims=True) + RMS_EPS)
    mixed_at = _gate(y_t * inv * gna_ref[...], z_t)
    mixed_f = _gate(_rms(yf_ref[...].astype(F32), gf_ref[...]), zf_in_ref[...])
    k_ref[...] = nat[:, 0 * NA_WIDTH:1 * NA_WIDTH].astype(BF16)
    uf_ref[...] = nat[:, 1 * NA_WIDTH:2 * NA_WIDTH].astype(BF16)
    zf_ref[...] = nat[:, 2 * NA_WIDTH:3 * NA_WIDTH].astype(BF16)
    out = lax.dot_general(mixed_at, wa_ref[...], (((0,), (0,)), ((), ())), preferred_element_type=F32)
    out = out + jnp.dot(mixed_f, wf_ref[...], preferred_element_type=F32)
    tr = lax.dot_general(wt_ref[...], h, (((1,), (1,)), ((), ())), preferred_element_type=F32)
    o_ref[...] = xo_ref[...] + _rms(out, gp_ref[...])
    for i in range(qt_ref.shape[0]):
        tok = slice(i * Q_TOK, (i + 1) * Q_TOK)
        qt_ref[i] = tr[0 * NA_WIDTH:1 * NA_WIDTH, tok].astype(BF16)
        vt_ref[i] = tr[1 * NA_WIDTH:2 * NA_WIDTH, tok].astype(BF16)
        zat_ref[i] = tr[2 * NA_WIDTH:3 * NA_WIDTH, tok].astype(BF16)


def _in_out_proj(in_args, out_args, tm=FUSED_TOKEN_BLOCK):
    x_in, _, w_nat, w_tr = in_args
    x_out, w_a, w_fo = out_args[4], out_args[5], out_args[6]
    t = x_in.shape[0]
    assert x_out.shape[0] == t
    in_specs_a, out_specs_a, out_shape_a = _in_proj_specs(t, tm, w_nat, w_tr)
    in_specs_b, out_spec_b, out_shape_b = _out_proj_specs(t, tm, w_a, w_fo)
    res = pl.pallas_call(
        _in_out_proj_kernel,
        grid=(t // tm,),
        in_specs=in_specs_a + in_specs_b,
        out_specs=out_specs_a + [out_spec_b],
        out_shape=out_shape_a + [out_shape_b],
        compiler_params=pltpu.CompilerParams(
            dimension_semantics=("arbitrary",), vmem_limit_bytes=VMEM_LIMIT),
        name="in_out_proj",
    )(*in_args, *out_args)
    return res[:N_IN_PROJ_OUTPUTS], res[-1]


def _mixers(proj, bias, w_c, b_f, batch, seq):
    k, u_f, z_f, q_t, v_t, za_t = proj
    o_t = _attention(q_t, k, v_t, bias, batch, seq)
    y_f = _fourier(u_f, w_c, b_f, batch, seq)
    return o_t, za_t, y_f, z_f


def kernel(x_prompt, x_sample, w_in, rpb, w_fourier, b_fourier, g_pre, g_na, g_f, w_out, g_post):
    depth = w_in.shape[0]
    y_prompt, y_sample = x_prompt, x_sample
    scale = NA_HEAD_DIM ** -0.5
    for l in range(depth):
        w = w_in[l]
        w_nat = jnp.concatenate([w[:, i * NA_WIDTH:(i + 1) * NA_WIDTH] for i in (1, 4, 5)],
                                axis=1).astype(BF16)
        w_tr = _transposed_weights(w, (0, 2, 3), scale)
        bias = _bias_table(rpb[l])
        w_c = _fold_channel_dft(w_fourier[l])
        row = lambda v: v.reshape(1, -1).astype(F32)
        b_f, g_in = row(b_fourier[l]), row(g_pre[l])
        out_w = (w_out[l][:NA_WIDTH].astype(BF16), w_out[l][NA_WIDTH:].astype(BF16),
                 g_na[l].reshape(-1, 1).astype(F32), row(g_f[l]), row(g_post[l]))
        (bp, sp, _), (bs, ss, _) = y_prompt.shape, y_sample.shape
        xp, xs = y_prompt.reshape(bp * sp, D_MODEL), y_sample.reshape(bs * ss, D_MODEL)

        mixed_p = _mixers(_in_proj(xp, g_in, w_nat, w_tr), bias, w_c, b_f, bp, sp)
        if xp.shape[0] == xs.shape[0]:
            proj_s, out_p = _in_out_proj((xs, g_in, w_nat, w_tr), (*mixed_p, xp, *out_w))
        else:
            proj_s = _in_proj(xs, g_in, w_nat, w_tr)
            out_p = _out_proj(*mixed_p, xp, *out_w)
        mixed_s = _mixers(proj_s, bias, w_c, b_f, bs, ss)
        out_s = _out_proj(*mixed_s, xs, *out_w)
        y_prompt, y_sample = out_p.reshape(bp, sp, D_MODEL), out_s.reshape(bs, ss, D_MODEL)
    return (y_prompt, y_sample)
```

```python
import functools

import numpy as np
import jax
import jax.numpy as jnp
from jax import lax
from jax.experimental import pallas as pl
from jax.experimental.pallas import tpu as pltpu

D_MODEL = 1024
GRID_W = 64
WIN_ROWS = 8
WIN_COLS = 16
NA_HEADS = 8
NA_HEAD_DIM = 64
NA_WIDTH = NA_HEADS * NA_HEAD_DIM
F_GROUPS = 8
F_GROUP_DIM = 64
F_WIDTH = F_GROUPS * F_GROUP_DIM
RMS_EPS = 1e-6
NEG_INF = -1e30

Q_ROWS = 4
Q_TOK = Q_ROWS * GRID_W
STEP_SUBS = 4
KV_BLOCKS = STEP_SUBS + 2
KEY_ROWS = 3 * Q_ROWS
ROW_SLOTS = 2 * WIN_ROWS
DFT_MINOR = 128
TOKEN_BLOCK = 1024
FUSED_TOKEN_BLOCK = 1024
VMEM_LIMIT = 62 * 1024 * 1024

BF16 = jnp.bfloat16
F32 = jnp.float32


def _rms(x, g):
    inv = lax.rsqrt(jnp.mean(x * x, axis=-1, keepdims=True) + RMS_EPS)
    return x * inv * g


def _silu(z):
    return z * (1.0 / (1.0 + jnp.exp(-z)))


def _gate(y_normed, z):
    return y_normed.astype(BF16) * _silu(z)


def _in_proj_kernel(x_ref, g_ref, wn_ref, wt_ref,
                    k_ref, uf_ref, zf_ref, qt_ref, vt_ref, zat_ref):
    h = _rms(x_ref[...], g_ref[...]).astype(BF16)
    nat = jnp.dot(h, wn_ref[...], preferred_element_type=F32)
    k_ref[...] = nat[:, 0 * NA_WIDTH:1 * NA_WIDTH].astype(BF16)
    uf_ref[...] = nat[:, 1 * NA_WIDTH:2 * NA_WIDTH].astype(BF16)
    zf_ref[...] = nat[:, 2 * NA_WIDTH:3 * NA_WIDTH].astype(BF16)
    tr = lax.dot_general(wt_ref[...], h, (((1,), (1,)), ((), ())),
                         preferred_element_type=F32)
    for i in range(qt_ref.shape[0]):
        tok = slice(i * Q_TOK, (i + 1) * Q_TOK)
        qt_ref[i] = tr[0 * NA_WIDTH:1 * NA_WIDTH, tok].astype(BF16)
        vt_ref[i] = tr[1 * NA_WIDTH:2 * NA_WIDTH, tok].astype(BF16)
        zat_ref[i] = tr[2 * NA_WIDTH:3 * NA_WIDTH, tok].astype(BF16)


def _transposed_weights_kernel(col_ref, w_ref, o_ref, *, scale):
    del col_ref
    factor = jnp.where(pl.program_id(0) == 0, scale, 1.0).astype(F32)
    o_ref[...] = (w_ref[...] * factor).T.astype(BF16)


def _transposed_weights(w, groups, scale):
    d, width = w.shape[0], NA_WIDTH
    col = jnp.asarray(groups, jnp.int32)
    return pl.pallas_call(
        functools.partial(_transposed_weights_kernel, scale=scale),
        grid_spec=pltpu.PrefetchScalarGridSpec(
            num_scalar_prefetch=1,
            grid=(len(groups),),
            in_specs=[pl.BlockSpec((d, width), lambda i, col: (0, col[i]))],
            out_specs=pl.BlockSpec((width, d), lambda i, col: (i, 0))),
        out_shape=jax.ShapeDtypeStruct((len(groups) * width, d), BF16),
        name="transposed_weights",
    )(col, w)


N_IN_PROJ_INPUTS = 4
N_IN_PROJ_OUTPUTS = 6


def _in_proj_specs(t, tm, w_nat, w_tr):
    tok = lambda i: (i, 0)
    blk = lambda i: (i, 0, 0)
    const = lambda i: (0, 0)
    nat_shape = jax.ShapeDtypeStruct((t, NA_WIDTH), BF16)
    tr_shape = jax.ShapeDtypeStruct((t // Q_TOK, NA_WIDTH, Q_TOK), BF16)
    in_specs = [pl.BlockSpec((tm, D_MODEL), tok),
                pl.BlockSpec((1, D_MODEL), const),
                pl.BlockSpec(w_nat.shape, const),
                pl.BlockSpec(w_tr.shape, const)]
    out_specs = ([pl.BlockSpec((tm, NA_WIDTH), tok)] * 3
                 + [pl.BlockSpec((tm // Q_TOK, NA_WIDTH, Q_TOK), blk)] * 3)
    return in_specs, out_specs, [nat_shape] * 3 + [tr_shape] * 3


def _in_proj(x2d, g_pre, w_nat, w_tr, tm=TOKEN_BLOCK):
    t = x2d.shape[0]
    in_specs, out_specs, out_shape = _in_proj_specs(t, tm, w_nat, w_tr)
    return pl.pallas_call(
        _in_proj_kernel,
        grid=(t // tm,),
        in_specs=in_specs,
        out_specs=out_specs,
        out_shape=out_shape,
        compiler_params=pltpu.CompilerParams(
            dimension_semantics=("arbitrary",), vmem_limit_bytes=VMEM_LIMIT),
        name="in_proj",
    )(x2d, g_pre, w_nat, w_tr)


def _bias_table_kernel(rpb_ref, t2_ref):
    head = pl.program_id(0)
    shape = (GRID_W, 2 * GRID_W)
    kc = lax.broadcasted_iota(jnp.int32, shape, 0)
    lane = lax.broadcasted_iota(jnp.int32, shape, 1)
    second = lane >= GRID_W
    qc = jnp.where(second, lane - GRID_W, lane)
    col_off = kc - qc + (WIN_COLS - 1)
    win_start = jnp.clip(qc - WIN_COLS // 2, 0, GRID_W - WIN_COLS)
    col_valid = (kc >= win_start) & (kc < win_start + WIN_COLS)
    n_off = 2 * WIN_COLS - 1
    n_row = 2 * WIN_ROWS - 1

    def entry(d, o):
        if 0 <= d < n_row:
            return rpb_ref[(head * n_row + d) * n_off + o]
        return jnp.float32(NEG_INF)

    for d in range(ROW_SLOTS):
        tile = jnp.full(shape, NEG_INF, F32)
        for o in range(n_off):
            val = jnp.where(second, entry(d - 1, o), entry(d, o))
            tile = jnp.where(col_off == o, val, tile)
        t2_ref[0, d] = jnp.where(col_valid, tile, NEG_INF)


def _bias_table(rpb):
    return pl.pallas_call(
        _bias_table_kernel,
        grid=(NA_HEADS,),
        in_specs=[pl.BlockSpec(memory_space=pltpu.SMEM)],
        out_specs=pl.BlockSpec((1, ROW_SLOTS, GRID_W, 2 * GRID_W), lambda h: (h, 0, 0, 0)),
        out_shape=jax.ShapeDtypeStruct((NA_HEADS, ROW_SLOTS, GRID_W, 2 * GRID_W), F32),
        compiler_params=pltpu.CompilerParams(dimension_semantics=("arbitrary",)),
        name="bias_table",
    )(rpb.astype(F32).reshape(-1))


def _token_range(refs, lo, hi, other, axis):
    pieces = []
    for idx, ref in enumerate(refs):
        a, b = max(lo, idx * Q_TOK), min(hi, (idx + 1) * Q_TOK)
        if a < b:
            tok = slice(a - idx * Q_TOK, b - idx * Q_TOK)
            pieces.append(ref[tok, other] if axis == 0 else ref[0, other, tok])
    return pieces[0] if len(pieces) == 1 else jnp.concatenate(pieces, axis=axis)


def _window_plan(variant, jp):
    if variant == 0:
        return [(i, i - 2 * jp + WIN_ROWS - 1, None) for i in range(WIN_ROWS)]
    if variant == 2:
        q_row = KEY_ROWS - Q_ROWS + 2 * jp
        return [(i, i - q_row + WIN_ROWS - 1, None) for i in range(KEY_ROWS - WIN_ROWS, KEY_ROWS)]
    q_row = Q_ROWS + 2 * jp
    plan = []
    for i in range(q_row - WIN_ROWS // 2, q_row + WIN_ROWS // 2 + 1):
        half = "low" if i == q_row - WIN_ROWS // 2 else "high" if i == q_row + WIN_ROWS // 2 else None
        plan.append((i, i - q_row + WIN_ROWS - 1, half))
    return plan


def _step_plan(position):
    plan = []
    for j in range(STEP_SUBS):
        if position == 0:
            plan.append((0, 0) if j == 0 else (1, (j - 1) * Q_ROWS))
        elif position == 1:
            plan.append((1, j * Q_ROWS))
        else:
            last = j == STEP_SUBS - 1
            plan.append((2 if last else 1, (j + 1 - last) * Q_ROWS))
    return tuple(plan)


def _attention_kernel(qt_ref, *refs, steps):
    k_refs, v_refs = refs[:KV_BLOCKS], refs[KV_BLOCKS:2 * KV_BLOCKS]
    t2_ref, out_ref = refs[2 * KV_BLOCKS:]
    step = pl.program_id(0)
    pair_lanes = 2 * NA_HEAD_DIM
    q_lanes = 2 * GRID_W
    n_pairs = NA_HEADS // 2

    def sub_group(sub, variant, key_off):
        plans = [_window_plan(variant, jp) for jp in range(Q_ROWS // 2)]
        key_lo = min(p[0][0] for p in plans)
        key_top = max(p[-1][0] for p in plans) + 1
        key_hi = key_top + (key_top - key_lo) % 2
        tok_lo, tok_top, tok_hi = ((key_off + r) * GRID_W for r in (key_lo, key_top, key_hi))
        lane = lax.broadcasted_iota(jnp.int32, (1, q_lanes), 1)
        half_mask = {"low": jnp.where(lane < GRID_W, 0.0, NEG_INF).astype(F32),
                     "high": jnp.where(lane >= GRID_W, 0.0, NEG_INF).astype(F32)}
        ones_rows = jnp.ones((16, tok_hi - tok_lo), BF16)
        zero_blk = jnp.zeros((GRID_W, q_lanes), BF16)

        def scores(pair, jp):
            cols = slice(pair * pair_lanes, (pair + 1) * pair_lanes)
            zero = jnp.zeros((NA_HEAD_DIM, q_lanes), BF16)
            plan = plans[jp]
            qt = qt_ref[sub, cols, jp * q_lanes:(jp + 1) * q_lanes]
            qt_both = jnp.concatenate(
                [jnp.concatenate([qt[:NA_HEAD_DIM], zero], axis=0),
                 jnp.concatenate([zero, qt[NA_HEAD_DIM:]], axis=0)], axis=1)
            row_lo, row_hi = plan[0][0], plan[-1][0] + 1
            k_rows = _token_range(k_refs, (key_off + row_lo) * GRID_W, (key_off + row_hi) * GRID_W, cols, 0)
            s = jnp.dot(k_rows, qt_both, preferred_element_type=F32)
            m_acc = [None, None]
            biased = [{}, {}]
            for i, slot_d, half in plan:
                rows = slice((i - row_lo) * GRID_W, (i - row_lo + 1) * GRID_W)
                for head_sub in range(2):
                    blk = s[rows, head_sub * q_lanes:(head_sub + 1) * q_lanes] + t2_ref[2 * pair + head_sub, slot_d]
                    if half is not None:
                        blk = blk + half_mask[half]
                    biased[head_sub][i] = blk
                    blk_max = jnp.max(blk.reshape(GRID_W // 8, 8, q_lanes), axis=0)
                    m_acc[head_sub] = blk_max if m_acc[head_sub] is None else jnp.maximum(m_acc[head_sub], blk_max)
            return [jnp.max(m, axis=0, keepdims=True) for m in m_acc], biased

        def probs(state):
            maxima, biased = state
            return [jnp.concatenate(
                [jnp.exp(biased[h][i] - maxima[h]).astype(BF16) if i in biased[h] else zero_blk
                 for i in range(key_lo, key_hi)], axis=0) for h in range(2)]

        def pv(pair, p_cols):
            for head_sub in range(2):
                head = 2 * pair + head_sub
                p_head = jnp.concatenate([c[head_sub] for c in p_cols], axis=1)
                rows = slice(head * NA_HEAD_DIM, (head + 1) * NA_HEAD_DIM)
                vt_head = _token_range(v_refs, tok_lo, tok_hi, rows, 1)
                vt_ext = jnp.concatenate([vt_head, ones_rows], axis=0)
                o = jnp.dot(vt_ext, p_head, preferred_element_type=F32)
                out_ref[sub, rows, :] = (o[:NA_HEAD_DIM] * (1.0 / o[NA_HEAD_DIM:NA_HEAD_DIM + 1])).astype(BF16)

        return scores, probs, pv

    def body(step_variant):
        fns = [sub_group(sub, *plan) for sub, plan in enumerate(_step_plan(step_variant))]
        n_jp = Q_ROWS // 2
        stages = [(sub, pair, jp) for sub in range(len(fns)) for pair in range(n_pairs) for jp in range(n_jp)]
        state = fns[0][0](0, 0)
        p_cols = []
        for idx, (sub, pair, jp) in enumerate(stages):
            nxt = None
            if idx + 1 < len(stages):
                nxt_sub, nxt_pair, nxt_jp = stages[idx + 1]
                nxt = fns[nxt_sub][0](nxt_pair, nxt_jp)
            p_cols.append(fns[sub][1](state))
            if jp == n_jp - 1:
                fns[sub][2](pair, p_cols)
                p_cols = []
            state = nxt

    is_top = step == 0
    is_bot = step == steps - 1
    pl.when(is_top)(lambda: body(0))
    pl.when(jnp.logical_not(is_top | is_bot))(lambda: body(1))
    pl.when(is_bot)(lambda: body(2))


def _attention(qt, k, vt, t2, batch, seq):
    t = batch * seq
    blocks = seq // Q_TOK
    steps = blocks // STEP_SUBS
    assert steps >= 2 and blocks >= KV_BLOCKS

    def first_kv(u):
        return jnp.clip(STEP_SUBS * u - 1, 0, blocks - KV_BLOCKS)

    def blk(u, b):
        return (b * steps + u, 0, 0)

    def kv(i):
        return lambda u, b: (b * blocks + first_kv(u) + i, 0)

    def kv_blk(i):
        return lambda u, b: (b * blocks + first_kv(u) + i, 0, 0)

    return pl.pallas_call(
        functools.partial(_attention_kernel, steps=steps),
        grid=(steps, batch),
        in_specs=[pl.BlockSpec((STEP_SUBS, NA_WIDTH, Q_TOK), blk)]
                 + [pl.BlockSpec((Q_TOK, NA_WIDTH), kv(i)) for i in range(KV_BLOCKS)]
                 + [pl.BlockSpec((1, NA_WIDTH, Q_TOK), kv_blk(i)) for i in range(KV_BLOCKS)]
                 + [pl.BlockSpec(t2.shape, lambda u, b: (0, 0, 0, 0))],
        out_specs=pl.BlockSpec((STEP_SUBS, NA_WIDTH, Q_TOK), blk),
        out_shape=jax.ShapeDtypeStruct((t // Q_TOK, NA_WIDTH, Q_TOK), BF16),
        compiler_params=pltpu.CompilerParams(
            dimension_semantics=("arbitrary", "arbitrary"),
            vmem_limit_bytes=VMEM_LIMIT),
        name="attention",
    )(qt, *([k] * KV_BLOCKS), *([vt] * KV_BLOCKS), t2)


DFT_CHUNK = 16
LANES = 128
STRIDED_CHUNKS = 2


def _to_lane_tiles(scr, x):
    for c in range(scr.shape[0]):
        scr[c] = x[:, c * LANES:(c + 1) * LANES]


def _from_lane_tiles(scr):
    return jnp.concatenate([scr[c] for c in range(scr.shape[0])], axis=1)


def _strided_rows(scr, start, size, stride):
    return jnp.concatenate([scr[c, pl.ds(start, size, stride=stride), :] for c in range(scr.shape[0])], axis=1)


def _store_strided_rows(scr, start, stride, x):
    for c in range(scr.shape[0]):
        scr[c, pl.ds(start, x.shape[0], stride=stride), :] = x[:, c * LANES:(c + 1) * LANES]


def _fourier_1_kernel(w_ref, u_ref, a_ref, *scratch, major, chunks):
    if not scratch:
        for c in range(chunks):
            rows = slice(c * DFT_CHUNK, (c + 1) * DFT_CHUNK)
            u2d = u_ref[0, :, rows, :].reshape(major * DFT_CHUNK, F_WIDTH)
            res = jnp.dot(w_ref[...], u2d, preferred_element_type=F32)
            a_ref[0, :, :, rows, :] = res.astype(BF16).reshape(major, 2, DFT_CHUNK, F_WIDTH)
        return
    in_w, out_w = scratch
    pairs = DFT_CHUNK // 2
    for c in range(chunks):
        rows = slice(c * DFT_CHUNK, (c + 1) * DFT_CHUNK)
        _to_lane_tiles(in_w, pltpu.bitcast(u_ref[0, :, rows, :].reshape(major * DFT_CHUNK, F_WIDTH), jnp.uint32))
        for j in range(pairs):
            rhs = pltpu.bitcast(_strided_rows(in_w, j, major, pairs), BF16)
            res = jnp.dot(w_ref[...], rhs, preferred_element_type=F32)
            _store_strided_rows(out_w, j, pairs, pltpu.bitcast(res.astype(BF16), jnp.uint32))
        a_ref[0, :, :, rows, :] = pltpu.bitcast(_from_lane_tiles(out_w), BF16).reshape(
            major, 2, DFT_CHUNK, F_WIDTH)


def _fourier_1(u4, w1, kron):
    batch, major, minor, _ = u4.shape
    chunks = minor // DFT_CHUNK if kron else STRIDED_CHUNKS
    tb = chunks * DFT_CHUNK
    scratch = [] if kron else [pltpu.VMEM((F_WIDTH // LANES, major * DFT_CHUNK // 2, LANES), jnp.uint32),
                               pltpu.VMEM((F_WIDTH // LANES, major * DFT_CHUNK, LANES), jnp.uint32)]
    return pl.pallas_call(
        functools.partial(_fourier_1_kernel, major=major, chunks=chunks),
        grid=(batch, minor // tb),
        in_specs=[pl.BlockSpec(w1.shape, lambda b, j: (0, 0)),
                  pl.BlockSpec((1, major, tb, F_WIDTH), lambda b, j: (b, 0, j, 0))],
        out_specs=pl.BlockSpec((1, major, 2, tb, F_WIDTH), lambda b, j: (b, 0, 0, j, 0)),
        out_shape=jax.ShapeDtypeStruct((batch, major, 2, minor, F_WIDTH), BF16),
        scratch_shapes=scratch,
        compiler_params=pltpu.CompilerParams(
            dimension_semantics=("arbitrary", "arbitrary"),
            vmem_limit_bytes=VMEM_LIMIT),
        name="fourier_1",
    )(w1, u4)


def _fourier_2_kernel(a_ref, m_ref, wc_ref, bf_ref, y_ref, x_scr, y_scr):
    for j in range(DFT_CHUNK):
        rhs = a_ref[0, j].reshape(2 * DFT_MINOR, F_WIDTH)
        x = jnp.dot(m_ref[j], rhs, preferred_element_type=F32)
        rows = slice(j * DFT_MINOR, (j + 1) * DFT_MINOR)
        x_scr[rows, :F_WIDTH] = x[:DFT_MINOR].astype(BF16)
        x_scr[rows, F_WIDTH:] = x[DFT_MINOR:].astype(BF16)
    y = jnp.dot(x_scr[...], wc_ref[...], preferred_element_type=F32) + bf_ref[...]
    pitch = DFT_CHUNK + 1
    for j in range(DFT_CHUNK):
        _store_strided_rows(y_scr, j, pitch, y[j * DFT_MINOR:(j + 1) * DFT_MINOR])
    packed = jnp.concatenate(
        [jnp.concatenate([y_scr[c, p * pitch:p * pitch + DFT_CHUNK, :] for p in range(DFT_MINOR)], axis=0)
         for c in range(y_scr.shape[0])], axis=1)
    y_ref[0] = packed.astype(BF16).reshape(DFT_MINOR, DFT_CHUNK, F_WIDTH)


def _fold_channel_dft_kernel(cs_ref, wf_ref, wc_ref):
    wc_ref[...] = jnp.dot(cs_ref[...], wf_ref[...], preferred_element_type=F32,
                          precision=lax.Precision.HIGHEST).astype(BF16)


def _fold_channel_dft(w_f):
    c = np.arange(F_GROUP_DIM)
    ang = 2.0 * np.pi * np.outer(c, c) / F_GROUP_DIM
    eye = np.eye(F_GROUPS)
    cs = np.concatenate([np.kron(eye, np.cos(ang)), np.kron(eye, np.sin(ang))], axis=0)
    return pl.pallas_call(
        _fold_channel_dft_kernel,
        out_shape=jax.ShapeDtypeStruct((2 * F_WIDTH, F_WIDTH), BF16),
        name="fold_channel_dft",
    )(jnp.asarray(cs, F32), w_f.astype(F32))


def _fourier_2(a5, m_tab, w_c, b_f):
    batch, major, _, minor, _ = a5.shape
    tr = DFT_CHUNK
    return pl.pallas_call(
        _fourier_2_kernel,
        grid=(batch, major // tr),
        in_specs=[pl.BlockSpec((1, tr, 2, minor, F_WIDTH), lambda b, r: (b, r, 0, 0, 0)),
                  pl.BlockSpec((tr, 2 * minor, 2 * minor), lambda b, r: (r, 0, 0)),
                  pl.BlockSpec(w_c.shape, lambda b, r: (0, 0)),
                  pl.BlockSpec((1, F_WIDTH), lambda b, r: (0, 0))],
        out_specs=pl.BlockSpec((1, minor, tr, F_WIDTH), lambda b, r: (b, 0, r, 0)),
        out_shape=jax.ShapeDtypeStruct((batch, minor, major, F_WIDTH), BF16),
        scratch_shapes=[pltpu.VMEM((tr * minor, 2 * F_WIDTH), BF16),
                        pltpu.VMEM((F_WIDTH // LANES, minor * (tr + 1), LANES), F32)],
        compiler_params=pltpu.CompilerParams(
            dimension_semantics=("arbitrary", "arbitrary"),
            vmem_limit_bytes=VMEM_LIMIT),
        name="fourier_2",
    )(a5, m_tab, w_c, b_f)


@functools.lru_cache(maxsize=None)
def _dft_constants(seq):
    major = seq // DFT_MINOR
    kron = major * DFT_CHUNK <= 256
    a = np.arange(major)
    ang1 = 2.0 * np.pi * np.outer(a, a) / major
    w1 = np.stack([np.cos(ang1), -np.sin(ang1)], axis=1).reshape(2 * major, major)
    w1 = np.kron(w1, np.eye(DFT_CHUNK if kron else 2))
    r = np.arange(major)[:, None, None]
    p = np.arange(DFT_MINOR)[None, :, None]
    b = np.arange(DFT_MINOR)[None, None, :]
    ang2 = 2.0 * np.pi * ((b * (r + major * p)) % seq) / seq
    scale = 1.0 / np.sqrt(seq * F_GROUP_DIM)
    e_re, e_im = np.cos(ang2) * scale, -np.sin(ang2) * scale
    m_tab = np.concatenate([np.concatenate([e_re, -e_im], axis=2),
                            np.concatenate([e_im, e_re], axis=2)], axis=1)
    return kron, np.asarray(w1, np.float32), np.asarray(m_tab, np.float32)


def _fourier_fused_kernel(w1_ref, u_ref, m_ref, wc_ref, bf_ref, y_ref, a_scr, *scratch, major, chunks):
    _fourier_1_kernel(w1_ref, u_ref, a_scr, major=major, chunks=chunks)
    _fourier_2_kernel(a_scr, m_ref, wc_ref, bf_ref, y_ref, *scratch)


def _fourier_fused(u4, w1, m_tab, w_c, b_f):
    batch, major, minor, _ = u4.shape
    const2 = lambda b: (0, 0)
    return pl.pallas_call(
        functools.partial(_fourier_fused_kernel, major=major, chunks=minor // DFT_CHUNK),
        grid=(batch,),
        in_specs=[pl.BlockSpec(w1.shape, const2),
                  pl.BlockSpec((1, major, minor, F_WIDTH), lambda b: (b, 0, 0, 0)),
                  pl.BlockSpec(m_tab.shape, lambda b: (0, 0, 0)),
                  pl.BlockSpec(w_c.shape, const2),
                  pl.BlockSpec((1, F_WIDTH), const2)],
        out_specs=pl.BlockSpec((1, minor, major, F_WIDTH), lambda b: (b, 0, 0, 0)),
        out_shape=jax.ShapeDtypeStruct((batch, minor, major, F_WIDTH), BF16),
        scratch_shapes=[pltpu.VMEM((1, major, 2, minor, F_WIDTH), BF16),
                        pltpu.VMEM((major * minor, 2 * F_WIDTH), BF16),
                        pltpu.VMEM((F_WIDTH // LANES, minor * (major + 1), LANES), F32)],
        compiler_params=pltpu.CompilerParams(
            dimension_semantics=("arbitrary",), vmem_limit_bytes=VMEM_LIMIT),
        name="fourier_fused",
    )(w1, u4, m_tab, w_c, b_f)


def _fourier(u, w_c, b_f, batch, seq):
    major = seq // DFT_MINOR
    kron, w1, m_tab = _dft_constants(seq)
    u4 = u.reshape(batch, major, DFT_MINOR, F_WIDTH)
    w1, m_tab = jnp.asarray(w1).astype(BF16), jnp.asarray(m_tab).astype(BF16)
    if kron and major == DFT_CHUNK:
        return _fourier_fused(u4, w1, m_tab, w_c, b_f).reshape(batch * seq, F_WIDTH)
    a = _fourier_1(u4, w1, kron)
    y = _fourier_2(a, m_tab, w_c, b_f)
    return y.reshape(batch * seq, F_WIDTH)


def _out_proj_kernel(ot_ref, zat_ref, yf_ref, zf_ref, x_ref, wa_ref, wf_ref,
                     gna_ref, gf_ref, gp_ref, o_ref):
    blocks = range(ot_ref.shape[0])
    y_t = jnp.concatenate([ot_ref[i] for i in blocks], axis=1).astype(F32)
    z_t = jnp.concatenate([zat_ref[i] for i in blocks], axis=1)
    inv = lax.rsqrt(jnp.mean(y_t * y_t, axis=0, keepdims=True) + RMS_EPS)
    mixed_at = _gate(y_t * inv * gna_ref[...], z_t)
    mixed_f = _gate(_rms(yf_ref[...].astype(F32), gf_ref[...]), zf_ref[...])
    out = lax.dot_general(mixed_at, wa_ref[...], (((0,), (0,)), ((), ())),
                          preferred_element_type=F32)
    out = out + jnp.dot(mixed_f, wf_ref[...], preferred_element_type=F32)
    o_ref[...] = x_ref[...] + _rms(out, gp_ref[...])


def _out_proj_specs(t, tm, w_a, w_fo):
    tok = lambda i: (i, 0)
    blk = lambda i: (i, 0, 0)
    const = lambda i: (0, 0)
    in_specs = [pl.BlockSpec((tm // Q_TOK, NA_WIDTH, Q_TOK), blk),
                pl.BlockSpec((tm // Q_TOK, NA_WIDTH, Q_TOK), blk),
                pl.BlockSpec((tm, F_WIDTH), tok),
                pl.BlockSpec((tm, F_WIDTH), tok),
                pl.BlockSpec((tm, D_MODEL), tok),
                pl.BlockSpec(w_a.shape, const),
                pl.BlockSpec(w_fo.shape, const),
                pl.BlockSpec((NA_WIDTH, 1), const),
                pl.BlockSpec((1, F_WIDTH), const),
                pl.BlockSpec((1, D_MODEL), const)]
    return in_specs, pl.BlockSpec((tm, D_MODEL), tok), jax.ShapeDtypeStruct((t, D_MODEL), F32)


def _out_proj(o_t, za_t, y_f, z_f, x2d, w_a, w_fo, g_na_col, g_f, g_post, tm=TOKEN_BLOCK):
    t = x2d.shape[0]
    in_specs, out_spec, out_shape = _out_proj_specs(t, tm, w_a, w_fo)
    return pl.pallas_call(
        _out_proj_kernel,
        grid=(t // tm,),
        in_specs=in_specs,
        out_specs=out_spec,
        out_shape=out_shape,
        compiler_params=pltpu.CompilerParams(
            dimension_semantics=("arbitrary",), vmem_limit_bytes=VMEM_LIMIT),
        name="out_proj",
    )(o_t, za_t, y_f, z_f, x2d, w_a, w_fo, g_na_col, g_f, g_post)


def _in_out_proj_kernel(x_ref, g_ref, wn_ref, wt_ref,
                        ot_ref, zat_in_ref, yf_ref, zf_in_ref, xo_ref, wa_ref, wf_ref, gna_ref, gf_ref, gp_ref,
                        k_ref, uf_ref, zf_ref, qt_ref, vt_ref, zat_ref, o_ref):
    h = _rms(x_ref[...], g_ref[...]).astype(BF16)
    blocks = range(ot_ref.shape[0])
    y_t = jnp.concatenate([ot_ref[i] for i in blocks], axis=1).astype(F32)
    z_t = jnp.concatenate([zat_in_ref[i] for i in blocks], axis=1)
    nat = jnp.dot(h, wn_ref[...], preferred_element_type=F32)
    inv = lax.rsqrt(jnp.mean(y_t * y_t, axis=0, keepdims=True) + RMS_EPS)
    mixed_at = _gate(y_t * inv * gna_ref[...], z_t)
    mixed_f = _gate(_rms(yf_ref[...].astype(F32), gf_ref[...]), zf_in_ref[...])
    k_ref[...] = nat[:, 0 * NA_WIDTH:1 * NA_WIDTH].astype(BF16)
    uf_ref[...] = nat[:, 1 * NA_WIDTH:2 * NA_WIDTH].astype(BF16)
    zf_ref[...] = nat[:, 2 * NA_WIDTH:3 * NA_WIDTH].astype(BF16)
    out = lax.dot_general(mixed_at, wa_ref[...], (((0,), (0,)), ((), ())), preferred_element_type=F32)
    out = out + jnp.dot(mixed_f, wf_ref[...], preferred_element_type=F32)
    tr = lax.dot_general(wt_ref[...], h, (((1,), (1,)), ((), ())), preferred_element_type=F32)
    o_ref[...] = xo_ref[...] + _rms(out, gp_ref[...])
    for i in range(qt_ref.shape[0]):
        tok = slice(i * Q_TOK, (i + 1) * Q_TOK)
        qt_ref[i] = tr[0 * NA_WIDTH:1 * NA_WIDTH, tok].astype(BF16)
        vt_ref[i] = tr[1 * NA_WIDTH:2 * NA_WIDTH, tok].astype(BF16)
        zat_ref[i] = tr[2 * NA_WIDTH:3 * NA_WIDTH, tok].astype(BF16)


def _in_out_proj(in_args, out_args, tm=FUSED_TOKEN_BLOCK):
    x_in, _, w_nat, w_tr = in_args
    x_out, w_a, w_fo = out_args[4], out_args[5], out_args[6]
    t = x_in.shape[0]
    assert x_out.shape[0] == t
    in_specs_a, out_specs_a, out_shape_a = _in_proj_specs(t, tm, w_nat, w_tr)
    in_specs_b, out_spec_b, out_shape_b = _out_proj_specs(t, tm, w_a, w_fo)
    res = pl.pallas_call(
        _in_out_proj_kernel,
        grid=(t // tm,),
        in_specs=in_specs_a + in_specs_b,
        out_specs=out_specs_a + [out_spec_b],
        out_shape=out_shape_a + [out_shape_b],
        compiler_params=pltpu.CompilerParams(
            dimension_semantics=("arbitrary",), vmem_limit_bytes=VMEM_LIMIT),
        name="in_out_proj",
    )(*in_args, *out_args)
    return res[:N_IN_PROJ_OUTPUTS], res[-1]


def _mixers(proj, bias, w_c, b_f, batch, seq):
    k, u_f, z_f, q_t, v_t, za_t = proj
    o_t = _attention(q_t, k, v_t, bias, batch, seq)
    y_f = _fourier(u_f, w_c, b_f, batch, seq)
    return o_t, za_t, y_f, z_f


def kernel(x_prompt, x_sample, w_in, rpb, w_fourier, b_fourier, g_pre, g_na, g_f, w_out, g_post):
    depth = w_in.shape[0]
    y_prompt, y_sample = x_prompt, x_sample
    scale = NA_HEAD_DIM ** -0.5
    for l in range(depth):
        w = w_in[l]
        w_nat = jnp.concatenate([w[:, i * NA_WIDTH:(i + 1) * NA_WIDTH] for i in (1, 4, 5)],
                                axis=1).astype(BF16)
        w_tr = _transposed_weights(w, (0, 2, 3), scale)
        bias = _bias_table(rpb[l])
        w_c = _fold_channel_dft(w_fourier[l])
        row = lambda v: v.reshape(1, -1).astype(F32)
        b_f, g_in = row(b_fourier[l]), row(g_pre[l])
        out_w = (w_out[l][:NA_WIDTH].astype(BF16), w_out[l][NA_WIDTH:].astype(BF16),
                 g_na[l].reshape(-1, 1).astype(F32), row(g_f[l]), row(g_post[l]))
        (bp, sp, _), (bs, ss, _) = y_prompt.shape, y_sample.shape
        xp, xs = y_prompt.reshape(bp * sp, D_MODEL), y_sample.reshape(bs * ss, D_MODEL)

        mixed_p = _mixers(_in_proj(xp, g_in, w_nat, w_tr), bias, w_c, b_f, bp, sp)
        if xp.shape[0] == xs.shape[0]:
            proj_s, out_p = _in_out_proj((xs, g_in, w_nat, w_tr), (*mixed_p, xp, *out_w))
        else:
            proj_s = _in_proj(xs, g_in, w_nat, w_tr)
            out_p = _out_proj(*mixed_p, xp, *out_w)
        mixed_s = _mixers(proj_s, bias, w_c, b_f, bs, ss)
        out_s = _out_proj(*mixed_s, xs, *out_w)
        y_prompt, y_sample = out_p.reshape(bp, sp, D_MODEL), out_s.reshape(bs, ss, D_MODEL)
    return (y_prompt, y_sample)
```
